```python
import math
import jax
import jax.numpy as jnp
from jax import lax
import numpy as np

D_MODEL = 2048
BATCH = 2
SEQ = 4096
DEPTH = 2
DEC_BATCH = 8
DEC_SEQ = 4
PAST_LEN = 16384
PAGE_SIZE = 128

A_HEADS = 8
A_HEAD_DIM = 128
A_WIDTH = A_HEADS * A_HEAD_DIM
IDX_HEADS = 16
IDX_DIM = 64
TOPK_MAX = 256
QBLK = 64
REL_BUCKETS = 32
REL_MAX_DIST = 128
GLA_HEADS = 4
GLA_DK = 64
GLA_DV = 128
GLA_WIDTH = GLA_HEADS * GLA_DV
GLA_RANK = 16
GLA_TAU = 16.0
RET_HEADS = 4
RET_DK = 64
RET_DV = 128
RET_WIDTH = RET_HEADS * RET_DV
ROPE_BASE = 10000.0

CHUNK = 64
MIX_WIDTH = A_WIDTH + GLA_WIDTH + RET_WIDTH
D_FF = -(-8 * D_MODEL // (3 * 256)) * 256
EPS = 1e-6
IN_SIZES = (A_WIDTH, A_WIDTH, A_WIDTH, IDX_HEADS * IDX_DIM, IDX_DIM, IDX_HEADS,
            GLA_HEADS * GLA_DK, GLA_HEADS * GLA_DK, GLA_WIDTH, GLA_RANK, GLA_WIDTH,
            RET_HEADS * RET_DK, RET_HEADS * RET_DK, RET_WIDTH, RET_WIDTH, 3 * D_MODEL)
IN_TOTAL = sum(IN_SIZES)

kernel_name = 'hybrid_dsa_gla_retnet_step'


def rmsnorm(x, g):
    xf = x.astype(jnp.float32)
    y = xf * lax.rsqrt(jnp.mean(xf * xf, axis=-1, keepdims=True) + EPS)
    return (y * g.astype(jnp.float32)).astype(x.dtype)


def groupnorm_heads(x, g):
    xf = x.astype(jnp.float32)
    xc = xf - jnp.mean(xf, axis=-1, keepdims=True)
    y = xc * lax.rsqrt(jnp.mean(xc * xc, axis=-1, keepdims=True) + EPS)
    return (y * g.astype(jnp.float32)).astype(x.dtype)


def rope(x, pos):
    half = x.shape[-1] // 2
    freqs = ROPE_BASE ** (-jnp.arange(half, dtype=jnp.float32) / half)
    ang = pos.astype(jnp.float32)[:, None] * freqs[None, :]
    cos = jnp.cos(ang)[None, :, None, :]
    sin = jnp.sin(ang)[None, :, None, :]
    xf = x.astype(jnp.float32)
    x1, x2 = xf[..., :half], xf[..., half:]
    return jnp.concatenate([x1 * cos - x2 * sin, x1 * sin + x2 * cos], axis=-1).astype(x.dtype)


def rel_bucket(dist):
    n = jnp.maximum(dist, 0)
    max_exact = REL_BUCKETS // 2
    nf = jnp.maximum(n, 1).astype(jnp.float32)
    large = max_exact + (jnp.log(nf / max_exact) / math.log(REL_MAX_DIST / max_exact)
                         * (REL_BUCKETS - max_exact)).astype(jnp.int32)
    large = jnp.minimum(large, REL_BUCKETS - 1)
    return jnp.where(n < max_exact, n, large)


def split_columns(z):
    points = []
    acc = 0
    for s in IN_SIZES[:-1]:
        acc += s
        points.append(acc)
    return jnp.split(z, points, axis=-1)


def gather_rows(rows, ids):
    return jax.vmap(lambda r, i: r[i])(rows, ids)


def index_select(qi, wi, ki, q_pos, topk):
    s = jax.nn.relu(jnp.einsum('bqhd,bkd->bqhk', qi, ki))
    score = jnp.einsum('bqhk,bqh->bqk', s, wi).astype(jnp.float32)
    k_pos = jnp.arange(ki.shape[1])
    admissible = k_pos[None, :] <= q_pos[:, None]
    score = jnp.where(admissible[None], score, -jnp.inf)
    _, idx = lax.top_k(score, topk)
    valid = idx <= q_pos[None, :, None]
    return idx, valid


def sparse_attend(q, k_sel, v_sel, idx, valid, q_pos, rel_table):
    hd = q.shape[-1]
    logits = jnp.einsum('bqhd,bqkhd->bhqk', q, k_sel).astype(jnp.float32) * (hd ** -0.5)
    bias = rel_table[rel_bucket(q_pos[None, :, None] - idx)]
    logits = logits + bias.astype(jnp.float32).transpose(0, 3, 1, 2)
    logits = jnp.where(valid[:, None], logits, -jnp.inf)
    p = jax.nn.softmax(logits, axis=-1).astype(v_sel.dtype)
    return jnp.einsum('bhqk,bqkhd->bqhd', p, v_sel)


def dsa_prompt(q, k, v, qi, wi, ki, rel_table):
    B, T, H, hd = q.shape
    topk = min(TOPK_MAX, T // 4)

    def block(i):
        s0 = i * QBLK
        q_pos = s0 + jnp.arange(QBLK)
        sl = lambda a: lax.dynamic_slice_in_dim(a, s0, QBLK, axis=1)
        idx, valid = index_select(sl(qi), sl(wi), ki, q_pos, topk)
        return sparse_attend(sl(q), gather_rows(k, idx), gather_rows(v, idx), idx, valid, q_pos, rel_table)

    out = lax.map(block, jnp.arange(T // QBLK))
    return out.transpose(1, 0, 2, 3, 4).reshape(B, T, H, hd)


def dsa_sample(q, k_new, v_new, qi, wi, ki_new, pool_k, pool_v, pool_ki, page_table, rel_table):
    B, Tn = q.shape[:2]
    past = page_table.shape[1] * PAGE_SIZE
    topk = min(TOPK_MAX, (past + Tn) // 4)
    q_pos = past + jnp.arange(Tn)
    ki_past = pool_ki[page_table].reshape(B, past, IDX_DIM)
    ki_all = jnp.concatenate([ki_past, ki_new], axis=1)
    idx, valid = index_select(qi, wi, ki_all, q_pos, topk)
    in_past = (idx < past)[..., None, None]
    pidx = jnp.minimum(idx, past - 1)
    phys = jax.vmap(lambda pt, i: pt[i])(page_table, pidx // PAGE_SIZE)
    row = pidx % PAGE_SIZE
    nidx = jnp.clip(idx - past, 0, Tn - 1)
    k_sel = jnp.where(in_past, pool_k[phys, row], gather_rows(k_new, nidx))
    v_sel = jnp.where(in_past, pool_v[phys, row], gather_rows(v_new, nidx))
    return sparse_attend(q, k_sel, v_sel, idx, valid, q_pos, rel_table)


def chunked_linear_attention(q, k, v, log_a, s0):
    B, T, H, DK = q.shape
    DV = v.shape[-1]
    c = CHUNK if T % CHUNK == 0 else T
    n = T // c

    def to_chunks(a):
        return a.astype(jnp.float32).reshape(B, n, c, H, a.shape[-1]).transpose(1, 0, 3, 2, 4)

    causal = jnp.tril(jnp.ones((c, c), dtype=bool))

    def step(S, blk):
        qb, kb, vb, ab = blk
        b = jnp.cumsum(ab, axis=2)
        o_inter = jnp.einsum('bhtd,bhde->bhte', qb * jnp.exp(b), S)
        diff = jnp.where(causal[None, None, :, :, None],
                         b[:, :, :, None, :] - b[:, :, None, :, :], -jnp.inf)
        scores = jnp.einsum('bhtd,bhsd,bhtsd->bhts', qb, kb, jnp.exp(diff))
        o_intra = jnp.einsum('bhts,bhse->bhte', scores, vb)
        b_end = b[:, :, -1:, :]
        S_new = (jnp.exp(b_end[:, :, 0, :, None]) * S
                 + jnp.einsum('bhsd,bhse->bhde', kb * jnp.exp(b_end - b), vb))
        return S_new, o_inter + o_intra

    s_final, o = lax.scan(step, s0.astype(jnp.float32),
                          (to_chunks(q), to_chunks(k), to_chunks(v), to_chunks(log_a)))
    o = o.transpose(1, 0, 3, 2, 4).reshape(B, T, H, DV).astype(q.dtype)
    return o, s_final


def trunk_layer(x, pos, s_gla0, s_ret0, past_cache, rel_table,
                w_in, a_q_norm, a_k_norm, gla_wa, gla_ba, gla_norm, ret_norm,
                w_branch, w_out, norm_mix, norm_ffn, w_ffn_in, w_ffn_out):
    B, T, _ = x.shape
    heads = lambda a, nh: a.reshape(B, T, nh, -1)
    h = rmsnorm(x, norm_mix)
    z = jnp.einsum('btd,de->bte', h, w_in)
    (aq, ak, av, iq, ik, iw, gq, gk, gv, ga, gg, rq, rk, rv, rg, gates) = split_columns(z)

    aq = rmsnorm(heads(aq, A_HEADS), a_q_norm)
    ak = rmsnorm(heads(ak, A_HEADS), a_k_norm)
    av = heads(av, A_HEADS)
    iq = heads(iq, IDX_HEADS)
    iw = iw * ((IDX_HEADS * IDX_DIM) ** -0.5)
    if past_cache is None:
        o_a = dsa_prompt(aq, ak, av, iq, iw, ik, rel_table)
    else:
        o_a = dsa_sample(aq, ak, av, iq, iw, ik, *past_cache, rel_table)

    log_a_gla = jax.nn.log_sigmoid(jnp.einsum('btr,re->bte', ga, gla_wa) + gla_ba).astype(jnp.float32) / GLA_TAU
    o_b, s_gla = chunked_linear_attention(heads(gq, GLA_HEADS) * (GLA_DK ** -0.5), heads(gk, GLA_HEADS),
                                          heads(gv, GLA_HEADS), heads(log_a_gla, GLA_HEADS), s_gla0)
    o_b = rmsnorm(o_b, gla_norm) * jax.nn.silu(heads(gg, GLA_HEADS))

    rq_r = rope(heads(rq, RET_HEADS), pos)
    rk_r = rope(heads(rk, RET_HEADS), pos) * (RET_DK ** -0.5)
    log_gamma = jnp.log1p(-jnp.exp2(-5.0 - jnp.arange(RET_HEADS, dtype=jnp.float32)))
    log_a_ret = jnp.broadcast_to(log_gamma[None, None, :, None], (B, T, RET_HEADS, RET_DK))
    o_c, s_ret = chunked_linear_attention(rq_r, rk_r, heads(rv, RET_HEADS), log_a_ret, s_ret0)
    o_c = groupnorm_heads(o_c, ret_norm) * jax.nn.silu(heads(rg, RET_HEADS))

    y_a = jnp.einsum('bte,ed->btd', o_a.reshape(B, T, A_WIDTH), w_branch[:A_WIDTH])
    y_b = jnp.einsum('bte,ed->btd', o_b.reshape(B, T, GLA_WIDTH), w_branch[A_WIDTH:A_WIDTH + GLA_WIDTH])
    y_c = jnp.einsum('bte,ed->btd', o_c.reshape(B, T, RET_WIDTH), w_branch[A_WIDTH + GLA_WIDTH:])
    g_a, g_b, g_c = jnp.split(jax.nn.sigmoid(gates), 3, axis=-1)
    merged = g_a * y_a + g_b * y_b + g_c * y_c
    x = x + jnp.einsum('btd,de->bte', merged, w_out)

    h2 = rmsnorm(x, norm_ffn)
    gate, up = jnp.split(jnp.einsum('btd,df->btf', h2, w_ffn_in), 2, axis=-1)
    x = x + jnp.einsum('btf,fd->btd', jax.nn.silu(gate) * up, w_ffn_out)
    return x, (ak, av, ik, s_gla, s_ret)


def setup_inputs(seed: int = 0) -> dict:
    key = jax.random.key(seed)
    ks = jax.random.split(key, 24)
    f32 = jnp.float32
    n_pages = PAST_LEN // PAGE_SIZE
    n_used = DEC_BATCH * n_pages
    n_pool = n_used + max(1, n_used // 4)

    def normal(k, shape, scale=1.0):
        return jax.random.normal(k, shape, f32) * scale

    def gain(k, shape):
        return 1.0 + normal(k, shape, 0.05)

    page_table = jax.random.permutation(ks[7], n_pool)[:n_used].reshape(DEC_BATCH, n_pages).astype(jnp.int32)
    return {
        'x_prompt': normal(ks[0], (BATCH, SEQ, D_MODEL)),
        'x_sample': normal(ks[1], (DEC_BATCH, DEC_SEQ, D_MODEL)),
        'cache_k': normal(ks[2], (DEPTH, n_pool, PAGE_SIZE, A_HEADS, A_HEAD_DIM)),
        'cache_v': normal(ks[3], (DEPTH, n_pool, PAGE_SIZE, A_HEADS, A_HEAD_DIM)),
        'cache_kidx': normal(ks[4], (DEPTH, n_pool, PAGE_SIZE, IDX_DIM)),
        'state_gla': normal(ks[5], (DEPTH, DEC_BATCH, GLA_HEADS, GLA_DK, GLA_DV)),
        'state_ret': normal(ks[6], (DEPTH, DEC_BATCH, RET_HEADS, RET_DK, RET_DV)),
        'page_table': page_table,
        'rel_table': normal(ks[8], (REL_BUCKETS, A_HEADS), 0.1),
        'w_in': normal(ks[9], (DEPTH, D_MODEL, IN_TOTAL), D_MODEL ** -0.5),
        'a_q_norm': gain(ks[10], (DEPTH, A_HEAD_DIM)),
        'a_k_norm': gain(ks[11], (DEPTH, A_HEAD_DIM)),
        'gla_wa': normal(ks[12], (DEPTH, GLA_RANK, GLA_HEADS * GLA_DK), GLA_RANK ** -0.5),
        'gla_ba': normal(ks[13], (DEPTH, GLA_HEADS * GLA_DK), 0.1),
        'gla_norm': gain(ks[14], (DEPTH, GLA_DV)),
        'ret_norm': gain(ks[15], (DEPTH, RET_DV)),
        'w_branch': normal(ks[16], (DEPTH, MIX_WIDTH, D_MODEL), A_WIDTH ** -0.5),
        'w_out': normal(ks[17], (DEPTH, D_MODEL, D_MODEL), D_MODEL ** -0.5),
        'norm_mix': gain(ks[18], (DEPTH, D_MODEL)),
        'norm_ffn': gain(ks[19], (DEPTH, D_MODEL)),
        'w_ffn_in': normal(ks[20], (DEPTH, D_MODEL, 2 * D_FF), D_MODEL ** -0.5),
        'w_ffn_out': normal(ks[21], (DEPTH, D_FF, D_MODEL), D_FF ** -0.5),
    }


def reference(x_prompt, x_sample, cache_k, cache_v, cache_kidx, state_gla, state_ret, page_table,
              rel_table, w_in, a_q_norm, a_k_norm, gla_wa, gla_ba, gla_norm, ret_norm,
              w_branch, w_out, norm_mix, norm_ffn, w_ffn_in, w_ffn_out):
    bp, tp = x_prompt.shape[:2]
    ts = x_sample.shape[1]
    past = page_table.shape[1] * PAGE_SIZE
    pos_p = jnp.arange(tp)
    pos_s = past + jnp.arange(ts)
    zero_gla = jnp.zeros((bp, GLA_HEADS, GLA_DK, GLA_DV), jnp.float32)
    zero_ret = jnp.zeros((bp, RET_HEADS, RET_DK, RET_DV), jnp.float32)
    xp, xs = x_prompt, x_sample
    rows_p = [[], [], [], [], []]
    rows_s = [[], [], [], [], []]
    for l in range(DEPTH):
        weights = (w_in[l], a_q_norm[l], a_k_norm[l], gla_wa[l], gla_ba[l], gla_norm[l], ret_norm[l],
                   w_branch[l], w_out[l], norm_mix[l], norm_ffn[l], w_ffn_in[l], w_ffn_out[l])
        xp, new_p = trunk_layer(xp, pos_p, zero_gla, zero_ret, None, rel_table, *weights)
        xs, new_s = trunk_layer(xs, pos_s, state_gla[l], state_ret[l],
                                (cache_k[l], cache_v[l], cache_kidx[l], page_table), rel_table, *weights)
        for lst, r in zip(rows_p, new_p):
            lst.append(r)
        for lst, r in zip(rows_s, new_s):
            lst.append(r)
    k_p, v_p, kidx_p, gla_p, ret_p = [jnp.stack(r) for r in rows_p]
    k_s, v_s, kidx_s, gla_s, ret_s = [jnp.stack(r) for r in rows_s]
    return (xp, xs, k_p, v_p, kidx_p, gla_p, ret_p, k_s, v_s, kidx_s, gla_s, ret_s)
```

```python
import functools
import math

import numpy as np
import jax
import jax.numpy as jnp
from jax import lax
from jax.experimental import pallas as pl
from jax.experimental.pallas import tpu as pltpu

D_MODEL = 2048
PAGE_SIZE = 128
A_HEADS = 8
A_HEAD_DIM = 128
A_WIDTH = A_HEADS * A_HEAD_DIM
IDX_HEADS = 16
IDX_DIM = 64
TOPK_MAX = 256
REL_BUCKETS = 32
REL_MAX_DIST = 128
GLA_HEADS = 4
GLA_DK = 64
GLA_DV = 128
GLA_WIDTH = GLA_HEADS * GLA_DV
GLA_RANK = 16
GLA_TAU = 16.0
RET_HEADS = 4
RET_DK = 64
RET_DV = 128
RET_WIDTH = RET_HEADS * RET_DV
ROPE_BASE = 10000.0
CHUNK = 64
MIX_WIDTH = A_WIDTH + GLA_WIDTH + RET_WIDTH
D_FF = -(-8 * D_MODEL // (3 * 256)) * 256
EPS = 1e-6
IN_SIZES = (A_WIDTH, A_WIDTH, A_WIDTH, IDX_HEADS * IDX_DIM, IDX_DIM, IDX_HEADS,
            GLA_HEADS * GLA_DK, GLA_HEADS * GLA_DK, GLA_WIDTH, GLA_RANK, GLA_WIDTH,
            RET_HEADS * RET_DK, RET_HEADS * RET_DK, RET_WIDTH, RET_WIDTH, 3 * D_MODEL)

COL_GATES = 0
COL_AQ = 3 * D_MODEL
COL_GV = COL_AQ + 4 * A_WIDTH
COL_GQ = COL_GV + 4 * GLA_WIDTH
COL_MISC = COL_GQ + 4 * 256
IN_PADDED = COL_MISC + 128
MISC_IW = IDX_DIM
MISC_GA = IDX_DIM + IDX_HEADS

LANES = 128
SAMPLE_ROWS = 16
VMEM_LIMIT = 56 * 1024 * 1024
INT_MIN = -2147483648
NEG_BIG = -1e30

BF16 = jnp.bfloat16
F32 = jnp.float32
NT_DIMS = (((1,), (1,)), ((), ()))
TN_DIMS = (((0,), (0,)), ((), ()))


def _params(n_axes):
    return pltpu.CompilerParams(dimension_semantics=("arbitrary",) * n_axes,
                                vmem_limit_bytes=VMEM_LIMIT)


def _dot(a, b):
    return jnp.dot(a, b, preferred_element_type=F32)


def _dot_nt(a, b):
    return lax.dot_general(a, b, NT_DIMS, preferred_element_type=F32)


def _dot_tn(a, b):
    return lax.dot_general(a, b, TN_DIMS, preferred_element_type=F32)


def _norm_matmul_kernel(x_ref, g_ref, w_ref, o_ref, hb_ref):
    @pl.when(pl.program_id(1) == 0)
    def _():
        x = x_ref[...]
        ms = jnp.mean(x * x, axis=-1, keepdims=True)
        hb_ref[...] = (x * lax.rsqrt(ms + EPS) * g_ref[...]).astype(BF16)

    o_ref[...] = _dot(hb_ref[...], w_ref[...])


def _norm_matmul(x, g, w, tm, tn):
    m, d = x.shape
    n = w.shape[1]
    return pl.pallas_call(
        _norm_matmul_kernel,
        grid=(m // tm, n // tn),
        in_specs=[pl.BlockSpec((tm, d), lambda i, j: (i, 0)),
                  pl.BlockSpec((1, d), lambda i, j: (0, 0)),
                  pl.BlockSpec((d, tn), lambda i, j: (0, j))],
        out_specs=pl.BlockSpec((tm, tn), lambda i, j: (i, j)),
        out_shape=jax.ShapeDtypeStruct((m, n), F32),
        scratch_shapes=[pltpu.VMEM((tm, d), BF16)],
        compiler_params=_params(2),
        name="in_proj",
    )(x, g, w)


def _norm_swiglu_kernel(x_ref, g_ref, wg_ref, wu_ref, o_ref, hb_ref):
    @pl.when(pl.program_id(1) == 0)
    def _():
        x = x_ref[...]
        ms = jnp.mean(x * x, axis=-1, keepdims=True)
        hb_ref[...] = (x * lax.rsqrt(ms + EPS) * g_ref[...]).astype(BF16)

    h = hb_ref[...]
    gate = _dot(h, wg_ref[...])
    up = _dot(h, wu_ref[...])
    o_ref[...] = (gate * jax.nn.sigmoid(gate) * up).astype(BF16)


def _norm_swiglu(x, g, w, tm, tf):
    m, d = x.shape
    f = w.shape[1] // 2
    nf = f // tf
    return pl.pallas_call(
        _norm_swiglu_kernel,
        grid=(m // tm, nf),
        in_specs=[pl.BlockSpec((tm, d), lambda i, j: (i, 0)),
                  pl.BlockSpec((1, d), lambda i, j: (0, 0)),
                  pl.BlockSpec((d, tf), lambda i, j: (0, j)),
                  pl.BlockSpec((d, tf), lambda i, j: (0, j + nf))],
        out_specs=pl.BlockSpec((tm, tf), lambda i, j: (i, j)),
        out_shape=jax.ShapeDtypeStruct((m, f), BF16),
        scratch_shapes=[pltpu.VMEM((tm, d), BF16)],
        compiler_params=_params(2),
        name="swiglu",
    )(x, g, w, w)


def _matmul_residual_kernel(a_ref, w_ref, r_ref, o_ref):
    o_ref[...] = r_ref[...] + _dot(a_ref[...], w_ref[...])


def _matmul_residual(a, w, r, tm, tn, name):
    m, k = a.shape
    n = w.shape[1]
    return pl.pallas_call(
        _matmul_residual_kernel,
        grid=(m // tm, n // tn),
        in_specs=[pl.BlockSpec((tm, k), lambda i, j: (i, 0)),
                  pl.BlockSpec((k, tn), lambda i, j: (0, j)),
                  pl.BlockSpec((tm, tn), lambda i, j: (i, j))],
        out_specs=pl.BlockSpec((tm, tn), lambda i, j: (i, j)),
        out_shape=jax.ShapeDtypeStruct((m, n), F32),
        compiler_params=_params(2),
        name=name,
    )(a, w, r)


def _merge_kernel(oa_ref, ob_ref, oc_ref, ga_ref, gb_ref, gc_ref, wa_ref, wb_ref, wc_ref, o_ref):
    ya = _dot(oa_ref[...], wa_ref[...])
    yb = _dot(ob_ref[...], wb_ref[...])
    yc = _dot(oc_ref[...], wc_ref[...])
    merged = (jax.nn.sigmoid(ga_ref[...]) * ya + jax.nn.sigmoid(gb_ref[...]) * yb
              + jax.nn.sigmoid(gc_ref[...]) * yc)
    o_ref[...] = merged.astype(BF16)


def _merge(oa, ob, oc, z, w_branch, tm, tn):
    m = oa.shape[0]
    n = D_MODEL
    nb = n // tn
    a_blocks = A_WIDTH // GLA_WIDTH
    return pl.pallas_call(
        _merge_kernel,
        grid=(m // tm, nb),
        in_specs=[pl.BlockSpec((tm, A_WIDTH), lambda i, j: (i, 0)),
                  pl.BlockSpec((tm, GLA_WIDTH), lambda i, j: (i, 0)),
                  pl.BlockSpec((tm, RET_WIDTH), lambda i, j: (i, 0)),
                  pl.BlockSpec((tm, tn), lambda i, j: (i, j)),
                  pl.BlockSpec((tm, tn), lambda i, j: (i, j + nb)),
                  pl.BlockSpec((tm, tn), lambda i, j: (i, j + 2 * nb)),
                  pl.BlockSpec((A_WIDTH, tn), lambda i, j: (0, j)),
                  pl.BlockSpec((GLA_WIDTH, tn), lambda i, j: (a_blocks, j)),
                  pl.BlockSpec((RET_WIDTH, tn), lambda i, j: (a_blocks + 1, j))],
        out_specs=pl.BlockSpec((tm, tn), lambda i, j: (i, j)),
        out_shape=jax.ShapeDtypeStruct((m, n), BF16),
        compiler_params=_params(2),
        name="merge",
    )(oa, ob, oc, z, z, z, w_branch, w_branch, w_branch)


def _head_rmsnorm(x, g):
    outs = []
    for h in range(A_HEADS):
        xh = x[:, h * A_HEAD_DIM:(h + 1) * A_HEAD_DIM]
        ms = jnp.mean(xh * xh, axis=-1, keepdims=True)
        outs.append(xh * lax.rsqrt(ms + EPS) * g)
    return jnp.concatenate(outs, axis=-1)


def _prep_a_kernel(aq_ref, ak_ref, av_ref, iq_ref, misc_ref, gq_ref, gk_ref,
                   qn_ref, kf_ref, kb_ref, vf_ref, vb_ref, iqm_ref, ik2_ref):
    qn_ref[...] = _head_rmsnorm(aq_ref[...], gq_ref[...]).astype(BF16)
    kn = _head_rmsnorm(ak_ref[...], gk_ref[...])
    kf_ref[...] = kn
    kb_ref[...] = kn.astype(BF16)
    v = av_ref[...]
    vf_ref[...] = v
    vb_ref[...] = v.astype(BF16)
    iq = iq_ref[...].astype(BF16)
    lane = lax.broadcasted_iota(jnp.int32, (iq.shape[0], LANES), 1)
    zero = jnp.zeros((iq.shape[0], LANES), BF16)
    for h in range(IDX_HEADS):
        pair = iq[:, (h // 2) * LANES:(h // 2 + 1) * LANES]
        keep = (lane < IDX_DIM) if h % 2 == 0 else (lane >= IDX_DIM)
        iqm_ref[:, h * LANES:(h + 1) * LANES] = jnp.where(keep, pair, zero)
    ik = misc_ref[...][:, :IDX_DIM].astype(BF16)
    ik2_ref[...] = jnp.concatenate([ik, ik], axis=-1)


def _prep_a(z, gq, gk, tm):
    m = z.shape[0]
    blk = COL_AQ // A_WIDTH
    wide = lambda c: pl.BlockSpec((tm, A_WIDTH), lambda i: (i, c))
    row = pl.BlockSpec((tm, A_WIDTH), lambda i: (i, 0))
    return pl.pallas_call(
        _prep_a_kernel,
        grid=(m // tm,),
        in_specs=[wide(blk), wide(blk + 1), wide(blk + 2), wide(blk + 3),
                  pl.BlockSpec((tm, LANES), lambda i: (i, COL_MISC // LANES)),
                  pl.BlockSpec((1, A_HEAD_DIM), lambda i: (0, 0)),
                  pl.BlockSpec((1, A_HEAD_DIM), lambda i: (0, 0))],
        out_specs=[row, row, row, row, row,
                   pl.BlockSpec((tm, IDX_HEADS * LANES), lambda i: (i, 0)),
                   pl.BlockSpec((tm, LANES), lambda i: (i, 0))],
        out_shape=[jax.ShapeDtypeStruct((m, A_WIDTH), BF16),
                   jax.ShapeDtypeStruct((m, A_WIDTH), F32),
                   jax.ShapeDtypeStruct((m, A_WIDTH), BF16),
                   jax.ShapeDtypeStruct((m, A_WIDTH), F32),
                   jax.ShapeDtypeStruct((m, A_WIDTH), BF16),
                   jax.ShapeDtypeStruct((m, IDX_HEADS * LANES), BF16),
                   jax.ShapeDtypeStruct((m, LANES), BF16)],
        compiler_params=_params(1),
        name="prep_a",
    )(z, z, z, z, z, gq, gk)


def _sortable_key(score):
    bits = pltpu.bitcast(score, jnp.int32)
    return bits ^ ((bits >> 31) & jnp.int32(0x7FFFFFFF))


def _topk_threshold(count_ge, rows, topk):
    def body(it, ans_u):
        bit = lax.shift_left(jnp.int32(1), jnp.int32(31) - it)
        cand_u = ans_u | bit
        cnt = count_ge(cand_u ^ jnp.int32(INT_MIN))
        return jnp.where(cnt >= float(topk), cand_u, ans_u)

    ans_u = lax.fori_loop(0, 32, body, jnp.zeros((rows, 1), jnp.int32))
    return jnp.maximum(ans_u ^ jnp.int32(INT_MIN), jnp.int32(INT_MIN + 1))


def _a_prompt_kernel(cfar_ref, qn_ref, iqm_ref, mq_ref, kb_ref, vb_ref, ik2_ref, band_ref,
                     o_ref, s_scr, w_scr, acc_scr, m_scr, l_scr, *, topk):
    i = pl.program_id(1)
    tq = qn_ref.shape[1]
    kt = tq
    scale = A_HEAD_DIM ** -0.5
    iw_scale = (IDX_HEADS * IDX_DIM) ** -0.5

    iw = mq_ref[0][:, MISC_IW:MISC_IW + IDX_HEADS] * iw_scale
    for h in range(IDX_HEADS):
        w_scr[h] = jnp.broadcast_to(iw[:, h:h + 1], (tq, LANES))

    def score_tile(j, diagonal):
        k0 = pl.multiple_of(j * kt, kt)
        ik = ik2_ref[0, pl.ds(k0, kt), :]
        acc = jnp.zeros((tq, kt), F32)
        for h in range(IDX_HEADS):
            x = _dot_nt(iqm_ref[0, :, h * LANES:(h + 1) * LANES], ik)
            acc = acc + jnp.maximum(x, 0.0) * jnp.tile(w_scr[h], (1, kt // LANES))
        key = _sortable_key(acc)
        if diagonal:
            r = lax.broadcasted_iota(jnp.int32, (tq, kt), 0)
            c = lax.broadcasted_iota(jnp.int32, (tq, kt), 1)
            key = jnp.where(c <= r, key, jnp.int32(INT_MIN))
        s_scr[:, pl.ds(k0, kt)] = key

    def score_body(j, carry):
        score_tile(j, False)
        return carry

    lax.fori_loop(0, i, score_body, 0)
    score_tile(i, True)

    def count_ge(t):
        def body(j, cnt):
            k0 = pl.multiple_of(j * kt, kt)
            hit = jnp.where(s_scr[:, pl.ds(k0, kt)] >= t, jnp.int32(1), jnp.int32(0))
            part = hit[:, :LANES]
            for c in range(1, kt // LANES):
                part = part + hit[:, c * LANES:(c + 1) * LANES]
            return cnt + part
        cnt = lax.fori_loop(0, i + 1, body, jnp.zeros((tq, LANES), jnp.int32))
        return jnp.sum(cnt.astype(F32), axis=-1, keepdims=True)

    thr = _topk_threshold(count_ge, tq, topk)

    m_scr[...] = jnp.full(m_scr.shape, NEG_BIG, F32)
    l_scr[...] = jnp.zeros(l_scr.shape, F32)
    acc_scr[...] = jnp.zeros(acc_scr.shape, F32)

    def attn_tile(j, band_off):
        k0 = pl.multiple_of(j * kt, kt)
        sel = s_scr[:, pl.ds(k0, kt)] >= thr
        for h in range(A_HEADS):
            hs = slice(h * A_HEAD_DIM, (h + 1) * A_HEAD_DIM)
            lg = _dot_nt(qn_ref[0, :, hs], kb_ref[0, pl.ds(k0, kt), hs]) * scale
            if band_off is None:
                lg = lg + cfar_ref[h]
            else:
                lg = lg + band_ref[h, :, band_off:band_off + kt]
            lg = jnp.where(sel, lg, NEG_BIG)
            m_old = m_scr[h]
            m_new = jnp.maximum(m_old, jnp.max(lg, axis=-1, keepdims=True))
            alpha = jnp.exp(m_old - m_new)
            p = jnp.exp(lg - m_new)
            l_scr[h] = alpha * l_scr[h] + jnp.sum(p, axis=-1, keepdims=True)
            acc_scr[:, hs] = alpha * acc_scr[:, hs] + _dot(p.astype(BF16), vb_ref[0, pl.ds(k0, kt), hs])
            m_scr[h] = m_new

    def far_body(j, carry):
        attn_tile(j, None)
        return carry

    lax.fori_loop(0, jnp.maximum(i - 1, 0), far_body, 0)

    @pl.when(i >= 1)
    def _():
        attn_tile(i - 1, 0)

    attn_tile(i, kt)

    for h in range(A_HEADS):
        hs = slice(h * A_HEAD_DIM, (h + 1) * A_HEAD_DIM)
        o_ref[0, :, hs] = (acc_scr[:, hs] / l_scr[h]).astype(BF16)


def _rel_bucket(dist):
    n = jnp.maximum(dist, 0)
    max_exact = REL_BUCKETS // 2
    nf = jnp.maximum(n, 1).astype(F32)
    large = max_exact + (jnp.log(nf / max_exact) / math.log(REL_MAX_DIST / max_exact)
                         * (REL_BUCKETS - max_exact)).astype(jnp.int32)
    large = jnp.minimum(large, REL_BUCKETS - 1)
    return jnp.where(n < max_exact, n, large)


def _bias_band(rel_table, rows, kt):
    d = jnp.arange(rows)[:, None] + kt - jnp.arange(2 * kt)[None, :]
    return rel_table[_rel_bucket(d)].transpose(2, 0, 1).astype(F32)


def _a_prompt(qn, iqm, z, kb, vb, ik2, band, cfar, batch, seq, tq):
    topk = min(TOPK_MAX, seq // 4)
    r3 = lambda a: a.reshape(batch, seq, a.shape[-1])
    qblk = lambda w: pl.BlockSpec((1, tq, w), lambda b, i: (b, i, 0))
    full = lambda w: pl.BlockSpec((1, seq, w), lambda b, i: (b, 0, 0), pipeline_mode=pl.Buffered(1))
    out = pl.pallas_call(
        functools.partial(_a_prompt_kernel, topk=topk),
        grid=(batch, seq // tq),
        in_specs=[pl.BlockSpec(memory_space=pltpu.SMEM),
                  qblk(A_WIDTH), qblk(IDX_HEADS * LANES),
                  pl.BlockSpec((1, tq, LANES), lambda b, i: (b, i, COL_MISC // LANES)),
                  full(A_WIDTH), full(A_WIDTH), full(LANES),
                  pl.BlockSpec((A_HEADS, tq, 2 * tq), lambda b, i: (0, 0, 0),
                               pipeline_mode=pl.Buffered(1))],
        out_specs=qblk(A_WIDTH),
        scratch_shapes=[pltpu.VMEM((tq, seq), jnp.int32),
                        pltpu.VMEM((IDX_HEADS, tq, LANES), F32),
                        pltpu.VMEM((tq, A_WIDTH), F32),
                        pltpu.VMEM((A_HEADS, tq, 1), F32),
                        pltpu.VMEM((A_HEADS, tq, 1), F32)],
        out_shape=jax.ShapeDtypeStruct((batch, seq, A_WIDTH), BF16),
        compiler_params=_params(2),
        name="a_prompt",
    )(cfar, r3(qn), r3(iqm), r3(z), r3(kb), r3(vb), r3(ik2), band)
    return out.reshape(batch * seq, A_WIDTH)


def _a_sample_score_kernel(pt_ref, iq_ref, w_ref, mnew_ref, *rest, pages_per_step, n_pages, topk):
    page_refs = rest[:pages_per_step]
    keys_ref, thr_ref = rest[pages_per_step:]
    g = pl.program_id(1)
    rows = SAMPLE_ROWS
    iq = iq_ref[0]
    w = w_ref[0]

    def scores(ik):
        x = jnp.maximum(_dot_nt(iq, ik), 0.0) * w
        return jnp.sum(x.reshape(IDX_HEADS, rows, PAGE_SIZE), axis=0)

    for p in range(pages_per_step):
        off = pl.multiple_of((g * pages_per_step + p) * PAGE_SIZE, PAGE_SIZE)
        keys_ref[0, :, pl.ds(off, PAGE_SIZE)] = _sortable_key(scores(page_refs[p][...].astype(BF16)))

    @pl.when(g == pl.num_programs(1) - 1)
    def _():
        past = n_pages * PAGE_SIZE
        ik_new = mnew_ref[0][:, :IDX_DIM].astype(BF16)
        ik_new = jnp.concatenate([ik_new, jnp.zeros((PAGE_SIZE - rows, IDX_DIM), BF16)], axis=0)
        r = lax.broadcasted_iota(jnp.int32, (rows, PAGE_SIZE), 0)
        c = lax.broadcasted_iota(jnp.int32, (rows, PAGE_SIZE), 1)
        keys_ref[0, :, past:past + PAGE_SIZE] = jnp.where(c <= r, _sortable_key(scores(ik_new)),
                                                          jnp.int32(INT_MIN))

        def count_ge(t):
            def body(j, cnt):
                off = pl.multiple_of(j * PAGE_SIZE, PAGE_SIZE)
                return cnt + jnp.where(keys_ref[0, :, pl.ds(off, PAGE_SIZE)] >= t,
                                       jnp.int32(1), jnp.int32(0))
            cnt = lax.fori_loop(0, n_pages + 1, body, jnp.zeros((rows, PAGE_SIZE), jnp.int32))
            return jnp.sum(cnt.astype(F32), axis=-1, keepdims=True)

        thr = _topk_threshold(count_ge, rows, topk)
        thr_ref[0] = jnp.broadcast_to(thr, (rows, LANES))


def _a_sample_scores(page_table, iq_hq, w_hq, z3, cache_kidx, layer, n_tok, pages_per_step):
    batch, n_pages = page_table.shape
    rows = SAMPLE_ROWS
    nk = (n_pages + 1) * PAGE_SIZE
    topk = min(TOPK_MAX, (n_pages * PAGE_SIZE + n_tok) // 4)

    def page_spec(p):
        return pl.BlockSpec((None, None, PAGE_SIZE, IDX_DIM),
                            lambda b, g, pt: (layer, pt[b, g * pages_per_step + p], 0, 0))

    return pl.pallas_call(
        functools.partial(_a_sample_score_kernel, pages_per_step=pages_per_step, n_pages=n_pages,
                          topk=topk),
        grid_spec=pltpu.PrefetchScalarGridSpec(
            num_scalar_prefetch=1,
            grid=(batch, n_pages // pages_per_step),
            in_specs=[pl.BlockSpec((1, IDX_HEADS * rows, IDX_DIM), lambda b, g, pt: (b, 0, 0)),
                      pl.BlockSpec((1, IDX_HEADS * rows, 1), lambda b, g, pt: (b, 0, 0)),
                      pl.BlockSpec((1, rows, LANES), lambda b, g, pt: (b, 0, COL_MISC // LANES))]
                     + [page_spec(p) for p in range(pages_per_step)],
            out_specs=[pl.BlockSpec((1, rows, nk), lambda b, g, pt: (b, 0, 0)),
                       pl.BlockSpec((1, rows, LANES), lambda b, g, pt: (b, 0, 0))]),
        out_shape=[jax.ShapeDtypeStruct((batch, rows, nk), jnp.int32),
                   jax.ShapeDtypeStruct((batch, rows, LANES), jnp.int32)],
        compiler_params=_params(2),
        name="a_sample_scores",
    )(page_table, iq_hq, w_hq, z3, *([cache_kidx] * pages_per_step))


def _a_sample_attn_kernel(pt_ref, qall_ref, keys_ref, thr_ref, knew_ref, vnew_ref, cfar_ref, band_last_ref,
                          band_new_ref, expand_ref, *rest, pages_per_step, n_pages):
    k_refs = rest[:pages_per_step]
    v_refs = rest[pages_per_step:2 * pages_per_step]
    o_ref, acc_scr, m_scr, l_scr = rest[2 * pages_per_step:]
    g = pl.program_id(1)
    rows = SAMPLE_ROWS
    hq = A_HEADS * rows
    cols = PAGE_SIZE * A_HEADS
    scale = A_HEAD_DIM ** -0.5
    thr = thr_ref[0][:, :1]
    qall = qall_ref[0]

    @pl.when(g == 0)
    def _():
        m_scr[...] = jnp.full(m_scr.shape, NEG_BIG, F32)
        l_scr[...] = jnp.zeros(l_scr.shape, F32)
        acc_scr[...] = jnp.zeros(acc_scr.shape, F32)

    def attend(key_tile, expand, kf, vf, bias):
        n = expand.shape[1]
        sel = jnp.where(key_tile >= thr, 1.0, 0.0).astype(BF16)
        sel = _dot(sel, expand)
        sel = jnp.concatenate([sel] * A_HEADS, axis=0)
        same_head = (lax.broadcasted_iota(jnp.int32, (hq, n), 0) // rows
                     == lax.broadcasted_iota(jnp.int32, (hq, n), 1) % A_HEADS)
        ok = jnp.where(same_head, sel, 0.0) > 0.5
        lg = jnp.where(ok, _dot_nt(qall, kf) * scale + bias, NEG_BIG)
        m_old = m_scr[...]
        m_new = jnp.maximum(m_old, jnp.max(lg, axis=-1, keepdims=True))
        alpha = jnp.exp(m_old - m_new)
        p = jnp.exp(lg - m_new)
        l_scr[...] = alpha * l_scr[...] + jnp.sum(p, axis=-1, keepdims=True)
        acc_scr[...] = alpha * acc_scr[...] + _dot(p.astype(BF16), vf)
        m_scr[...] = m_new

    for p in range(pages_per_step):
        page = g * pages_per_step + p
        off = pl.multiple_of(page * PAGE_SIZE, PAGE_SIZE)
        bias = jnp.where(page == n_pages - 1, band_last_ref[...], cfar_ref[...])
        attend(keys_ref[0, :, pl.ds(off, PAGE_SIZE)], expand_ref[...],
               k_refs[p][...].reshape(cols, A_HEAD_DIM).astype(BF16),
               v_refs[p][...].reshape(cols, A_HEAD_DIM).astype(BF16), bias)

    @pl.when(g == pl.num_programs(1) - 1)
    def _():
        past = n_pages * PAGE_SIZE
        attend(keys_ref[0, :, past:past + PAGE_SIZE], expand_ref[:, :hq],
               knew_ref[0], vnew_ref[0], band_new_ref[...])
        o_ref[0] = (acc_scr[...] / l_scr[...]).astype(BF16)


def _a_sample_attn(page_table, qall, keys, thr, knew, vnew, cfar_rows, band_last, band_new, expand,
                   cache_k, cache_v, layer, pages_per_step):
    batch, n_pages = page_table.shape
    rows = SAMPLE_ROWS
    hq = A_HEADS * rows
    nk = keys.shape[-1]

    def page_spec(p):
        return pl.BlockSpec((None, None, PAGE_SIZE, A_HEADS, A_HEAD_DIM),
                            lambda b, g, pt: (layer, pt[b, g * pages_per_step + p], 0, 0, 0))

    per_b = lambda r, w: pl.BlockSpec((1, r, w), lambda b, g, pt: (b, 0, 0))
    const = lambda r, w: pl.BlockSpec((r, w), lambda b, g, pt: (0, 0))
    return pl.pallas_call(
        functools.partial(_a_sample_attn_kernel, pages_per_step=pages_per_step, n_pages=n_pages),
        grid_spec=pltpu.PrefetchScalarGridSpec(
            num_scalar_prefetch=1,
            grid=(batch, n_pages // pages_per_step),
            in_specs=[per_b(hq, A_HEAD_DIM), per_b(rows, nk), per_b(rows, LANES),
                      per_b(hq, A_HEAD_DIM), per_b(hq, A_HEAD_DIM),
                      const(hq, 1), const(hq, PAGE_SIZE * A_HEADS), const(hq, hq),
                      const(PAGE_SIZE, PAGE_SIZE * A_HEADS)]
                     + [page_spec(p) for p in range(pages_per_step)] * 2,
            out_specs=per_b(hq, A_HEAD_DIM),
            scratch_shapes=[pltpu.VMEM((hq, A_HEAD_DIM), F32),
                            pltpu.VMEM((hq, 1), F32),
                            pltpu.VMEM((hq, 1), F32)]),
        out_shape=jax.ShapeDtypeStruct((batch, hq, A_HEAD_DIM), BF16),
        compiler_params=_params(2),
        name="a_sample_attn",
    )(page_table, qall, keys, thr, knew, vnew, cfar_rows, band_last, band_new, expand,
      *([cache_k] * pages_per_step), *([cache_v] * pages_per_step))


def _pad_rows(x, rows):
    if x.shape[0] == rows:
        return x
    return jnp.concatenate([x, jnp.zeros((rows - x.shape[0], x.shape[1]), x.dtype)], axis=0)


def _split3(x):
    hi = x.astype(BF16)
    r1 = x - hi.astype(F32)
    mid = r1.astype(BF16)
    lo = (r1 - mid.astype(F32)).astype(BF16)
    return hi, mid, lo


def _gla_kernel(gq_ref, gk_ref, gv_ref, gg_ref, misc_ref, wa_ref, ba_ref, gn_ref, s0_ref,
                o_ref, sfin_ref, st_scr, oacc_scr, *, chunk, rows, n_valid):
    tt = gq_ref.shape[1]

    @pl.when(pl.program_id(1) == 0)
    def _():
        st_scr[...] = s0_ref[0]

    ga = _pad_rows(misc_ref[0][:, MISC_GA:MISC_GA + GLA_RANK], rows)
    x = _dot(ga.astype(BF16), wa_ref[...]) + ba_ref[...]
    la = (jnp.minimum(x, 0.0) - jnp.log1p(jnp.exp(-jnp.abs(x)))) * (1.0 / GLA_TAU)
    r = lax.broadcasted_iota(jnp.int32, (rows, rows), 0)
    c = lax.broadcasted_iota(jnp.int32, (rows, rows), 1)
    if n_valid < rows:
        rr = lax.broadcasted_iota(jnp.int32, la.shape, 0)
        la = jnp.where(rr < n_valid, la, 0.0)
    tri = jnp.where((c <= r) & (r // chunk == c // chunk), 1.0, 0.0).astype(BF16)
    hi, mid, lo = _split3(la)
    b = _dot(tri, hi) + _dot(tri, mid) + _dot(tri, lo)

    gq = _pad_rows(gq_ref[0], rows) * (GLA_DK ** -0.5)
    gk = _pad_rows(gk_ref[0], rows)
    if n_valid < rows:
        gk = jnp.where(rr < n_valid, gk, 0.0)
    gv = _pad_rows(gv_ref[0], rows)
    causal = (lax.broadcasted_iota(jnp.int32, (chunk, chunk), 1)
              <= lax.broadcasted_iota(jnp.int32, (chunk, chunk), 0))

    for ci in range(rows // chunk):
        rs = slice(ci * chunk, (ci + 1) * chunk)
        for h in range(GLA_HEADS):
            ks = slice(h * GLA_DK, (h + 1) * GLA_DK)
            vs = slice(h * GLA_DV, (h + 1) * GLA_DV)
            bc = b[rs, ks]
            bend = bc[chunk - 1:chunk, :]
            qt = (gq[rs, ks] * jnp.exp(bc)).astype(BF16)
            kt = (gk[rs, ks] * jnp.exp(-bc)).astype(BF16)
            kd = (gk[rs, ks] * jnp.exp(bend - bc)).astype(BF16)
            vh = gv[rs, vs].astype(BF16)
            st = st_scr[h]
            sc = jnp.where(causal, _dot_nt(qt, kt), 0.0)
            oacc_scr[rs, vs] = _dot(sc.astype(BF16), vh) + _dot_nt(qt, st.astype(BF16))
            st_scr[h] = st * jnp.exp(bend) + _dot_tn(vh, kd)

    gg = gg_ref[0]
    for h in range(GLA_HEADS):
        vs = slice(h * GLA_DV, (h + 1) * GLA_DV)
        oh = oacc_scr[0:tt, vs]
        ms = jnp.mean(oh * oh, axis=-1, keepdims=True)
        g = gg[:, vs]
        o_ref[0, :, vs] = (oh * lax.rsqrt(ms + EPS) * gn_ref[...] * (g * jax.nn.sigmoid(g))).astype(BF16)

    @pl.when(pl.program_id(1) == pl.num_programs(1) - 1)
    def _():
        sfin_ref[0] = st_scr[...]


def _gla(z3, wa, ba, gn, s0t, tt, chunk, rows, n_valid):
    batch, seq, _ = z3.shape
    q_blk = COL_GQ // 256
    v_blk = COL_GV // GLA_WIDTH
    tok = lambda w, cblk: pl.BlockSpec((1, tt, w), lambda b, t: (b, t, cblk))
    const = lambda shape: pl.BlockSpec(shape, lambda b, t: (0,) * len(shape))
    st_spec = pl.BlockSpec((1, GLA_HEADS, GLA_DV, GLA_DK), lambda b, t: (b, 0, 0, 0))
    return pl.pallas_call(
        functools.partial(_gla_kernel, chunk=chunk, rows=rows, n_valid=n_valid),
        grid=(batch, seq // tt),
        in_specs=[tok(256, q_blk), tok(256, q_blk + 1), tok(GLA_WIDTH, v_blk), tok(GLA_WIDTH, v_blk + 1),
                  tok(LANES, COL_MISC // LANES),
                  const((GLA_RANK, GLA_HEADS * GLA_DK)), const((1, GLA_HEADS * GLA_DK)),
                  const((1, GLA_DV)), st_spec],
        out_specs=[pl.BlockSpec((1, tt, GLA_WIDTH), lambda b, t: (b, t, 0)), st_spec],
        out_shape=[jax.ShapeDtypeStruct((batch, seq, GLA_WIDTH), BF16),
                   jax.ShapeDtypeStruct((batch, GLA_HEADS, GLA_DV, GLA_DK), F32)],
        scratch_shapes=[pltpu.VMEM((GLA_HEADS, GLA_DV, GLA_DK), F32),
                        pltpu.VMEM((rows, GLA_WIDTH), F32)],
        compiler_params=_params(2),
        name="gla",
    )(z3, z3, z3, z3, z3, wa, ba, gn, s0t)


def _ret_kernel(rq_ref, rk_ref, rv_ref, rg_ref, cos_ref, sin_ref, gn_ref, s0_ref,
                o_ref, sfin_ref, st_scr, oacc_scr, *, rows, n_valid, log_gamma):
    tt = rq_ref.shape[1]
    width = RET_HEADS * RET_DK

    @pl.when(pl.program_id(1) == 0)
    def _():
        st_scr[...] = s0_ref[0]

    lane = lax.broadcasted_iota(jnp.int32, (rows, width), 1)
    first_half = (lane % RET_DK) < (RET_DK // 2)
    cos = _pad_rows(cos_ref[...], rows)
    sin = _pad_rows(sin_ref[...], rows)

    def rope(x):
        rot = jnp.where(first_half, pltpu.roll(x, width - RET_DK // 2, axis=1),
                        pltpu.roll(x, RET_DK // 2, axis=1))
        return x * cos + rot * sin

    q = rope(_pad_rows(rq_ref[0], rows))
    k = rope(_pad_rows(rk_ref[0], rows)) * (RET_DK ** -0.5)
    v = _pad_rows(rv_ref[0], rows)
    rr = lax.broadcasted_iota(jnp.int32, (rows, RET_DK), 0)
    if n_valid < rows:
        k = jnp.where(lax.broadcasted_iota(jnp.int32, k.shape, 0) < n_valid, k, 0.0)
    t_i = lax.broadcasted_iota(jnp.int32, (rows, rows), 0)
    s_i = lax.broadcasted_iota(jnp.int32, (rows, rows), 1)
    dist = (t_i - s_i).astype(F32)
    pos1 = (rr + 1).astype(F32)
    rem = (n_valid - 1 - rr).astype(F32)

    for h in range(RET_HEADS):
        ks = slice(h * RET_DK, (h + 1) * RET_DK)
        vs = slice(h * RET_DV, (h + 1) * RET_DV)
        lg = log_gamma[h]
        decay = jnp.where(t_i >= s_i, jnp.exp(dist * lg), 0.0)
        qh = q[:, ks]
        kh = k[:, ks]
        vh = v[:, vs].astype(BF16)
        st = st_scr[h]
        sc = _dot_nt(qh.astype(BF16), kh.astype(BF16)) * decay
        q_in = (qh * jnp.exp(pos1 * lg)).astype(BF16)
        oacc_scr[:, vs] = _dot(sc.astype(BF16), vh) + _dot_nt(q_in, st.astype(BF16))
        kd = (kh * jnp.exp(rem * lg)).astype(BF16)
        st_scr[h] = st * math.exp(n_valid * lg) + _dot_tn(vh, kd)

    rg = rg_ref[0]
    for h in range(RET_HEADS):
        vs = slice(h * RET_DV, (h + 1) * RET_DV)
        oh = oacc_scr[0:tt, vs]
        oc = oh - jnp.mean(oh, axis=-1, keepdims=True)
        var = jnp.mean(oc * oc, axis=-1, keepdims=True)
        g = rg[:, vs]
        o_ref[0, :, vs] = (oc * lax.rsqrt(var + EPS) * gn_ref[...] * (g * jax.nn.sigmoid(g))).astype(BF16)

    @pl.when(pl.program_id(1) == pl.num_programs(1) - 1)
    def _():
        sfin_ref[0] = st_scr[...]


def _ret(z3, cos, sin, gn, s0t, tt, rows, n_valid):
    batch, seq, _ = z3.shape
    q_blk = COL_GQ // 256 + 2
    v_blk = COL_GV // RET_WIDTH + 2
    width = RET_HEADS * RET_DK
    log_gamma = tuple(float(np.log1p(-np.exp2(np.float32(-5.0 - h)), dtype=np.float32))
                      for h in range(RET_HEADS))
    tok = lambda w, cblk: pl.BlockSpec((1, tt, w), lambda b, t: (b, t, cblk))
    st_spec = pl.BlockSpec((1, RET_HEADS, RET_DV, RET_DK), lambda b, t: (b, 0, 0, 0))
    return pl.pallas_call(
        functools.partial(_ret_kernel, rows=rows, n_valid=n_valid, log_gamma=log_gamma),
        grid=(batch, seq // tt),
        in_specs=[tok(256, q_blk), tok(256, q_blk + 1), tok(RET_WIDTH, v_blk), tok(RET_WIDTH, v_blk + 1),
                  pl.BlockSpec((tt, width), lambda b, t: (t, 0)),
                  pl.BlockSpec((tt, width), lambda b, t: (t, 0)),
                  pl.BlockSpec((1, RET_DV), lambda b, t: (0, 0)), st_spec],
        out_specs=[pl.BlockSpec((1, tt, RET_WIDTH), lambda b, t: (b, t, 0)), st_spec],
        out_shape=[jax.ShapeDtypeStruct((batch, seq, RET_WIDTH), BF16),
                   jax.ShapeDtypeStruct((batch, RET_HEADS, RET_DV, RET_DK), F32)],
        scratch_shapes=[pltpu.VMEM((RET_HEADS, RET_DV, RET_DK), F32),
                        pltpu.VMEM((rows, RET_WIDTH), F32)],
        compiler_params=_params(2),
        name="ret",
    )(z3, z3, z3, z3, cos, sin, gn, s0t)


def _rope_tables(pos):
    half = RET_DK // 2
    freqs = ROPE_BASE ** (-jnp.arange(half, dtype=F32) / half)
    ang = pos.astype(F32)[:, None] * freqs[None, :]
    cos = jnp.cos(ang)
    sin = jnp.sin(ang)
    cos_t = jnp.tile(jnp.concatenate([cos, cos], axis=-1), (1, RET_HEADS))
    sin_t = jnp.tile(jnp.concatenate([-sin, sin], axis=-1), (1, RET_HEADS))
    return cos_t, sin_t


def _permute_w_in(w_in):
    parts = []
    acc = 0
    for s in IN_SIZES:
        parts.append(w_in[..., acc:acc + s])
        acc += s
    aq, ak, av, iq, ik, iw, gq, gk, gv, ga, gg, rq, rk, rv, rg, gates = parts
    pad = jnp.zeros(w_in.shape[:-1] + (LANES - IDX_DIM - IDX_HEADS - GLA_RANK,), w_in.dtype)
    return jnp.concatenate([gates, aq, ak, av, iq, gv, gg, rv, rg, gq, gk, rq, rk, ik, iw, ga, pad],
                           axis=-1).astype(BF16)


def _dense_tail(x, z, oa, ob, oc, w, tm):
    merged = _merge(oa, ob, oc, z, w["w_branch"], tm, 512)
    x = _matmul_residual(merged, w["w_out"], x, tm, 512, "out_proj")
    act = _norm_swiglu(x, w["norm_ffn"], w["w_ffn_in"], tm, 512)
    return _matmul_residual(act, w["w_ffn_out"], x, tm, 512, "ffn_out")


def _prompt_layer(x, w, batch, seq, band, cfar, cos, sin):
    tm = 512
    z = _norm_matmul(x, w["norm_mix"], w["w_in"], tm, 1920)
    qn, kf, kb, vf, vb, iqm, ik2 = _prep_a(z, w["a_q_norm"], w["a_k_norm"], tm)
    oa = _a_prompt(qn, iqm, z, kb, vb, ik2, band, cfar, batch, seq, 256)
    z3 = z.reshape(batch, seq, IN_PADDED)
    zero_state = jnp.zeros((batch, GLA_HEADS, GLA_DV, GLA_DK), F32)
    ob, s_gla = _gla(z3, w["gla_wa"], w["gla_ba"], w["gla_norm"], zero_state, 256, CHUNK, 256, 256)
    oc, s_ret = _ret(z3, cos, sin, w["ret_norm"], zero_state, 256, 256, 256)
    x = _dense_tail(x, z, oa, ob.reshape(-1, GLA_WIDTH), oc.reshape(-1, RET_WIDTH), w, tm)
    kidx = z3[:, :, COL_MISC:COL_MISC + IDX_DIM]
    new = (kf.reshape(batch, seq, A_HEADS, A_HEAD_DIM), vf.reshape(batch, seq, A_HEADS, A_HEAD_DIM),
           kidx, s_gla.transpose(0, 1, 3, 2), s_ret.transpose(0, 1, 3, 2))
    return x, new


def _sample_layer(x, w, layer, batch, n_tok, page_table, cache_k, cache_v, cache_kidx,
                  s_gla0, s_ret0, bias_s, expand, cos, sin):
    rows = SAMPLE_ROWS
    hq = A_HEADS * rows
    tm = batch * rows
    z = _norm_matmul(x, w["norm_mix"], w["w_in"], tm, 1920)
    qn, kf, kb, vf, vb, _, _ = _prep_a(z, w["a_q_norm"], w["a_k_norm"], tm)
    z3 = z.reshape(batch, rows, IN_PADDED)
    iq = z3[:, :, COL_AQ + 3 * A_WIDTH:COL_AQ + 4 * A_WIDTH].reshape(batch, rows, IDX_HEADS, IDX_DIM)
    iq_hq = iq.transpose(0, 2, 1, 3).reshape(batch, IDX_HEADS * rows, IDX_DIM).astype(BF16)
    iw = z3[:, :, COL_MISC + MISC_IW:COL_MISC + MISC_IW + IDX_HEADS] * ((IDX_HEADS * IDX_DIM) ** -0.5)
    w_hq = iw.transpose(0, 2, 1).reshape(batch, IDX_HEADS * rows, 1)
    keys, thr = _a_sample_scores(page_table, iq_hq, w_hq, z3, cache_kidx, layer, n_tok, 8)
    qall = qn.reshape(batch, rows, A_HEADS, A_HEAD_DIM).transpose(0, 2, 1, 3).reshape(batch, hq, A_HEAD_DIM)
    cfar_rows, band_last, band_new = bias_s
    oa = _a_sample_attn(page_table, qall, keys, thr, kb.reshape(batch, hq, A_HEAD_DIM),
                        vb.reshape(batch, hq, A_HEAD_DIM), cfar_rows, band_last, band_new, expand,
                        cache_k, cache_v, layer, 4)
    oa = oa.reshape(batch, A_HEADS, rows, A_HEAD_DIM).transpose(0, 2, 1, 3).reshape(tm, A_WIDTH)
    ob, s_gla = _gla(z3, w["gla_wa"], w["gla_ba"], w["gla_norm"], s_gla0.transpose(0, 1, 3, 2),
                     rows, LANES, LANES, n_tok)
    oc, s_ret = _ret(z3, cos, sin, w["ret_norm"], s_ret0.transpose(0, 1, 3, 2), rows, LANES, n_tok)
    x = _dense_tail(x, z, oa, ob.reshape(tm, GLA_WIDTH), oc.reshape(tm, RET_WIDTH), w, tm)
    kidx = z3[:, :n_tok, COL_MISC:COL_MISC + IDX_DIM]
    new = (kf.reshape(batch, rows, A_HEADS, A_HEAD_DIM)[:, :n_tok],
           vf.reshape(batch, rows, A_HEADS, A_HEAD_DIM)[:, :n_tok],
           kidx, s_gla.transpose(0, 1, 3, 2), s_ret.transpose(0, 1, 3, 2))
    return x, new


def _sample_bias(rel_table):
    rows = SAMPLE_ROWS
    hq = A_HEADS * rows
    band = _bias_band(rel_table, rows, PAGE_SIZE)
    cfar_rows = jnp.repeat(rel_table[REL_BUCKETS - 1].astype(F32), rows)[:, None]
    band_last = jnp.repeat(band[:, :, :PAGE_SIZE].reshape(hq, PAGE_SIZE), A_HEADS, axis=1)
    band_new = jnp.repeat(band[:, :, PAGE_SIZE:PAGE_SIZE + rows].reshape(hq, rows), A_HEADS, axis=1)
    return cfar_rows, band_last, band_new


def kernel(x_prompt, x_sample, cache_k, cache_v, cache_kidx, state_gla, state_ret, page_table, rel_table, w_in, a_q_norm, a_k_norm, gla_wa, gla_ba, gla_norm, ret_norm, w_branch, w_out, norm_mix, norm_ffn, w_ffn_in, w_ffn_out):
    depth = w_in.shape[0]
    bp, tp, d = x_prompt.shape
    bs, ts, _ = x_sample.shape
    past = page_table.shape[1] * PAGE_SIZE
    tq = 256
    row = lambda a: a.reshape(1, -1).astype(F32)

    cfar = rel_table[REL_BUCKETS - 1].astype(F32)
    band_p = _bias_band(rel_table, tq, tq)
    bias_s = _sample_bias(rel_table)
    expand = jnp.asarray(np.kron(np.eye(PAGE_SIZE), np.ones((1, A_HEADS))), dtype=BF16)
    cos_p, sin_p = _rope_tables(jnp.arange(tp))
    cos_s, sin_s = _rope_tables(past + jnp.arange(SAMPLE_ROWS))

    xp = x_prompt.reshape(bp * tp, d)
    xs = jnp.pad(x_sample, ((0, 0), (0, SAMPLE_ROWS - ts), (0, 0))).reshape(bs * SAMPLE_ROWS, d)
    rows_p, rows_s = [], []
    for l in range(depth):
        w = dict(w_in=_permute_w_in(w_in[l]), a_q_norm=row(a_q_norm[l]), a_k_norm=row(a_k_norm[l]),
                 gla_wa=gla_wa[l].astype(BF16), gla_ba=row(gla_ba[l]), gla_norm=row(gla_norm[l]),
                 ret_norm=row(ret_norm[l]), w_branch=w_branch[l].astype(BF16), w_out=w_out[l].astype(BF16),
                 norm_mix=row(norm_mix[l]), norm_ffn=row(norm_ffn[l]), w_ffn_in=w_ffn_in[l].astype(BF16),
                 w_ffn_out=w_ffn_out[l].astype(BF16))
        xp, new_p = _prompt_layer(xp, w, bp, tp, band_p, cfar, cos_p, sin_p)
        xs, new_s = _sample_layer(xs, w, l, bs, ts, page_table, cache_k, cache_v, cache_kidx,
                                  state_gla[l], state_ret[l], bias_s, expand, cos_s, sin_s)
        rows_p.append(new_p)
        rows_s.append(new_s)
    outs_p = [jnp.stack(r) for r in zip(*rows_p)]
    outs_s = [jnp.stack(r) for r in zip(*rows_s)]
    y_p = xp.reshape(bp, tp, d)
    y_s = xs.reshape(bs, SAMPLE_ROWS, d)[:, :ts]
    return (y_p, y_s, *outs_p, *outs_s)
```

```python
import functools
import math

import numpy as np
import jax
import jax.numpy as jnp
from jax import lax
from jax.experimental import pallas as pl
from jax.experimental.pallas import tpu as pltpu

D_MODEL = 2048
PAGE_SIZE = 128
A_HEADS = 8
A_HEAD_DIM = 128
A_WIDTH = A_HEADS * A_HEAD_DIM
IDX_HEADS = 16
IDX_DIM = 64
TOPK_MAX = 256
REL_BUCKETS = 32
REL_MAX_DIST = 128
GLA_HEADS = 4
GLA_DK = 64
GLA_DV = 128
GLA_WIDTH = GLA_HEADS * GLA_DV
GLA_RANK = 16
GLA_TAU = 16.0
RET_HEADS = 4
RET_DK = 64
RET_DV = 128
RET_WIDTH = RET_HEADS * RET_DV
ROPE_BASE = 10000.0
CHUNK = 64
MIX_WIDTH = A_WIDTH + GLA_WIDTH + RET_WIDTH
D_FF = -(-8 * D_MODEL // (3 * 256)) * 256
EPS = 1e-6
IN_SIZES = (A_WIDTH, A_WIDTH, A_WIDTH, IDX_HEADS * IDX_DIM, IDX_DIM, IDX_HEADS,
            GLA_HEADS * GLA_DK, GLA_HEADS * GLA_DK, GLA_WIDTH, GLA_RANK, GLA_WIDTH,
            RET_HEADS * RET_DK, RET_HEADS * RET_DK, RET_WIDTH, RET_WIDTH, 3 * D_MODEL)

COL_GATES = 0
COL_AQ = 3 * D_MODEL
COL_GV = COL_AQ + 4 * A_WIDTH
COL_GQ = COL_GV + 4 * GLA_WIDTH
COL_MISC = COL_GQ + 4 * 256
IN_PADDED = COL_MISC + 128
MISC_IW = IDX_DIM
MISC_GA = IDX_DIM + IDX_HEADS

LANES = 128
SAMPLE_ROWS = 16
VMEM_LIMIT = 56 * 1024 * 1024
INT_MIN = -2147483648
NEG_BIG = -1e30
LOG2_E = math.log2(math.e)
ROW_CHUNK = 128

BF16 = jnp.bfloat16
F32 = jnp.float32
NT_DIMS = (((1,), (1,)), ((), ()))
TN_DIMS = (((0,), (0,)), ((), ()))


def _params(n_axes):
    return pltpu.CompilerParams(dimension_semantics=("arbitrary",) * n_axes,
                                vmem_limit_bytes=VMEM_LIMIT)


def _dot(a, b):
    return jnp.dot(a, b, preferred_element_type=F32)


def _dot_nt(a, b):
    return lax.dot_general(a, b, NT_DIMS, preferred_element_type=F32)


def _dot_tn(a, b):
    return lax.dot_general(a, b, TN_DIMS, preferred_element_type=F32)


def _norm_matmul_kernel(x_ref, g_ref, w_ref, o_ref, hb_ref):
    @pl.when(pl.program_id(1) == 0)
    def _():
        x = x_ref[...]
        ms = jnp.mean(x * x, axis=-1, keepdims=True)
        hb_ref[...] = (x * lax.rsqrt(ms + EPS) * g_ref[...]).astype(BF16)

    o_ref[...] = _dot(hb_ref[...], w_ref[...])


def _norm_matmul(x, g, w, tm, tn):
    m, d = x.shape
    n = w.shape[1]
    return pl.pallas_call(
        _norm_matmul_kernel,
        grid=(m // tm, n // tn),
        in_specs=[pl.BlockSpec((tm, d), lambda i, j: (i, 0)),
                  pl.BlockSpec((1, d), lambda i, j: (0, 0)),
                  pl.BlockSpec((d, tn), lambda i, j: (0, j))],
        out_specs=pl.BlockSpec((tm, tn), lambda i, j: (i, j)),
        out_shape=jax.ShapeDtypeStruct((m, n), F32),
        scratch_shapes=[pltpu.VMEM((tm, d), BF16)],
        compiler_params=_params(2),
        name="in_proj",
    )(x, g, w)


def _norm_swiglu_kernel(x_ref, g_ref, wg_ref, wu_ref, o_ref, hb_ref):
    @pl.when(pl.program_id(1) == 0)
    def _():
        x = x_ref[...]
        ms = jnp.mean(x * x, axis=-1, keepdims=True)
        hb_ref[...] = (x * lax.rsqrt(ms + EPS) * g_ref[...]).astype(BF16)

    h = hb_ref[...]
    gate = _dot(h, wg_ref[...])
    up = _dot(h, wu_ref[...])
    o_ref[...] = (gate * jax.nn.sigmoid(gate) * up).astype(BF16)


def _norm_swiglu(x, g, w, tm, tf):
    m, d = x.shape
    f = w.shape[1] // 2
    nf = f // tf
    return pl.pallas_call(
        _norm_swiglu_kernel,
        grid=(m // tm, nf),
        in_specs=[pl.BlockSpec((tm, d), lambda i, j: (i, 0)),
                  pl.BlockSpec((1, d), lambda i, j: (0, 0)),
                  pl.BlockSpec((d, tf), lambda i, j: (0, j)),
                  pl.BlockSpec((d, tf), lambda i, j: (0, j + nf))],
        out_specs=pl.BlockSpec((tm, tf), lambda i, j: (i, j)),
        out_shape=jax.ShapeDtypeStruct((m, f), BF16),
        scratch_shapes=[pltpu.VMEM((tm, d), BF16)],
        compiler_params=_params(2),
        name="swiglu",
    )(x, g, w, w)


def _matmul_residual_kernel(a_ref, w_ref, r_ref, o_ref):
    o_ref[...] = r_ref[...] + _dot(a_ref[...], w_ref[...])


def _matmul_residual(a, w, r, tm, tn, name):
    m, k = a.shape
    n = w.shape[1]
    return pl.pallas_call(
        _matmul_residual_kernel,
        grid=(m // tm, n // tn),
        in_specs=[pl.BlockSpec((tm, k), lambda i, j: (i, 0)),
                  pl.BlockSpec((k, tn), lambda i, j: (0, j)),
                  pl.BlockSpec((tm, tn), lambda i, j: (i, j))],
        out_specs=pl.BlockSpec((tm, tn), lambda i, j: (i, j)),
        out_shape=jax.ShapeDtypeStruct((m, n), F32),
        compiler_params=_params(2),
        name=name,
    )(a, w, r)


def _merge_kernel(oa_ref, ob_ref, oc_ref, ga_ref, gb_ref, gc_ref, wa_ref, wb_ref, wc_ref, o_ref):
    ya = _dot(oa_ref[...], wa_ref[...])
    yb = _dot(ob_ref[...], wb_ref[...])
    yc = _dot(oc_ref[...], wc_ref[...])
    merged = (jax.nn.sigmoid(ga_ref[...]) * ya + jax.nn.sigmoid(gb_ref[...]) * yb
              + jax.nn.sigmoid(gc_ref[...]) * yc)
    o_ref[...] = merged.astype(BF16)


def _merge(oa, ob, oc, z, w_branch, tm, tn):
    m = oa.shape[0]
    n = D_MODEL
    nb = n // tn
    a_blocks = A_WIDTH // GLA_WIDTH
    return pl.pallas_call(
        _merge_kernel,
        grid=(m // tm, nb),
        in_specs=[pl.BlockSpec((tm, A_WIDTH), lambda i, j: (i, 0)),
                  pl.BlockSpec((tm, GLA_WIDTH), lambda i, j: (i, 0)),
                  pl.BlockSpec((tm, RET_WIDTH), lambda i, j: (i, 0)),
                  pl.BlockSpec((tm, tn), lambda i, j: (i, j)),
                  pl.BlockSpec((tm, tn), lambda i, j: (i, j + nb)),
                  pl.BlockSpec((tm, tn), lambda i, j: (i, j + 2 * nb)),
                  pl.BlockSpec((A_WIDTH, tn), lambda i, j: (0, j)),
                  pl.BlockSpec((GLA_WIDTH, tn), lambda i, j: (a_blocks, j)),
                  pl.BlockSpec((RET_WIDTH, tn), lambda i, j: (a_blocks + 1, j))],
        out_specs=pl.BlockSpec((tm, tn), lambda i, j: (i, j)),
        out_shape=jax.ShapeDtypeStruct((m, n), BF16),
        compiler_params=_params(2),
        name="merge",
    )(oa, ob, oc, z, z, z, w_branch, w_branch, w_branch)


def _head_rmsnorm(x, g):
    outs = []
    for h in range(A_HEADS):
        xh = x[:, h * A_HEAD_DIM:(h + 1) * A_HEAD_DIM]
        ms = jnp.mean(xh * xh, axis=-1, keepdims=True)
        outs.append(xh * lax.rsqrt(ms + EPS) * g)
    return jnp.concatenate(outs, axis=-1)


def _prep_a_kernel(aq_ref, ak_ref, av_ref, iq_ref, misc_ref, gq_ref, gk_ref,
                   qn_ref, kf_ref, kb_ref, vf_ref, vb_ref, iqm_ref, ik2_ref):
    qn_ref[...] = _head_rmsnorm(aq_ref[...], gq_ref[...]).astype(BF16)
    kn = _head_rmsnorm(ak_ref[...], gk_ref[...])
    kf_ref[...] = kn
    kb_ref[...] = kn.astype(BF16)
    v = av_ref[...]
    vf_ref[...] = v
    vb_ref[...] = v.astype(BF16)
    iq = iq_ref[...].astype(BF16)
    lane = lax.broadcasted_iota(jnp.int32, (iq.shape[0], LANES), 1)
    zero = jnp.zeros((iq.shape[0], LANES), BF16)
    for h in range(IDX_HEADS):
        pair = iq[:, (h // 2) * LANES:(h // 2 + 1) * LANES]
        keep = (lane < IDX_DIM) if h % 2 == 0 else (lane >= IDX_DIM)
        iqm_ref[:, h * LANES:(h + 1) * LANES] = jnp.where(keep, pair, zero)
    ik = misc_ref[...][:, :IDX_DIM].astype(BF16)
    ik2_ref[...] = jnp.concatenate([ik, ik], axis=-1)


def _prep_a(z, gq, gk, tm):
    m = z.shape[0]
    blk = COL_AQ // A_WIDTH
    wide = lambda c: pl.BlockSpec((tm, A_WIDTH), lambda i: (i, c))
    row = pl.BlockSpec((tm, A_WIDTH), lambda i: (i, 0))
    return pl.pallas_call(
        _prep_a_kernel,
        grid=(m // tm,),
        in_specs=[wide(blk), wide(blk + 1), wide(blk + 2), wide(blk + 3),
                  pl.BlockSpec((tm, LANES), lambda i: (i, COL_MISC // LANES)),
                  pl.BlockSpec((1, A_HEAD_DIM), lambda i: (0, 0)),
                  pl.BlockSpec((1, A_HEAD_DIM), lambda i: (0, 0))],
        out_specs=[row, row, row, row, row,
                   pl.BlockSpec((tm, IDX_HEADS * LANES), lambda i: (i, 0)),
                   pl.BlockSpec((tm, LANES), lambda i: (i, 0))],
        out_shape=[jax.ShapeDtypeStruct((m, A_WIDTH), BF16),
                   jax.ShapeDtypeStruct((m, A_WIDTH), F32),
                   jax.ShapeDtypeStruct((m, A_WIDTH), BF16),
                   jax.ShapeDtypeStruct((m, A_WIDTH), F32),
                   jax.ShapeDtypeStruct((m, A_WIDTH), BF16),
                   jax.ShapeDtypeStruct((m, IDX_HEADS * LANES), BF16),
                   jax.ShapeDtypeStruct((m, LANES), BF16)],
        compiler_params=_params(1),
        name="prep_a",
    )(z, z, z, z, z, gq, gk)


def _sortable_key(score):
    bits = pltpu.bitcast(score, jnp.int32)
    return bits ^ ((bits >> 31) & jnp.int32(0x7FFFFFFF))


def _topk_threshold(count_ge, rows, topk):
    def body(it, ans):
        cand = ans + lax.shift_left(jnp.int32(1), jnp.int32(31) - it)
        return jnp.where(count_ge(cand) >= float(topk), cand, ans)

    ans = lax.fori_loop(0, 32, body, jnp.full((rows, 1), INT_MIN, jnp.int32))
    return jnp.maximum(ans, jnp.int32(INT_MIN + 1))


def _a_prompt_kernel(cfar_ref, qn_ref, iqm_ref, mq_ref, kb_ref, vb_ref, ik2_ref, band_ref,
                     o_ref, s_scr, w_scr, acc_scr, m_scr, l_scr, thr_scr, *, topk):
    i = pl.program_id(1)
    tq = qn_ref.shape[1]
    kt = tq
    scale = A_HEAD_DIM ** -0.5
    iw_scale = (IDX_HEADS * IDX_DIM) ** -0.5

    iw = mq_ref[0][:, MISC_IW:MISC_IW + IDX_HEADS] * iw_scale
    for h in range(IDX_HEADS):
        w_scr[h] = jnp.broadcast_to(iw[:, h:h + 1], (tq, LANES))

    def score_tile(j, diagonal):
        k0 = pl.multiple_of(j * kt, kt)
        ik = ik2_ref[0, pl.ds(k0, kt), :]
        acc = jnp.zeros((tq, kt), F32)
        for h in range(IDX_HEADS):
            x = _dot_nt(iqm_ref[0, :, h * LANES:(h + 1) * LANES], ik)
            acc = acc + jnp.maximum(x, 0.0) * jnp.tile(w_scr[h], (1, kt // LANES))
        key = _sortable_key(acc)
        if diagonal:
            r = lax.broadcasted_iota(jnp.int32, (tq, kt), 0)
            c = lax.broadcasted_iota(jnp.int32, (tq, kt), 1)
            key = jnp.where(c <= r, key, jnp.int32(INT_MIN))
        s_scr[:, pl.ds(k0, kt)] = key

    def score_body(j, carry):
        score_tile(j, False)
        return carry

    lax.fori_loop(0, i, score_body, 0)
    score_tile(i, True)

    for rc in range(tq // ROW_CHUNK):
        rows = slice(rc * ROW_CHUNK, (rc + 1) * ROW_CHUNK)

        def count_ge(t, rows=rows):
            tb = jnp.broadcast_to(t, (ROW_CHUNK, LANES))

            def add_tile(j, cnt):
                k0 = pl.multiple_of(j * kt, kt)
                keys = s_scr[rows, pl.ds(k0, kt)]
                for c in range(kt // LANES):
                    cnt = cnt + jnp.where(keys[:, c * LANES:(c + 1) * LANES] >= tb,
                                          jnp.int32(1), jnp.int32(0))
                return cnt

            def add_pair(jj, cnt):
                return add_tile(2 * jj + 1, add_tile(2 * jj, cnt))

            n_pairs = (i + 1) // 2
            cnt = lax.fori_loop(0, n_pairs, add_pair, jnp.zeros((ROW_CHUNK, LANES), jnp.int32))
            cnt = lax.fori_loop(2 * n_pairs, i + 1, add_tile, cnt)
            return jnp.sum(cnt.astype(F32), axis=-1, keepdims=True)

        thr_scr[rows] = jnp.broadcast_to(_topk_threshold(count_ge, ROW_CHUNK, topk), (ROW_CHUNK, LANES))

    m_scr[...] = jnp.full(m_scr.shape, NEG_BIG, F32)
    l_scr[...] = jnp.zeros(l_scr.shape, F32)
    acc_scr[...] = jnp.zeros(acc_scr.shape, F32)
    c_qk = scale * LOG2_E

    def qk(k0, h, rows):
        hs = slice(h * A_HEAD_DIM, (h + 1) * A_HEAD_DIM)
        return _dot_nt(qn_ref[0, rows, hs], kb_ref[0, pl.ds(k0, kt), hs])

    def mask_of(k0, rows):
        return pltpu.bitcast(s_scr[rows, pl.ds(k0, kt)], F32)

    def halves_max(x):
        out = x[:, :LANES]
        for c in range(1, kt // LANES):
            out = jnp.maximum(out, x[:, c * LANES:(c + 1) * LANES])
        return out

    def halves_sum(x):
        out = x[:, :LANES]
        for c in range(1, kt // LANES):
            out = out + x[:, c * LANES:(c + 1) * LANES]
        return out

    def max_tile(j, band_off):
        k0 = pl.multiple_of(j * kt, kt)
        thr_t = jnp.tile(thr_scr[...], (1, kt // LANES))
        mask = jnp.where(s_scr[:, pl.ds(k0, kt)] >= thr_t, 0.0, NEG_BIG)
        s_scr[:, pl.ds(k0, kt)] = pltpu.bitcast(mask, jnp.int32)
        for h in range(A_HEADS):
            for rc in range(tq // ROW_CHUNK):
                rows = slice(rc * ROW_CHUNK, (rc + 1) * ROW_CHUNK)
                if band_off is None:
                    top = halves_max(qk(k0, h, rows) + mask_of(k0, rows)) * c_qk + cfar_ref[h]
                else:
                    top = halves_max(qk(k0, h, rows) * c_qk + band_ref[h, rows, band_off:band_off + kt]
                                     + mask_of(k0, rows))
                m_scr[h, rows] = jnp.maximum(m_scr[h, rows], top)

    def sum_tile(j, band_off):
        k0 = pl.multiple_of(j * kt, kt)
        for h in range(A_HEADS):
            hs = slice(h * A_HEAD_DIM, (h + 1) * A_HEAD_DIM)
            for rc in range(tq // ROW_CHUNK):
                rows = slice(rc * ROW_CHUNK, (rc + 1) * ROW_CHUNK)
                if band_off is None:
                    shift = jnp.tile(cfar_ref[h] - m_scr[h, rows], (1, kt // LANES))
                else:
                    shift = (band_ref[h, rows, band_off:band_off + kt]
                             - jnp.tile(m_scr[h, rows], (1, kt // LANES)))
                p = jnp.exp2(qk(k0, h, rows) * c_qk + shift + mask_of(k0, rows))
                l_scr[h, rows] = l_scr[h, rows] + halves_sum(p)
                acc_scr[rows, hs] = acc_scr[rows, hs] + _dot(p.astype(BF16), vb_ref[0, pl.ds(k0, kt), hs])

    def over_tiles(tile_fn):
        def far_body(j, carry):
            tile_fn(j, None)
            return carry

        lax.fori_loop(0, jnp.maximum(i - 1, 0), far_body, 0)

        @pl.when(i >= 1)
        def _():
            tile_fn(i - 1, 0)

        tile_fn(i, kt)

    over_tiles(max_tile)
    for h in range(A_HEADS):
        m_scr[h] = jnp.broadcast_to(jnp.max(m_scr[h], axis=-1, keepdims=True), (tq, LANES))
    over_tiles(sum_tile)
    for h in range(A_HEADS):
        hs = slice(h * A_HEAD_DIM, (h + 1) * A_HEAD_DIM)
        o_ref[0, :, hs] = (acc_scr[:, hs] / jnp.sum(l_scr[h], axis=-1, keepdims=True)).astype(BF16)


def _rel_bucket(dist):
    n = jnp.maximum(dist, 0)
    max_exact = REL_BUCKETS // 2
    nf = jnp.maximum(n, 1).astype(F32)
    large = max_exact + (jnp.log(nf / max_exact) / math.log(REL_MAX_DIST / max_exact)
                         * (REL_BUCKETS - max_exact)).astype(jnp.int32)
    large = jnp.minimum(large, REL_BUCKETS - 1)
    return jnp.where(n < max_exact, n, large)


def _bias_band(rel_table, rows, kt):
    d = jnp.arange(rows)[:, None] + kt - jnp.arange(2 * kt)[None, :]
    hit = _rel_bucket(d)[None, :, :, None] == jnp.arange(REL_BUCKETS)
    return jnp.sum(jnp.where(hit, rel_table.T.astype(F32)[:, None, None, :], 0.0), axis=-1)


def _a_prompt(qn, iqm, z, kb, vb, ik2, band, cfar, batch, seq, tq):
    topk = min(TOPK_MAX, seq // 4)
    r3 = lambda a: a.reshape(batch, seq, a.shape[-1])
    qblk = lambda w: pl.BlockSpec((1, tq, w), lambda b, i: (b, i, 0))
    full = lambda w: pl.BlockSpec((1, seq, w), lambda b, i: (b, 0, 0), pipeline_mode=pl.Buffered(1))
    out = pl.pallas_call(
        functools.partial(_a_prompt_kernel, topk=topk),
        grid=(batch, seq // tq),
        in_specs=[pl.BlockSpec(memory_space=pltpu.SMEM),
                  qblk(A_WIDTH), qblk(IDX_HEADS * LANES),
                  pl.BlockSpec((1, tq, LANES), lambda b, i: (b, i, COL_MISC // LANES)),
                  full(A_WIDTH), full(A_WIDTH), full(LANES),
                  pl.BlockSpec((A_HEADS, tq, 2 * tq), lambda b, i: (0, 0, 0),
                               pipeline_mode=pl.Buffered(1))],
        out_specs=qblk(A_WIDTH),
        scratch_shapes=[pltpu.VMEM((tq, seq), jnp.int32),
                        pltpu.VMEM((IDX_HEADS, tq, LANES), F32),
                        pltpu.VMEM((tq, A_WIDTH), F32),
                        pltpu.VMEM((A_HEADS, tq, LANES), F32),
                        pltpu.VMEM((A_HEADS, tq, LANES), F32),
                        pltpu.VMEM((tq, LANES), jnp.int32)],
        out_shape=jax.ShapeDtypeStruct((batch, seq, A_WIDTH), BF16),
        compiler_params=_params(2),
        name="a_prompt",
    )(cfar, r3(qn), r3(iqm), r3(z), r3(kb), r3(vb), r3(ik2), band)
    return out.reshape(batch * seq, A_WIDTH)


def _a_sample_score_kernel(pt_ref, iq_ref, w_ref, mnew_ref, *rest, pages_per_step, n_pages, topk):
    page_refs = rest[:pages_per_step]
    keys_ref, thr_ref = rest[pages_per_step:]
    g = pl.program_id(1)
    rows = SAMPLE_ROWS
    iq = iq_ref[0]
    w = w_ref[0]

    def scores(ik):
        x = jnp.maximum(_dot_nt(iq, ik), 0.0) * w
        return jnp.sum(x.reshape(IDX_HEADS, rows, PAGE_SIZE), axis=0)

    for p in range(pages_per_step):
        off = pl.multiple_of((g * pages_per_step + p) * PAGE_SIZE, PAGE_SIZE)
        keys_ref[0, :, pl.ds(off, PAGE_SIZE)] = _sortable_key(scores(page_refs[p][...].astype(BF16)))

    @pl.when(g == pl.num_programs(1) - 1)
    def _():
        past = n_pages * PAGE_SIZE
        ik_new = mnew_ref[0][:, :IDX_DIM].astype(BF16)
        ik_new = jnp.concatenate([ik_new, jnp.zeros((PAGE_SIZE - rows, IDX_DIM), BF16)], axis=0)
        r = lax.broadcasted_iota(jnp.int32, (rows, PAGE_SIZE), 0)
        c = lax.broadcasted_iota(jnp.int32, (rows, PAGE_SIZE), 1)
        keys_ref[0, :, past:past + PAGE_SIZE] = jnp.where(c <= r, _sortable_key(scores(ik_new)),
                                                          jnp.int32(INT_MIN))

        def count_ge(t):
            def body(j, cnt):
                off = pl.multiple_of(j * PAGE_SIZE, PAGE_SIZE)
                return cnt + jnp.where(keys_ref[0, :, pl.ds(off, PAGE_SIZE)] >= t,
                                       jnp.int32(1), jnp.int32(0))
            cnt = lax.fori_loop(0, n_pages + 1, body, jnp.zeros((rows, PAGE_SIZE), jnp.int32))
            return jnp.sum(cnt.astype(F32), axis=-1, keepdims=True)

        thr = _topk_threshold(count_ge, rows, topk)
        thr_ref[0] = jnp.broadcast_to(thr, (rows, LANES))


def _a_sample_scores(page_table, iq_hq, w_hq, z3, cache_kidx, layer, n_tok, pages_per_step):
    batch, n_pages = page_table.shape
    rows = SAMPLE_ROWS
    nk = (n_pages + 1) * PAGE_SIZE
    topk = min(TOPK_MAX, (n_pages * PAGE_SIZE + n_tok) // 4)

    def page_spec(p):
        return pl.BlockSpec((None, None, PAGE_SIZE, IDX_DIM),
                            lambda b, g, pt: (layer, pt[b, g * pages_per_step + p], 0, 0))

    return pl.pallas_call(
        functools.partial(_a_sample_score_kernel, pages_per_step=pages_per_step, n_pages=n_pages,
                          topk=topk),
        grid_spec=pltpu.PrefetchScalarGridSpec(
            num_scalar_prefetch=1,
            grid=(batch, n_pages // pages_per_step),
            in_specs=[pl.BlockSpec((1, IDX_HEADS * rows, IDX_DIM), lambda b, g, pt: (b, 0, 0)),
                      pl.BlockSpec((1, IDX_HEADS * rows, 1), lambda b, g, pt: (b, 0, 0)),
                      pl.BlockSpec((1, rows, LANES), lambda b, g, pt: (b, 0, COL_MISC // LANES))]
                     + [page_spec(p) for p in range(pages_per_step)],
            out_specs=[pl.BlockSpec((1, rows, nk), lambda b, g, pt: (b, 0, 0)),
                       pl.BlockSpec((1, rows, LANES), lambda b, g, pt: (b, 0, 0))]),
        out_shape=[jax.ShapeDtypeStruct((batch, rows, nk), jnp.int32),
                   jax.ShapeDtypeStruct((batch, rows, LANES), jnp.int32)],
        compiler_params=_params(2),
        name="a_sample_scores",
    )(page_table, iq_hq, w_hq, z3, *([cache_kidx] * pages_per_step))


def _a_sample_attn_kernel(pt_ref, qall_ref, keys_ref, thr_ref, knew_ref, vnew_ref, bias_far_ref, bias_last_ref,
                          bias_new_ref, expand_ref, *rest, pages_per_step, n_pages):
    k_refs = rest[:pages_per_step]
    v_refs = rest[pages_per_step:2 * pages_per_step]
    o_ref, acc_scr, m_scr, l_scr, lg_scr = rest[2 * pages_per_step:]
    g = pl.program_id(1)
    last_step = pl.num_programs(1) - 1
    rows = SAMPLE_ROWS
    hq = A_HEADS * rows
    cols = PAGE_SIZE * A_HEADS
    scale = A_HEAD_DIM ** -0.5
    thr = thr_ref[0][:, :1]
    qall = qall_ref[0]

    @pl.when(g == 0)
    def _():
        m_scr[...] = jnp.full(m_scr.shape, NEG_BIG, F32)
        l_scr[...] = jnp.zeros(l_scr.shape, F32)
        acc_scr[...] = jnp.zeros(acc_scr.shape, F32)

    def masked_logits(key_tile, expand, kf, bias):
        hit = jnp.where(key_tile >= thr, 1.0, 0.0).astype(BF16)
        drop = (_dot(hit, expand) - 1.0) * (-NEG_BIG)
        return _dot_nt(qall, kf) * scale + bias + jnp.concatenate([drop] * A_HEADS, axis=0)

    def lane_fold(x, op):
        out = x[:, :LANES]
        for c in range(1, x.shape[1] // LANES):
            out = op(out, x[:, c * LANES:(c + 1) * LANES])
        return out

    def update(n_tiles, width, v_of):
        top = lane_fold(lg_scr[:, :n_tiles * width], jnp.maximum)
        m_old = m_scr[...]
        m_new = jnp.maximum(m_old, jnp.max(top, axis=-1, keepdims=True))
        alpha = jnp.exp(m_old - m_new)
        part = jnp.zeros((hq, LANES), F32)
        acc = alpha * acc_scr[...]
        for t in range(n_tiles):
            p = jnp.exp(lg_scr[:, t * width:(t + 1) * width] - m_new)
            part = part + lane_fold(p, jnp.add)
            acc = acc + _dot(p.astype(BF16), v_of(t))
        l_scr[...] = alpha * l_scr[...] + jnp.sum(part, axis=-1, keepdims=True)
        acc_scr[...] = acc
        m_scr[...] = m_new

    for p in range(pages_per_step):
        page = g * pages_per_step + p
        off = pl.multiple_of(page * PAGE_SIZE, PAGE_SIZE)
        bias = bias_far_ref[...]
        if p == pages_per_step - 1:
            bias = jnp.where(g == last_step, bias_last_ref[...], bias)
        lg_scr[:, p * cols:(p + 1) * cols] = masked_logits(
            keys_ref[0, :, pl.ds(off, PAGE_SIZE)], expand_ref[...],
            k_refs[p][...].reshape(cols, A_HEAD_DIM).astype(BF16), bias)
    update(pages_per_step, cols, lambda t: v_refs[t][...].reshape(cols, A_HEAD_DIM).astype(BF16))

    @pl.when(g == last_step)
    def _():
        past = n_pages * PAGE_SIZE
        lg_scr[:, :hq] = masked_logits(keys_ref[0, :, past:past + PAGE_SIZE], expand_ref[:, :hq],
                                       knew_ref[0], bias_new_ref[...])
        update(1, hq, lambda t: vnew_ref[0])
        o_ref[0] = (acc_scr[...] / l_scr[...]).astype(BF16)


def _a_sample_attn(page_table, qall, keys, thr, knew, vnew, bias_far, bias_last, bias_new, expand,
                   cache_k, cache_v, layer, pages_per_step):
    batch, n_pages = page_table.shape
    rows = SAMPLE_ROWS
    hq = A_HEADS * rows
    nk = keys.shape[-1]

    def page_spec(p):
        return pl.BlockSpec((None, None, PAGE_SIZE, A_HEADS, A_HEAD_DIM),
                            lambda b, g, pt: (layer, pt[b, g * pages_per_step + p], 0, 0, 0))

    per_b = lambda r, w: pl.BlockSpec((1, r, w), lambda b, g, pt: (b, 0, 0))
    const = lambda r, w: pl.BlockSpec((r, w), lambda b, g, pt: (0, 0))
    return pl.pallas_call(
        functools.partial(_a_sample_attn_kernel, pages_per_step=pages_per_step, n_pages=n_pages),
        grid_spec=pltpu.PrefetchScalarGridSpec(
            num_scalar_prefetch=1,
            grid=(batch, n_pages // pages_per_step),
            in_specs=[per_b(hq, A_HEAD_DIM), per_b(rows, nk), per_b(rows, LANES),
                      per_b(hq, A_HEAD_DIM), per_b(hq, A_HEAD_DIM),
                      const(hq, PAGE_SIZE * A_HEADS), const(hq, PAGE_SIZE * A_HEADS), const(hq, hq),
                      const(PAGE_SIZE, PAGE_SIZE * A_HEADS)]
                     + [page_spec(p) for p in range(pages_per_step)] * 2,
            out_specs=per_b(hq, A_HEAD_DIM),
            scratch_shapes=[pltpu.VMEM((hq, A_HEAD_DIM), F32),
                            pltpu.VMEM((hq, 1), F32),
                            pltpu.VMEM((hq, 1), F32),
                            pltpu.VMEM((hq, pages_per_step * PAGE_SIZE * A_HEADS), F32)]),
        out_shape=jax.ShapeDtypeStruct((batch, hq, A_HEAD_DIM), BF16),
        compiler_params=_params(2),
        name="a_sample_attn",
    )(page_table, qall, keys, thr, knew, vnew, bias_far, bias_last, bias_new, expand,
      *([cache_k] * pages_per_step), *([cache_v] * pages_per_step))


def _pad_rows(x, rows):
    if x.shape[0] == rows:
        return x
    return jnp.concatenate([x, jnp.zeros((rows - x.shape[0], x.shape[1]), x.dtype)], axis=0)


def _split3(x):
    hi = x.astype(BF16)
    r1 = x - hi.astype(F32)
    mid = r1.astype(BF16)
    lo = (r1 - mid.astype(F32)).astype(BF16)
    return hi, mid, lo


def _gla_kernel(gq_ref, gk_ref, gv_ref, gg_ref, misc_ref, wa_ref, ba_ref, gn_ref, s0_ref,
                o_ref, sfin_ref, st_scr, oacc_scr, *, chunk, rows, n_valid):
    tt = gq_ref.shape[1]

    @pl.when(pl.program_id(1) == 0)
    def _():
        st_scr[...] = s0_ref[0]

    ga = _pad_rows(misc_ref[0][:, MISC_GA:MISC_GA + GLA_RANK], rows)
    x = _dot(ga.astype(BF16), wa_ref[...]) + ba_ref[...]
    la = (jnp.minimum(x, 0.0) - jnp.log1p(jnp.exp(-jnp.abs(x)))) * (1.0 / GLA_TAU)
    r = lax.broadcasted_iota(jnp.int32, (rows, rows), 0)
    c = lax.broadcasted_iota(jnp.int32, (rows, rows), 1)
    if n_valid < rows:
        rr = lax.broadcasted_iota(jnp.int32, la.shape, 0)
        la = jnp.where(rr < n_valid, la, 0.0)
    tri = jnp.where((c <= r) & (r // chunk == c // chunk), 1.0, 0.0).astype(BF16)
    hi, mid, lo = _split3(la)
    b = _dot(tri, hi) + _dot(tri, mid) + _dot(tri, lo)

    gq = _pad_rows(gq_ref[0], rows) * (GLA_DK ** -0.5)
    gk = _pad_rows(gk_ref[0], rows)
    if n_valid < rows:
        gk = jnp.where(rr < n_valid, gk, 0.0)
    gv = _pad_rows(gv_ref[0], rows)
    causal = (lax.broadcasted_iota(jnp.int32, (chunk, chunk), 1)
              <= lax.broadcasted_iota(jnp.int32, (chunk, chunk), 0))

    for ci in range(rows // chunk):
        rs = slice(ci * chunk, (ci + 1) * chunk)
        for h in range(GLA_HEADS):
            ks = slice(h * GLA_DK, (h + 1) * GLA_DK)
            vs = slice(h * GLA_DV, (h + 1) * GLA_DV)
            bc = b[rs, ks]
            bend = bc[chunk - 1:chunk, :]
            qt = (gq[rs, ks] * jnp.exp(bc)).astype(BF16)
            kt = (gk[rs, ks] * jnp.exp(-bc)).astype(BF16)
            kd = (gk[rs, ks] * jnp.exp(bend - bc)).astype(BF16)
            vh = gv[rs, vs].astype(BF16)
            st = st_scr[h]
            sc = jnp.where(causal, _dot_nt(qt, kt), 0.0)
            oacc_scr[rs, vs] = _dot(sc.astype(BF16), vh) + _dot_nt(qt, st.astype(BF16))
            st_scr[h] = st * jnp.exp(bend) + _dot_tn(vh, kd)

    gg = gg_ref[0]
    for h in range(GLA_HEADS):
        vs = slice(h * GLA_DV, (h + 1) * GLA_DV)
        oh = oacc_scr[0:tt, vs]
        ms = jnp.mean(oh * oh, axis=-1, keepdims=True)
        g = gg[:, vs]
        o_ref[0, :, vs] = (oh * lax.rsqrt(ms + EPS) * gn_ref[...] * (g * jax.nn.sigmoid(g))).astype(BF16)

    @pl.when(pl.program_id(1) == pl.num_programs(1) - 1)
    def _():
        sfin_ref[0] = st_scr[...]


def _gla(z3, wa, ba, gn, s0t, tt, chunk, rows, n_valid):
    batch, seq, _ = z3.shape
    q_blk = COL_GQ // 256
    v_blk = COL_GV // GLA_WIDTH
    tok = lambda w, cblk: pl.BlockSpec((1, tt, w), lambda b, t: (b, t, cblk))
    const = lambda shape: pl.BlockSpec(shape, lambda b, t: (0,) * len(shape))
    st_spec = pl.BlockSpec((1, GLA_HEADS, GLA_DV, GLA_DK), lambda b, t: (b, 0, 0, 0))
    return pl.pallas_call(
        functools.partial(_gla_kernel, chunk=chunk, rows=rows, n_valid=n_valid),
        grid=(batch, seq // tt),
        in_specs=[tok(256, q_blk), tok(256, q_blk + 1), tok(GLA_WIDTH, v_blk), tok(GLA_WIDTH, v_blk + 1),
                  tok(LANES, COL_MISC // LANES),
                  const((GLA_RANK, GLA_HEADS * GLA_DK)), const((1, GLA_HEADS * GLA_DK)),
                  const((1, GLA_DV)), st_spec],
        out_specs=[pl.BlockSpec((1, tt, GLA_WIDTH), lambda b, t: (b, t, 0)), st_spec],
        out_shape=[jax.ShapeDtypeStruct((batch, seq, GLA_WIDTH), BF16),
                   jax.ShapeDtypeStruct((batch, GLA_HEADS, GLA_DV, GLA_DK), F32)],
        scratch_shapes=[pltpu.VMEM((GLA_HEADS, GLA_DV, GLA_DK), F32),
                        pltpu.VMEM((rows, GLA_WIDTH), F32)],
        compiler_params=_params(2),
        name="gla",
    )(z3, z3, z3, z3, z3, wa, ba, gn, s0t)


def _ret_kernel(rq_ref, rk_ref, rv_ref, rg_ref, cos_ref, sin_ref, gn_ref, s0_ref,
                o_ref, sfin_ref, st_scr, oacc_scr, *, rows, n_valid, log_gamma):
    tt = rq_ref.shape[1]
    width = RET_HEADS * RET_DK

    @pl.when(pl.program_id(1) == 0)
    def _():
        st_scr[...] = s0_ref[0]

    lane = lax.broadcasted_iota(jnp.int32, (rows, width), 1)
    first_half = (lane % RET_DK) < (RET_DK // 2)
    cos = _pad_rows(cos_ref[...], rows)
    sin = _pad_rows(sin_ref[...], rows)

    def rope(x):
        rot = jnp.where(first_half, pltpu.roll(x, width - RET_DK // 2, axis=1),
                        pltpu.roll(x, RET_DK // 2, axis=1))
        return x * cos + rot * sin

    q = rope(_pad_rows(rq_ref[0], rows))
    k = rope(_pad_rows(rk_ref[0], rows)) * (RET_DK ** -0.5)
    v = _pad_rows(rv_ref[0], rows)
    rr = lax.broadcasted_iota(jnp.int32, (rows, RET_DK), 0)
    if n_valid < rows:
        k = jnp.where(lax.broadcasted_iota(jnp.int32, k.shape, 0) < n_valid, k, 0.0)
    t_i = lax.broadcasted_iota(jnp.int32, (rows, rows), 0)
    s_i = lax.broadcasted_iota(jnp.int32, (rows, rows), 1)
    dist = (t_i - s_i).astype(F32)
    pos1 = (rr + 1).astype(F32)
    rem = (n_valid - 1 - rr).astype(F32)

    for h in range(RET_HEADS):
        ks = slice(h * RET_DK, (h + 1) * RET_DK)
        vs = slice(h * RET_DV, (h + 1) * RET_DV)
        lg = log_gamma[h]
        decay = jnp.where(t_i >= s_i, jnp.exp(dist * lg), 0.0)
        qh = q[:, ks]
        kh = k[:, ks]
        vh = v[:, vs].astype(BF16)
        st = st_scr[h]
        sc = _dot_nt(qh.astype(BF16), kh.astype(BF16)) * decay
        q_in = (qh * jnp.exp(pos1 * lg)).astype(BF16)
        oacc_scr[:, vs] = _dot(sc.astype(BF16), vh) + _dot_nt(q_in, st.astype(BF16))
        kd = (kh * jnp.exp(rem * lg)).astype(BF16)
        st_scr[h] = st * math.exp(n_valid * lg) + _dot_tn(vh, kd)

    rg = rg_ref[0]
    for h in range(RET_HEADS):
        vs = slice(h * RET_DV, (h + 1) * RET_DV)
        oh = oacc_scr[0:tt, vs]
        oc = oh - jnp.mean(oh, axis=-1, keepdims=True)
        var = jnp.mean(oc * oc, axis=-1, keepdims=True)
        g = rg[:, vs]
        o_ref[0, :, vs] = (oc * lax.rsqrt(var + EPS) * gn_ref[...] * (g * jax.nn.sigmoid(g))).astype(BF16)

    @pl.when(pl.program_id(1) == pl.num_programs(1) - 1)
    def _():
        sfin_ref[0] = st_scr[...]


def _ret(z3, cos, sin, gn, s0t, tt, rows, n_valid):
    batch, seq, _ = z3.shape
    q_blk = COL_GQ // 256 + 2
    v_blk = COL_GV // RET_WIDTH + 2
    width = RET_HEADS * RET_DK
    log_gamma = tuple(float(np.log1p(-np.exp2(np.float32(-5.0 - h)), dtype=np.float32))
                      for h in range(RET_HEADS))
    tok = lambda w, cblk: pl.BlockSpec((1, tt, w), lambda b, t: (b, t, cblk))
    st_spec = pl.BlockSpec((1, RET_HEADS, RET_DV, RET_DK), lambda b, t: (b, 0, 0, 0))
    return pl.pallas_call(
        functools.partial(_ret_kernel, rows=rows, n_valid=n_valid, log_gamma=log_gamma),
        grid=(batch, seq // tt),
        in_specs=[tok(256, q_blk), tok(256, q_blk + 1), tok(RET_WIDTH, v_blk), tok(RET_WIDTH, v_blk + 1),
                  pl.BlockSpec((tt, width), lambda b, t: (t, 0)),
                  pl.BlockSpec((tt, width), lambda b, t: (t, 0)),
                  pl.BlockSpec((1, RET_DV), lambda b, t: (0, 0)), st_spec],
        out_specs=[pl.BlockSpec((1, tt, RET_WIDTH), lambda b, t: (b, t, 0)), st_spec],
        out_shape=[jax.ShapeDtypeStruct((batch, seq, RET_WIDTH), BF16),
                   jax.ShapeDtypeStruct((batch, RET_HEADS, RET_DV, RET_DK), F32)],
        scratch_shapes=[pltpu.VMEM((RET_HEADS, RET_DV, RET_DK), F32),
                        pltpu.VMEM((rows, RET_WIDTH), F32)],
        compiler_params=_params(2),
        name="ret",
    )(z3, z3, z3, z3, cos, sin, gn, s0t)


def _rope_tables(pos):
    half = RET_DK // 2
    freqs = ROPE_BASE ** (-jnp.arange(half, dtype=F32) / half)
    ang = pos.astype(F32)[:, None] * freqs[None, :]
    cos = jnp.cos(ang)
    sin = jnp.sin(ang)
    cos_t = jnp.tile(jnp.concatenate([cos, cos], axis=-1), (1, RET_HEADS))
    sin_t = jnp.tile(jnp.concatenate([-sin, sin], axis=-1), (1, RET_HEADS))
    return cos_t, sin_t


def _permute_w_in(w_in):
    parts = []
    acc = 0
    for s in IN_SIZES:
        parts.append(w_in[..., acc:acc + s])
        acc += s
    aq, ak, av, iq, ik, iw, gq, gk, gv, ga, gg, rq, rk, rv, rg, gates = parts
    pad = jnp.zeros(w_in.shape[:-1] + (LANES - IDX_DIM - IDX_HEADS - GLA_RANK,), w_in.dtype)
    return jnp.concatenate([gates, aq, ak, av, iq, gv, gg, rv, rg, gq, gk, rq, rk, ik, iw, ga, pad],
                           axis=-1).astype(BF16)


def _dense_tail(x, z, oa, ob, oc, w, tm):
    merged = _merge(oa, ob, oc, z, w["w_branch"], tm, 512)
    x = _matmul_residual(merged, w["w_out"], x, tm, 512, "out_proj")
    act = _norm_swiglu(x, w["norm_ffn"], w["w_ffn_in"], tm, 512)
    return _matmul_residual(act, w["w_ffn_out"], x, tm, 512, "ffn_out")


def _prompt_layer(x, w, batch, seq, band, cfar, cos, sin):
    tm = 512
    z = _norm_matmul(x, w["norm_mix"], w["w_in"], tm, 1920)
    qn, kf, kb, vf, vb, iqm, ik2 = _prep_a(z, w["a_q_norm"], w["a_k_norm"], tm)
    oa = _a_prompt(qn, iqm, z, kb, vb, ik2, band * LOG2_E, cfar * LOG2_E, batch, seq, 256)
    z3 = z.reshape(batch, seq, IN_PADDED)
    zero_state = jnp.zeros((batch, GLA_HEADS, GLA_DV, GLA_DK), F32)
    ob, s_gla = _gla(z3, w["gla_wa"], w["gla_ba"], w["gla_norm"], zero_state, 256, CHUNK, 256, 256)
    oc, s_ret = _ret(z3, cos, sin, w["ret_norm"], zero_state, 256, 256, 256)
    x = _dense_tail(x, z, oa, ob.reshape(-1, GLA_WIDTH), oc.reshape(-1, RET_WIDTH), w, tm)
    kidx = z3[:, :, COL_MISC:COL_MISC + IDX_DIM]
    new = (kf.reshape(batch, seq, A_HEADS, A_HEAD_DIM), vf.reshape(batch, seq, A_HEADS, A_HEAD_DIM),
           kidx, s_gla.transpose(0, 1, 3, 2), s_ret.transpose(0, 1, 3, 2))
    return x, new


def _sample_layer(x, w, layer, batch, n_tok, page_table, cache_k, cache_v, cache_kidx,
                  s_gla0, s_ret0, bias_s, expand, cos, sin):
    rows = SAMPLE_ROWS
    hq = A_HEADS * rows
    tm = batch * rows
    z = _norm_matmul(x, w["norm_mix"], w["w_in"], tm, 1920)
    qn, kf, kb, vf, vb, _, _ = _prep_a(z, w["a_q_norm"], w["a_k_norm"], tm)
    z3 = z.reshape(batch, rows, IN_PADDED)
    iq = z3[:, :, COL_AQ + 3 * A_WIDTH:COL_AQ + 4 * A_WIDTH].reshape(batch, rows, IDX_HEADS, IDX_DIM)
    iq_hq = iq.transpose(0, 2, 1, 3).reshape(batch, IDX_HEADS * rows, IDX_DIM).astype(BF16)
    iw = z3[:, :, COL_MISC + MISC_IW:COL_MISC + MISC_IW + IDX_HEADS] * ((IDX_HEADS * IDX_DIM) ** -0.5)
    w_hq = iw.transpose(0, 2, 1).reshape(batch, IDX_HEADS * rows, 1)
    keys, thr = _a_sample_scores(page_table, iq_hq, w_hq, z3, cache_kidx, layer, n_tok, 32)
    qall = qn.reshape(batch, rows, A_HEADS, A_HEAD_DIM).transpose(0, 2, 1, 3).reshape(batch, hq, A_HEAD_DIM)
    oa = _a_sample_attn(page_table, qall, keys, thr, kb.reshape(batch, hq, A_HEAD_DIM),
                        vb.reshape(batch, hq, A_HEAD_DIM), *bias_s, expand, cache_k, cache_v, layer, 8)
    oa = oa.reshape(batch, A_HEADS, rows, A_HEAD_DIM).transpose(0, 2, 1, 3).reshape(tm, A_WIDTH)
    ob, s_gla = _gla(z3, w["gla_wa"], w["gla_ba"], w["gla_norm"], s_gla0.transpose(0, 1, 3, 2),
                     rows, LANES, LANES, n_tok)
    oc, s_ret = _ret(z3, cos, sin, w["ret_norm"], s_ret0.transpose(0, 1, 3, 2), rows, LANES, n_tok)
    x = _dense_tail(x, z, oa, ob.reshape(tm, GLA_WIDTH), oc.reshape(tm, RET_WIDTH), w, tm)
    kidx = z3[:, :n_tok, COL_MISC:COL_MISC + IDX_DIM]
    new = (kf.reshape(batch, rows, A_HEADS, A_HEAD_DIM)[:, :n_tok],
           vf.reshape(batch, rows, A_HEADS, A_HEAD_DIM)[:, :n_tok],
           kidx, s_gla.transpose(0, 1, 3, 2), s_ret.transpose(0, 1, 3, 2))
    return x, new


def _sample_bias(rel_table):
    rows = SAMPLE_ROWS
    hq = A_HEADS * rows
    band = _bias_band(rel_table, rows, PAGE_SIZE)
    cfar_rows = jnp.repeat(rel_table[REL_BUCKETS - 1].astype(F32), rows)[:, None]
    band_last = jnp.repeat(band[:, :, :PAGE_SIZE].reshape(hq, PAGE_SIZE), A_HEADS, axis=1)
    band_new = jnp.repeat(band[:, :, PAGE_SIZE:PAGE_SIZE + rows].reshape(hq, rows), A_HEADS, axis=1)
    cols = PAGE_SIZE * A_HEADS
    other_head = (jnp.arange(hq)[:, None] // rows) != (jnp.arange(cols)[None, :] % A_HEADS)
    head_mask = jnp.where(other_head, NEG_BIG, 0.0).astype(F32)
    return cfar_rows + head_mask, band_last + head_mask, band_new + head_mask[:, :hq]


def kernel(x_prompt, x_sample, cache_k, cache_v, cache_kidx, state_gla, state_ret, page_table, rel_table, w_in, a_q_norm, a_k_norm, gla_wa, gla_ba, gla_norm, ret_norm, w_branch, w_out, norm_mix, norm_ffn, w_ffn_in, w_ffn_out):
    depth = w_in.shape[0]
    bp, tp, d = x_prompt.shape
    bs, ts, _ = x_sample.shape
    past = page_table.shape[1] * PAGE_SIZE
    tq = 256
    row = lambda a: a.reshape(1, -1).astype(F32)

    cfar = rel_table[REL_BUCKETS - 1].astype(F32)
    band_p = _bias_band(rel_table, tq, tq)
    bias_s = _sample_bias(rel_table)
    expand = jnp.asarray(np.kron(np.eye(PAGE_SIZE), np.ones((1, A_HEADS))), dtype=BF16)
    cos_p, sin_p = _rope_tables(jnp.arange(tp))
    cos_s, sin_s = _rope_tables(past + jnp.arange(SAMPLE_ROWS))

    xp = x_prompt.reshape(bp * tp, d)
    xs = jnp.pad(x_sample, ((0, 0), (0, SAMPLE_ROWS - ts), (0, 0))).reshape(bs * SAMPLE_ROWS, d)
    rows_p, rows_s = [], []
    for l in range(depth):
        w = dict(w_in=_permute_w_in(w_in[l]), a_q_norm=row(a_q_norm[l]), a_k_norm=row(a_k_norm[l]),
                 gla_wa=gla_wa[l].astype(BF16), gla_ba=row(gla_ba[l]), gla_norm=row(gla_norm[l]),
                 ret_norm=row(ret_norm[l]), w_branch=w_branch[l].astype(BF16), w_out=w_out[l].astype(BF16),
                 norm_mix=row(norm_mix[l]), norm_ffn=row(norm_ffn[l]), w_ffn_in=w_ffn_in[l].astype(BF16),
                 w_ffn_out=w_ffn_out[l].astype(BF16))
        xp, new_p = _prompt_layer(xp, w, bp, tp, band_p, cfar, cos_p, sin_p)
        xs, new_s = _sample_layer(xs, w, l, bs, ts, page_table, cache_k, cache_v, cache_kidx,
                                  state_gla[l], state_ret[l], bias_s, expand, cos_s, sin_s)
        rows_p.append(new_p)
        rows_s.append(new_s)
    outs_p = [jnp.stack(r) for r in zip(*rows_p)]
    outs_s = [jnp.stack(r) for r in zip(*rows_s)]
    y_p = xp.reshape(bp, tp, d)
    y_s = xs.reshape(bs, SAMPLE_ROWS, d)[:, :ts]
    return (y_p, y_s, *outs_p, *outs_s)
```

```python
import functools
import math

import numpy as np
import jax
import jax.numpy as jnp
from jax import lax
from jax.experimental import pallas as pl
from jax.experimental.pallas import tpu as pltpu

D_MODEL = 2048
PAGE_SIZE = 128
A_HEADS = 8
A_HEAD_DIM = 128
A_WIDTH = A_HEADS * A_HEAD_DIM
IDX_HEADS = 16
IDX_DIM = 64
TOPK_MAX = 256
REL_BUCKETS = 32
REL_MAX_DIST = 128
GLA_HEADS = 4
GLA_DK = 64
GLA_DV = 128
GLA_WIDTH = GLA_HEADS * GLA_DV
GLA_RANK = 16
GLA_TAU = 16.0
RET_HEADS = 4
RET_DK = 64
RET_DV = 128
RET_WIDTH = RET_HEADS * RET_DV
ROPE_BASE = 10000.0
CHUNK = 64
MIX_WIDTH = A_WIDTH + GLA_WIDTH + RET_WIDTH
D_FF = -(-8 * D_MODEL // (3 * 256)) * 256
EPS = 1e-6
IN_SIZES = (A_WIDTH, A_WIDTH, A_WIDTH, IDX_HEADS * IDX_DIM, IDX_DIM, IDX_HEADS,
            GLA_HEADS * GLA_DK, GLA_HEADS * GLA_DK, GLA_WIDTH, GLA_RANK, GLA_WIDTH,
            RET_HEADS * RET_DK, RET_HEADS * RET_DK, RET_WIDTH, RET_WIDTH, 3 * D_MODEL)

COL_GATES = 0
COL_AQ = 3 * D_MODEL
COL_GV = COL_AQ + 4 * A_WIDTH
COL_GQ = COL_GV + 4 * GLA_WIDTH
COL_MISC = COL_GQ + 4 * 256
IN_PADDED = COL_MISC + 128
MISC_IW = IDX_DIM
MISC_GA = IDX_DIM + IDX_HEADS

LANES = 128
SAMPLE_ROWS = 16
VMEM_LIMIT = 56 * 1024 * 1024
INT_MIN = -2147483648
NEG_BIG = -1e30
LOG2_E = math.log2(math.e)
ROW_CHUNK = 128

BF16 = jnp.bfloat16
F32 = jnp.float32
NT_DIMS = (((1,), (1,)), ((), ()))
TN_DIMS = (((0,), (0,)), ((), ()))


def _params(n_axes):
    return pltpu.CompilerParams(dimension_semantics=("arbitrary",) * n_axes,
                                vmem_limit_bytes=VMEM_LIMIT)


def _dot(a, b):
    return jnp.dot(a, b, preferred_element_type=F32)


def _dot_nt(a, b):
    return lax.dot_general(a, b, NT_DIMS, preferred_element_type=F32)


def _dot_tn(a, b):
    return lax.dot_general(a, b, TN_DIMS, preferred_element_type=F32)


def _norm_matmul_kernel(x_ref, g_ref, w_ref, o_ref, hb_ref):
    @pl.when(pl.program_id(1) == 0)
    def _():
        x = x_ref[...]
        ms = jnp.mean(x * x, axis=-1, keepdims=True)
        hb_ref[...] = (x * lax.rsqrt(ms + EPS) * g_ref[...]).astype(BF16)

    o_ref[...] = _dot(hb_ref[...], w_ref[...])


def _norm_matmul(x, g, w, tm, tn):
    m, d = x.shape
    n = w.shape[1]
    return pl.pallas_call(
        _norm_matmul_kernel,
        grid=(m // tm, n // tn),
        in_specs=[pl.BlockSpec((tm, d), lambda i, j: (i, 0)),
                  pl.BlockSpec((1, d), lambda i, j: (0, 0)),
                  pl.BlockSpec((d, tn), lambda i, j: (0, j))],
        out_specs=pl.BlockSpec((tm, tn), lambda i, j: (i, j)),
        out_shape=jax.ShapeDtypeStruct((m, n), F32),
        scratch_shapes=[pltpu.VMEM((tm, d), BF16)],
        compiler_params=_params(2),
        name="in_proj",
    )(x, g, w)


def _norm_swiglu_kernel(x_ref, g_ref, wg_ref, wu_ref, o_ref, hb_ref):
    @pl.when(pl.program_id(1) == 0)
    def _():
        x = x_ref[...]
        ms = jnp.mean(x * x, axis=-1, keepdims=True)
        hb_ref[...] = (x * lax.rsqrt(ms + EPS) * g_ref[...]).astype(BF16)

    h = hb_ref[...]
    gate = _dot(h, wg_ref[...])
    up = _dot(h, wu_ref[...])
    o_ref[...] = (gate * jax.nn.sigmoid(gate) * up).astype(BF16)


def _norm_swiglu(x, g, w, tm, tf):
    m, d = x.shape
    f = w.shape[1] // 2
    nf = f // tf
    return pl.pallas_call(
        _norm_swiglu_kernel,
        grid=(m // tm, nf),
        in_specs=[pl.BlockSpec((tm, d), lambda i, j: (i, 0)),
                  pl.BlockSpec((1, d), lambda i, j: (0, 0)),
                  pl.BlockSpec((d, tf), lambda i, j: (0, j)),
                  pl.BlockSpec((d, tf), lambda i, j: (0, j + nf))],
        out_specs=pl.BlockSpec((tm, tf), lambda i, j: (i, j)),
        out_shape=jax.ShapeDtypeStruct((m, f), BF16),
        scratch_shapes=[pltpu.VMEM((tm, d), BF16)],
        compiler_params=_params(2),
        name="swiglu",
    )(x, g, w, w)


def _matmul_residual_kernel(a_ref, w_ref, r_ref, o_ref):
    o_ref[...] = r_ref[...] + _dot(a_ref[...], w_ref[...])


def _matmul_residual(a, w, r, tm, tn, name):
    m, k = a.shape
    n = w.shape[1]
    return pl.pallas_call(
        _matmul_residual_kernel,
        grid=(m // tm, n // tn),
        in_specs=[pl.BlockSpec((tm, k), lambda i, j: (i, 0)),
                  pl.BlockSpec((k, tn), lambda i, j: (0, j)),
                  pl.BlockSpec((tm, tn), lambda i, j: (i, j))],
        out_specs=pl.BlockSpec((tm, tn), lambda i, j: (i, j)),
        out_shape=jax.ShapeDtypeStruct((m, n), F32),
        compiler_params=_params(2),
        name=name,
    )(a, w, r)


def _merge_kernel(oa_ref, ob_ref, oc_ref, ga_ref, gb_ref, gc_ref, wa_ref, wb_ref, wc_ref, o_ref):
    ya = _dot(oa_ref[...], wa_ref[...])
    yb = _dot(ob_ref[...], wb_ref[...])
    yc = _dot(oc_ref[...], wc_ref[...])
    merged = (jax.nn.sigmoid(ga_ref[...]) * ya + jax.nn.sigmoid(gb_ref[...]) * yb
              + jax.nn.sigmoid(gc_ref[...]) * yc)
    o_ref[...] = merged.astype(BF16)


def _merge(oa, ob, oc, z, w_branch, tm, tn):
    m = oa.shape[0]
    n = D_MODEL
    nb = n // tn
    a_blocks = A_WIDTH // GLA_WIDTH
    return pl.pallas_call(
        _merge_kernel,
        grid=(m // tm, nb),
        in_specs=[pl.BlockSpec((tm, A_WIDTH), lambda i, j: (i, 0)),
                  pl.BlockSpec((tm, GLA_WIDTH), lambda i, j: (i, 0)),
                  pl.BlockSpec((tm, RET_WIDTH), lambda i, j: (i, 0)),
                  pl.BlockSpec((tm, tn), lambda i, j: (i, j)),
                  pl.BlockSpec((tm, tn), lambda i, j: (i, j + nb)),
                  pl.BlockSpec((tm, tn), lambda i, j: (i, j + 2 * nb)),
                  pl.BlockSpec((A_WIDTH, tn), lambda i, j: (0, j)),
                  pl.BlockSpec((GLA_WIDTH, tn), lambda i, j: (a_blocks, j)),
                  pl.BlockSpec((RET_WIDTH, tn), lambda i, j: (a_blocks + 1, j))],
        out_specs=pl.BlockSpec((tm, tn), lambda i, j: (i, j)),
        out_shape=jax.ShapeDtypeStruct((m, n), BF16),
        compiler_params=_params(2),
        name="merge",
    )(oa, ob, oc, z, z, z, w_branch, w_branch, w_branch)


def _head_rmsnorm(x, g):
    outs = []
    for h in range(A_HEADS):
        xh = x[:, h * A_HEAD_DIM:(h + 1) * A_HEAD_DIM]
        ms = jnp.mean(xh * xh, axis=-1, keepdims=True)
        outs.append(xh * lax.rsqrt(ms + EPS) * g)
    return jnp.concatenate(outs, axis=-1)


def _prep_a_kernel(aq_ref, ak_ref, av_ref, iq_ref, misc_ref, gq_ref, gk_ref,
                   qn_ref, kf_ref, kb_ref, vf_ref, vb_ref, iqm_ref, ik2_ref):
    qn_ref[...] = _head_rmsnorm(aq_ref[...], gq_ref[...]).astype(BF16)
    kn = _head_rmsnorm(ak_ref[...], gk_ref[...])
    kf_ref[...] = kn
    kb_ref[...] = kn.astype(BF16)
    v = av_ref[...]
    vf_ref[...] = v
    vb_ref[...] = v.astype(BF16)
    iq = iq_ref[...].astype(BF16)
    lane = lax.broadcasted_iota(jnp.int32, (iq.shape[0], LANES), 1)
    zero = jnp.zeros((iq.shape[0], LANES), BF16)
    for h in range(IDX_HEADS):
        pair = iq[:, (h // 2) * LANES:(h // 2 + 1) * LANES]
        keep = (lane < IDX_DIM) if h % 2 == 0 else (lane >= IDX_DIM)
        iqm_ref[:, h * LANES:(h + 1) * LANES] = jnp.where(keep, pair, zero)
    ik = misc_ref[...][:, :IDX_DIM].astype(BF16)
    ik2_ref[...] = jnp.concatenate([ik, ik], axis=-1)


def _prep_a(z, gq, gk, tm):
    m = z.shape[0]
    blk = COL_AQ // A_WIDTH
    wide = lambda c: pl.BlockSpec((tm, A_WIDTH), lambda i: (i, c))
    row = pl.BlockSpec((tm, A_WIDTH), lambda i: (i, 0))
    return pl.pallas_call(
        _prep_a_kernel,
        grid=(m // tm,),
        in_specs=[wide(blk), wide(blk + 1), wide(blk + 2), wide(blk + 3),
                  pl.BlockSpec((tm, LANES), lambda i: (i, COL_MISC // LANES)),
                  pl.BlockSpec((1, A_HEAD_DIM), lambda i: (0, 0)),
                  pl.BlockSpec((1, A_HEAD_DIM), lambda i: (0, 0))],
        out_specs=[row, row, row, row, row,
                   pl.BlockSpec((tm, IDX_HEADS * LANES), lambda i: (i, 0)),
                   pl.BlockSpec((tm, LANES), lambda i: (i, 0))],
        out_shape=[jax.ShapeDtypeStruct((m, A_WIDTH), BF16),
                   jax.ShapeDtypeStruct((m, A_WIDTH), F32),
                   jax.ShapeDtypeStruct((m, A_WIDTH), BF16),
                   jax.ShapeDtypeStruct((m, A_WIDTH), F32),
                   jax.ShapeDtypeStruct((m, A_WIDTH), BF16),
                   jax.ShapeDtypeStruct((m, IDX_HEADS * LANES), BF16),
                   jax.ShapeDtypeStruct((m, LANES), BF16)],
        compiler_params=_params(1),
        name="prep_a",
    )(z, z, z, z, z, gq, gk)


def _sortable_key(score):
    bits = pltpu.bitcast(score, jnp.int32)
    return bits ^ ((bits >> 31) & jnp.int32(0x7FFFFFFF))


def _topk_threshold(count_ge, rows, topk):
    def body(it, ans):
        cand = ans + lax.shift_left(jnp.int32(1), jnp.int32(31) - it)
        return jnp.where(count_ge(cand) >= float(topk), cand, ans)

    ans = lax.fori_loop(0, 32, body, jnp.full((rows, 1), INT_MIN, jnp.int32))
    return jnp.maximum(ans, jnp.int32(INT_MIN + 1))


def _a_prompt_kernel(cfar_ref, qn_ref, iqm_ref, mq_ref, kb_ref, vb_ref, ik2_ref, band_ref,
                     o_ref, s_scr, w_scr, acc_scr, m_scr, l_scr, thr_scr, *, topk):
    i = pl.program_id(1)
    tq = qn_ref.shape[1]
    kt = tq
    scale = A_HEAD_DIM ** -0.5
    iw_scale = (IDX_HEADS * IDX_DIM) ** -0.5

    iw = mq_ref[0][:, MISC_IW:MISC_IW + IDX_HEADS] * iw_scale
    for h in range(IDX_HEADS):
        w_scr[h] = jnp.broadcast_to(iw[:, h:h + 1], (tq, LANES))

    def score_tile(j, diagonal):
        k0 = pl.multiple_of(j * kt, kt)
        ik = ik2_ref[0, pl.ds(k0, kt), :]
        acc = jnp.zeros((tq, kt), F32)
        for h in range(IDX_HEADS):
            x = _dot_nt(iqm_ref[0, :, h * LANES:(h + 1) * LANES], ik)
            acc = acc + jnp.maximum(x, 0.0) * jnp.tile(w_scr[h], (1, kt // LANES))
        key = _sortable_key(acc)
        if diagonal:
            r = lax.broadcasted_iota(jnp.int32, (tq, kt), 0)
            c = lax.broadcasted_iota(jnp.int32, (tq, kt), 1)
            key = jnp.where(c <= r, key, jnp.int32(INT_MIN))
        s_scr[:, pl.ds(k0, kt)] = key

    def score_body(j, carry):
        score_tile(j, False)
        return carry

    lax.fori_loop(0, i, score_body, 0)
    score_tile(i, True)

    for rc in range(tq // ROW_CHUNK):
        rows = slice(rc * ROW_CHUNK, (rc + 1) * ROW_CHUNK)

        def count_ge(t, rows=rows):
            tb = jnp.broadcast_to(t, (ROW_CHUNK, LANES))

            def add_tile(j, cnt):
                k0 = pl.multiple_of(j * kt, kt)
                keys = s_scr[rows, pl.ds(k0, kt)]
                for c in range(kt // LANES):
                    cnt = cnt + jnp.where(keys[:, c * LANES:(c + 1) * LANES] >= tb,
                                          jnp.int32(1), jnp.int32(0))
                return cnt

            def add_pair(jj, cnt):
                return add_tile(2 * jj + 1, add_tile(2 * jj, cnt))

            n_pairs = (i + 1) // 2
            cnt = lax.fori_loop(0, n_pairs, add_pair, jnp.zeros((ROW_CHUNK, LANES), jnp.int32))
            cnt = lax.fori_loop(2 * n_pairs, i + 1, add_tile, cnt)
            return jnp.sum(cnt.astype(F32), axis=-1, keepdims=True)

        thr_scr[rows] = jnp.broadcast_to(_topk_threshold(count_ge, ROW_CHUNK, topk), (ROW_CHUNK, LANES))

    m_scr[...] = jnp.full(m_scr.shape, NEG_BIG, F32)
    l_scr[...] = jnp.zeros(l_scr.shape, F32)
    acc_scr[...] = jnp.zeros(acc_scr.shape, F32)
    c_qk = scale * LOG2_E

    def qk(k0, h, rows):
        hs = slice(h * A_HEAD_DIM, (h + 1) * A_HEAD_DIM)
        return _dot_nt(qn_ref[0, rows, hs], kb_ref[0, pl.ds(k0, kt), hs])

    def mask_of(k0, rows):
        return pltpu.bitcast(s_scr[rows, pl.ds(k0, kt)], F32)

    def halves_max(x):
        out = x[:, :LANES]
        for c in range(1, kt // LANES):
            out = jnp.maximum(out, x[:, c * LANES:(c + 1) * LANES])
        return out

    def halves_sum(x):
        out = x[:, :LANES]
        for c in range(1, kt // LANES):
            out = out + x[:, c * LANES:(c + 1) * LANES]
        return out

    def max_tile(j, band_off):
        k0 = pl.multiple_of(j * kt, kt)
        thr_t = jnp.tile(thr_scr[...], (1, kt // LANES))
        mask = jnp.where(s_scr[:, pl.ds(k0, kt)] >= thr_t, 0.0, NEG_BIG)
        s_scr[:, pl.ds(k0, kt)] = pltpu.bitcast(mask, jnp.int32)
        for h in range(A_HEADS):
            for rc in range(tq // ROW_CHUNK):
                rows = slice(rc * ROW_CHUNK, (rc + 1) * ROW_CHUNK)
                if band_off is None:
                    top = halves_max(qk(k0, h, rows) + mask_of(k0, rows)) * c_qk + cfar_ref[h]
                else:
                    top = halves_max(qk(k0, h, rows) * c_qk + band_ref[h, rows, band_off:band_off + kt]
                                     + mask_of(k0, rows))
                m_scr[h, rows] = jnp.maximum(m_scr[h, rows], top)

    def sum_tile(j, band_off):
        k0 = pl.multiple_of(j * kt, kt)
        for h in range(A_HEADS):
            hs = slice(h * A_HEAD_DIM, (h + 1) * A_HEAD_DIM)
            for rc in range(tq // ROW_CHUNK):
                rows = slice(rc * ROW_CHUNK, (rc + 1) * ROW_CHUNK)
                if band_off is None:
                    shift = jnp.tile(cfar_ref[h] - m_scr[h, rows], (1, kt // LANES))
                else:
                    shift = (band_ref[h, rows, band_off:band_off + kt]
                             - jnp.tile(m_scr[h, rows], (1, kt // LANES)))
                p = jnp.exp2(qk(k0, h, rows) * c_qk + shift + mask_of(k0, rows))
                l_scr[h, rows] = l_scr[h, rows] + halves_sum(p)
                acc_scr[rows, hs] = acc_scr[rows, hs] + _dot(p.astype(BF16), vb_ref[0, pl.ds(k0, kt), hs])

    def over_tiles(tile_fn):
        def far_body(j, carry):
            tile_fn(j, None)
            return carry

        lax.fori_loop(0, jnp.maximum(i - 1, 0), far_body, 0)

        @pl.when(i >= 1)
        def _():
            tile_fn(i - 1, 0)

        tile_fn(i, kt)

    over_tiles(max_tile)
    for h in range(A_HEADS):
        m_scr[h] = jnp.broadcast_to(jnp.max(m_scr[h], axis=-1, keepdims=True), (tq, LANES))
    over_tiles(sum_tile)
    for h in range(A_HEADS):
        hs = slice(h * A_HEAD_DIM, (h + 1) * A_HEAD_DIM)
        o_ref[0, :, hs] = (acc_scr[:, hs] / jnp.sum(l_scr[h], axis=-1, keepdims=True)).astype(BF16)


def _rel_bucket(dist):
    n = jnp.maximum(dist, 0)
    max_exact = REL_BUCKETS // 2
    nf = jnp.maximum(n, 1).astype(F32)
    large = max_exact + (jnp.log(nf / max_exact) / math.log(REL_MAX_DIST / max_exact)
                         * (REL_BUCKETS - max_exact)).astype(jnp.int32)
    large = jnp.minimum(large, REL_BUCKETS - 1)
    return jnp.where(n < max_exact, n, large)


def _bias_band(rel_table, rows, kt):
    d = jnp.arange(rows)[:, None] + kt - jnp.arange(2 * kt)[None, :]
    hit = _rel_bucket(d)[None, :, :, None] == jnp.arange(REL_BUCKETS)
    return jnp.sum(jnp.where(hit, rel_table.T.astype(F32)[:, None, None, :], 0.0), axis=-1)


def _a_prompt(qn, iqm, z, kb, vb, ik2, band, cfar, batch, seq, tq):
    topk = min(TOPK_MAX, seq // 4)
    r3 = lambda a: a.reshape(batch, seq, a.shape[-1])
    qblk = lambda w: pl.BlockSpec((1, tq, w), lambda b, i: (b, i, 0))
    full = lambda w: pl.BlockSpec((1, seq, w), lambda b, i: (b, 0, 0), pipeline_mode=pl.Buffered(1))
    out = pl.pallas_call(
        functools.partial(_a_prompt_kernel, topk=topk),
        grid=(batch, seq // tq),
        in_specs=[pl.BlockSpec(memory_space=pltpu.SMEM),
                  qblk(A_WIDTH), qblk(IDX_HEADS * LANES),
                  pl.BlockSpec((1, tq, LANES), lambda b, i: (b, i, COL_MISC // LANES)),
                  full(A_WIDTH), full(A_WIDTH), full(LANES),
                  pl.BlockSpec((A_HEADS, tq, 2 * tq), lambda b, i: (0, 0, 0),
                               pipeline_mode=pl.Buffered(1))],
        out_specs=qblk(A_WIDTH),
        scratch_shapes=[pltpu.VMEM((tq, seq), jnp.int32),
                        pltpu.VMEM((IDX_HEADS, tq, LANES), F32),
                        pltpu.VMEM((tq, A_WIDTH), F32),
                        pltpu.VMEM((A_HEADS, tq, LANES), F32),
                        pltpu.VMEM((A_HEADS, tq, LANES), F32),
                        pltpu.VMEM((tq, LANES), jnp.int32)],
        out_shape=jax.ShapeDtypeStruct((batch, seq, A_WIDTH), BF16),
        compiler_params=_params(2),
        name="a_prompt",
    )(cfar, r3(qn), r3(iqm), r3(z), r3(kb), r3(vb), r3(ik2), band)
    return out.reshape(batch * seq, A_WIDTH)


def _a_sample_score_kernel(pt_ref, iq_ref, w_ref, mnew_ref, *rest, pages_per_step, n_pages, topk):
    page_refs = rest[:pages_per_step]
    keys_ref, thr_ref = rest[pages_per_step:]
    g = pl.program_id(1)
    rows = SAMPLE_ROWS
    iq = iq_ref[0]
    w = w_ref[0]

    def head_sum(x):
        return jnp.sum((jnp.maximum(x, 0.0) * w).reshape(IDX_HEADS, rows, PAGE_SIZE), axis=0)

    for p in range(pages_per_step):
        off = pl.multiple_of((g * pages_per_step + p) * PAGE_SIZE, PAGE_SIZE)
        keys_ref[0, :, pl.ds(off, PAGE_SIZE)] = _sortable_key(
            head_sum(_dot(iq, page_refs[p][...].astype(BF16))))

    @pl.when(g == pl.num_programs(1) - 1)
    def _():
        past = n_pages * PAGE_SIZE
        ik_new = mnew_ref[0][:, :IDX_DIM].astype(BF16)
        ik_new = jnp.concatenate([ik_new, jnp.zeros((PAGE_SIZE - rows, IDX_DIM), BF16)], axis=0)
        r = lax.broadcasted_iota(jnp.int32, (rows, PAGE_SIZE), 0)
        c = lax.broadcasted_iota(jnp.int32, (rows, PAGE_SIZE), 1)
        keys_ref[0, :, past:past + PAGE_SIZE] = jnp.where(
            c <= r, _sortable_key(head_sum(_dot_nt(iq, ik_new))), jnp.int32(INT_MIN))

        def count_ge(t):
            def body(j, cnt):
                off = pl.multiple_of(j * PAGE_SIZE, PAGE_SIZE)
                return cnt + jnp.where(keys_ref[0, :, pl.ds(off, PAGE_SIZE)] >= t,
                                       jnp.int32(1), jnp.int32(0))
            cnt = lax.fori_loop(0, n_pages + 1, body, jnp.zeros((rows, PAGE_SIZE), jnp.int32))
            return jnp.sum(cnt.astype(F32), axis=-1, keepdims=True)

        thr = _topk_threshold(count_ge, rows, topk)
        thr_ref[0] = jnp.broadcast_to(thr, (rows, LANES))


def _a_sample_scores(page_table, iq_hq, w_hq, z3, cache_kidx, layer, n_tok, pages_per_step):
    batch, n_pages = page_table.shape
    rows = SAMPLE_ROWS
    nk = (n_pages + 1) * PAGE_SIZE
    topk = min(TOPK_MAX, (n_pages * PAGE_SIZE + n_tok) // 4)

    def page_spec(p):
        return pl.BlockSpec((None, None, IDX_DIM, PAGE_SIZE),
                            lambda b, g, pt: (layer, pt[b, g * pages_per_step + p], 0, 0))

    return pl.pallas_call(
        functools.partial(_a_sample_score_kernel, pages_per_step=pages_per_step, n_pages=n_pages,
                          topk=topk),
        grid_spec=pltpu.PrefetchScalarGridSpec(
            num_scalar_prefetch=1,
            grid=(batch, n_pages // pages_per_step),
            in_specs=[pl.BlockSpec((1, IDX_HEADS * rows, IDX_DIM), lambda b, g, pt: (b, 0, 0)),
                      pl.BlockSpec((1, IDX_HEADS * rows, 1), lambda b, g, pt: (b, 0, 0)),
                      pl.BlockSpec((1, rows, LANES), lambda b, g, pt: (b, 0, COL_MISC // LANES))]
                     + [page_spec(p) for p in range(pages_per_step)],
            out_specs=[pl.BlockSpec((1, rows, nk), lambda b, g, pt: (b, 0, 0)),
                       pl.BlockSpec((1, rows, LANES), lambda b, g, pt: (b, 0, 0))]),
        out_shape=[jax.ShapeDtypeStruct((batch, rows, nk), jnp.int32),
                   jax.ShapeDtypeStruct((batch, rows, LANES), jnp.int32)],
        compiler_params=_params(2),
        name="a_sample_scores",
    )(page_table, iq_hq, w_hq, z3, *([cache_kidx.transpose(0, 1, 3, 2)] * pages_per_step))


def _a_sample_attn_kernel(pt_ref, qall_ref, keys_ref, thr_ref, knew_ref, vnew_ref, bias_far_ref, bias_last_ref,
                          bias_new_ref, expand_ref, *rest, pages_per_step, n_pages):
    k_refs = rest[:pages_per_step]
    v_refs = rest[pages_per_step:2 * pages_per_step]
    o_ref, acc_scr, m_scr, l_scr, lg_scr = rest[2 * pages_per_step:]
    g = pl.program_id(1)
    last_step = pl.num_programs(1) - 1
    rows = SAMPLE_ROWS
    hq = A_HEADS * rows
    cols = PAGE_SIZE * A_HEADS
    scale = A_HEAD_DIM ** -0.5
    thr = thr_ref[0][:, :1]
    qall = qall_ref[0]

    @pl.when(g == 0)
    def _():
        m_scr[...] = jnp.full(m_scr.shape, NEG_BIG, F32)
        l_scr[...] = jnp.zeros(l_scr.shape, F32)
        acc_scr[...] = jnp.zeros(acc_scr.shape, F32)

    def masked_logits(key_tile, expand, kf, bias):
        hit = jnp.where(key_tile >= thr, 1.0, 0.0).astype(BF16)
        drop = (_dot(hit, expand) - 1.0) * (-NEG_BIG)
        return _dot_nt(qall, kf) * scale + bias + jnp.concatenate([drop] * A_HEADS, axis=0)

    def lane_fold(x, op):
        out = x[:, :LANES]
        for c in range(1, x.shape[1] // LANES):
            out = op(out, x[:, c * LANES:(c + 1) * LANES])
        return out

    def update(n_tiles, width, v_of):
        top = lane_fold(lg_scr[:, :n_tiles * width], jnp.maximum)
        m_old = m_scr[...]
        m_new = jnp.maximum(m_old, jnp.max(top, axis=-1, keepdims=True))
        alpha = jnp.exp(m_old - m_new)
        part = jnp.zeros((hq, LANES), F32)
        acc = alpha * acc_scr[...]
        for t in range(n_tiles):
            p = jnp.exp(lg_scr[:, t * width:(t + 1) * width] - m_new)
            part = part + lane_fold(p, jnp.add)
            acc = acc + _dot(p.astype(BF16), v_of(t))
        l_scr[...] = alpha * l_scr[...] + jnp.sum(part, axis=-1, keepdims=True)
        acc_scr[...] = acc
        m_scr[...] = m_new

    for p in range(pages_per_step):
        page = g * pages_per_step + p
        off = pl.multiple_of(page * PAGE_SIZE, PAGE_SIZE)
        bias = bias_far_ref[...]
        if p == pages_per_step - 1:
            bias = jnp.where(g == last_step, bias_last_ref[...], bias)
        lg_scr[:, p * cols:(p + 1) * cols] = masked_logits(
            keys_ref[0, :, pl.ds(off, PAGE_SIZE)], expand_ref[...],
            k_refs[p][...].reshape(cols, A_HEAD_DIM).astype(BF16), bias)
    update(pages_per_step, cols, lambda t: v_refs[t][...].reshape(cols, A_HEAD_DIM).astype(BF16))

    @pl.when(g == last_step)
    def _():
        past = n_pages * PAGE_SIZE
        lg_scr[:, :hq] = masked_logits(keys_ref[0, :, past:past + PAGE_SIZE], expand_ref[:, :hq],
                                       knew_ref[0], bias_new_ref[...])
        update(1, hq, lambda t: vnew_ref[0])
        o_ref[0] = (acc_scr[...] / l_scr[...]).astype(BF16)


def _a_sample_attn(page_table, qall, keys, thr, knew, vnew, bias_far, bias_last, bias_new, expand,
                   cache_k, cache_v, layer, pages_per_step):
    batch, n_pages = page_table.shape
    rows = SAMPLE_ROWS
    hq = A_HEADS * rows
    nk = keys.shape[-1]

    def page_spec(p):
        return pl.BlockSpec((None, None, PAGE_SIZE, A_HEADS, A_HEAD_DIM),
                            lambda b, g, pt: (layer, pt[b, g * pages_per_step + p], 0, 0, 0))

    per_b = lambda r, w: pl.BlockSpec((1, r, w), lambda b, g, pt: (b, 0, 0))
    const = lambda r, w: pl.BlockSpec((r, w), lambda b, g, pt: (0, 0))
    return pl.pallas_call(
        functools.partial(_a_sample_attn_kernel, pages_per_step=pages_per_step, n_pages=n_pages),
        grid_spec=pltpu.PrefetchScalarGridSpec(
            num_scalar_prefetch=1,
            grid=(batch, n_pages // pages_per_step),
            in_specs=[per_b(hq, A_HEAD_DIM), per_b(rows, nk), per_b(rows, LANES),
                      per_b(hq, A_HEAD_DIM), per_b(hq, A_HEAD_DIM),
                      const(hq, PAGE_SIZE * A_HEADS), const(hq, PAGE_SIZE * A_HEADS), const(hq, hq),
                      const(PAGE_SIZE, PAGE_SIZE * A_HEADS)]
                     + [page_spec(p) for p in range(pages_per_step)] * 2,
            out_specs=per_b(hq, A_HEAD_DIM),
            scratch_shapes=[pltpu.VMEM((hq, A_HEAD_DIM), F32),
                            pltpu.VMEM((hq, 1), F32),
                            pltpu.VMEM((hq, 1), F32),
                            pltpu.VMEM((hq, pages_per_step * PAGE_SIZE * A_HEADS), F32)]),
        out_shape=jax.ShapeDtypeStruct((batch, hq, A_HEAD_DIM), BF16),
        compiler_params=_params(2),
        name="a_sample_attn",
    )(page_table, qall, keys, thr, knew, vnew, bias_far, bias_last, bias_new, expand,
      *([cache_k] * pages_per_step), *([cache_v] * pages_per_step))


def _pad_rows(x, rows):
    if x.shape[0] == rows:
        return x
    return jnp.concatenate([x, jnp.zeros((rows - x.shape[0], x.shape[1]), x.dtype)], axis=0)


def _split3(x):
    hi = x.astype(BF16)
    r1 = x - hi.astype(F32)
    mid = r1.astype(BF16)
    lo = (r1 - mid.astype(F32)).astype(BF16)
    return hi, mid, lo


def _gla_kernel(gq_ref, gk_ref, gv_ref, gg_ref, misc_ref, wa_ref, ba_ref, gn_ref, s0_ref,
                o_ref, sfin_ref, st_scr, oacc_scr, *, chunk, rows, n_valid):
    tt = gq_ref.shape[1]

    @pl.when(pl.program_id(1) == 0)
    def _():
        st_scr[...] = s0_ref[0]

    ga = _pad_rows(misc_ref[0][:, MISC_GA:MISC_GA + GLA_RANK], rows)
    x = _dot(ga.astype(BF16), wa_ref[...]) + ba_ref[...]
    la = (jnp.minimum(x, 0.0) - jnp.log1p(jnp.exp(-jnp.abs(x)))) * (1.0 / GLA_TAU)
    r = lax.broadcasted_iota(jnp.int32, (rows, rows), 0)
    c = lax.broadcasted_iota(jnp.int32, (rows, rows), 1)
    if n_valid < rows:
        rr = lax.broadcasted_iota(jnp.int32, la.shape, 0)
        la = jnp.where(rr < n_valid, la, 0.0)
    tri = jnp.where((c <= r) & (r // chunk == c // chunk), 1.0, 0.0).astype(BF16)
    hi, mid, lo = _split3(la)
    b = _dot(tri, hi) + _dot(tri, mid) + _dot(tri, lo)

    gq = _pad_rows(gq_ref[0], rows) * (GLA_DK ** -0.5)
    gk = _pad_rows(gk_ref[0], rows)
    if n_valid < rows:
        gk = jnp.where(rr < n_valid, gk, 0.0)
    gv = _pad_rows(gv_ref[0], rows)
    causal = (lax.broadcasted_iota(jnp.int32, (chunk, chunk), 1)
              <= lax.broadcasted_iota(jnp.int32, (chunk, chunk), 0))

    for ci in range(rows // chunk):
        rs = slice(ci * chunk, (ci + 1) * chunk)
        for h in range(GLA_HEADS):
            ks = slice(h * GLA_DK, (h + 1) * GLA_DK)
            vs = slice(h * GLA_DV, (h + 1) * GLA_DV)
            bc = b[rs, ks]
            bend = bc[chunk - 1:chunk, :]
            bmid = bc[chunk // 2 - 1:chunk // 2, :]
            qt = (gq[rs, ks] * jnp.exp(bc)).astype(BF16)
            qm = (gq[rs, ks] * jnp.exp(bc - bmid)).astype(BF16)
            km = (gk[rs, ks] * jnp.exp(bmid - bc)).astype(BF16)
            kd = (gk[rs, ks] * jnp.exp(bend - bc)).astype(BF16)
            vh = gv[rs, vs].astype(BF16)
            st = st_scr[h]
            sc = jnp.where(causal, _dot_nt(qm, km), 0.0)
            oacc_scr[rs, vs] = _dot(sc.astype(BF16), vh) + _dot_nt(qt, st.astype(BF16))
            st_scr[h] = st * jnp.exp(bend) + _dot_tn(vh, kd)

    gg = gg_ref[0]
    for h in range(GLA_HEADS):
        vs = slice(h * GLA_DV, (h + 1) * GLA_DV)
        oh = oacc_scr[0:tt, vs]
        ms = jnp.mean(oh * oh, axis=-1, keepdims=True)
        g = gg[:, vs]
        o_ref[0, :, vs] = (oh * lax.rsqrt(ms + EPS) * gn_ref[...] * (g * jax.nn.sigmoid(g))).astype(BF16)

    @pl.when(pl.program_id(1) == pl.num_programs(1) - 1)
    def _():
        sfin_ref[0] = st_scr[...]


def _gla(z3, wa, ba, gn, s0t, tt, chunk, rows, n_valid):
    batch, seq, _ = z3.shape
    q_blk = COL_GQ // 256
    v_blk = COL_GV // GLA_WIDTH
    tok = lambda w, cblk: pl.BlockSpec((1, tt, w), lambda b, t: (b, t, cblk))
    const = lambda shape: pl.BlockSpec(shape, lambda b, t: (0,) * len(shape))
    st_spec = pl.BlockSpec((1, GLA_HEADS, GLA_DV, GLA_DK), lambda b, t: (b, 0, 0, 0))
    return pl.pallas_call(
        functools.partial(_gla_kernel, chunk=chunk, rows=rows, n_valid=n_valid),
        grid=(batch, seq // tt),
        in_specs=[tok(256, q_blk), tok(256, q_blk + 1), tok(GLA_WIDTH, v_blk), tok(GLA_WIDTH, v_blk + 1),
                  tok(LANES, COL_MISC // LANES),
                  const((GLA_RANK, GLA_HEADS * GLA_DK)), const((1, GLA_HEADS * GLA_DK)),
                  const((1, GLA_DV)), st_spec],
        out_specs=[pl.BlockSpec((1, tt, GLA_WIDTH), lambda b, t: (b, t, 0)), st_spec],
        out_shape=[jax.ShapeDtypeStruct((batch, seq, GLA_WIDTH), BF16),
                   jax.ShapeDtypeStruct((batch, GLA_HEADS, GLA_DV, GLA_DK), F32)],
        scratch_shapes=[pltpu.VMEM((GLA_HEADS, GLA_DV, GLA_DK), F32),
                        pltpu.VMEM((rows, GLA_WIDTH), F32)],
        compiler_params=_params(2),
        name="gla",
    )(z3, z3, z3, z3, z3, wa, ba, gn, s0t)


def _ret_kernel(rq_ref, rk_ref, rv_ref, rg_ref, cos_ref, sin_ref, gn_ref, s0_ref,
                o_ref, sfin_ref, st_scr, oacc_scr, *, rows, n_valid, log_gamma):
    tt = rq_ref.shape[1]
    width = RET_HEADS * RET_DK

    @pl.when(pl.program_id(1) == 0)
    def _():
        st_scr[...] = s0_ref[0]

    lane = lax.broadcasted_iota(jnp.int32, (rows, width), 1)
    first_half = (lane % RET_DK) < (RET_DK // 2)
    cos = _pad_rows(cos_ref[...], rows)
    sin = _pad_rows(sin_ref[...], rows)

    def rope(x):
        rot = jnp.where(first_half, pltpu.roll(x, width - RET_DK // 2, axis=1),
                        pltpu.roll(x, RET_DK // 2, axis=1))
        return x * cos + rot * sin

    q = rope(_pad_rows(rq_ref[0], rows))
    k = rope(_pad_rows(rk_ref[0], rows)) * (RET_DK ** -0.5)
    v = _pad_rows(rv_ref[0], rows)
    rr = lax.broadcasted_iota(jnp.int32, (rows, RET_DK), 0)
    if n_valid < rows:
        k = jnp.where(lax.broadcasted_iota(jnp.int32, k.shape, 0) < n_valid, k, 0.0)
    t_i = lax.broadcasted_iota(jnp.int32, (rows, rows), 0)
    s_i = lax.broadcasted_iota(jnp.int32, (rows, rows), 1)
    dist = (t_i - s_i).astype(F32)
    pos1 = (rr + 1).astype(F32)
    rem = (n_valid - 1 - rr).astype(F32)

    for h in range(RET_HEADS):
        ks = slice(h * RET_DK, (h + 1) * RET_DK)
        vs = slice(h * RET_DV, (h + 1) * RET_DV)
        lg = log_gamma[h]
        decay = jnp.where(t_i >= s_i, jnp.exp(dist * lg), 0.0)
        qh = q[:, ks]
        kh = k[:, ks]
        vh = v[:, vs].astype(BF16)
        st = st_scr[h]
        sc = _dot_nt(qh.astype(BF16), kh.astype(BF16)) * decay
        q_in = (qh * jnp.exp(pos1 * lg)).astype(BF16)
        oacc_scr[:, vs] = _dot(sc.astype(BF16), vh) + _dot_nt(q_in, st.astype(BF16))
        kd = (kh * jnp.exp(rem * lg)).astype(BF16)
        st_scr[h] = st * math.exp(n_valid * lg) + _dot_tn(vh, kd)

    rg = rg_ref[0]
    for h in range(RET_HEADS):
        vs = slice(h * RET_DV, (h + 1) * RET_DV)
        oh = oacc_scr[0:tt, vs]
        oc = oh - jnp.mean(oh, axis=-1, keepdims=True)
        var = jnp.mean(oc * oc, axis=-1, keepdims=True)
        g = rg[:, vs]
        o_ref[0, :, vs] = (oc * lax.rsqrt(var + EPS) * gn_ref[...] * (g * jax.nn.sigmoid(g))).astype(BF16)

    @pl.when(pl.program_id(1) == pl.num_programs(1) - 1)
    def _():
        sfin_ref[0] = st_scr[...]


def _ret(z3, cos, sin, gn, s0t, tt, rows, n_valid):
    batch, seq, _ = z3.shape
    q_blk = COL_GQ // 256 + 2
    v_blk = COL_GV // RET_WIDTH + 2
    width = RET_HEADS * RET_DK
    log_gamma = tuple(float(np.log1p(-np.exp2(np.float32(-5.0 - h)), dtype=np.float32))
                      for h in range(RET_HEADS))
    tok = lambda w, cblk: pl.BlockSpec((1, tt, w), lambda b, t: (b, t, cblk))
    st_spec = pl.BlockSpec((1, RET_HEADS, RET_DV, RET_DK), lambda b, t: (b, 0, 0, 0))
    return pl.pallas_call(
        functools.partial(_ret_kernel, rows=rows, n_valid=n_valid, log_gamma=log_gamma),
        grid=(batch, seq // tt),
        in_specs=[tok(256, q_blk), tok(256, q_blk + 1), tok(RET_WIDTH, v_blk), tok(RET_WIDTH, v_blk + 1),
                  pl.BlockSpec((tt, width), lambda b, t: (t, 0)),
                  pl.BlockSpec((tt, width), lambda b, t: (t, 0)),
                  pl.BlockSpec((1, RET_DV), lambda b, t: (0, 0)), st_spec],
        out_specs=[pl.BlockSpec((1, tt, RET_WIDTH), lambda b, t: (b, t, 0)), st_spec],
        out_shape=[jax.ShapeDtypeStruct((batch, seq, RET_WIDTH), BF16),
                   jax.ShapeDtypeStruct((batch, RET_HEADS, RET_DV, RET_DK), F32)],
        scratch_shapes=[pltpu.VMEM((RET_HEADS, RET_DV, RET_DK), F32),
                        pltpu.VMEM((rows, RET_WIDTH), F32)],
        compiler_params=_params(2),
        name="ret",
    )(z3, z3, z3, z3, cos, sin, gn, s0t)


def _rope_tables(pos):
    half = RET_DK // 2
    freqs = ROPE_BASE ** (-jnp.arange(half, dtype=F32) / half)
    ang = pos.astype(F32)[:, None] * freqs[None, :]
    cos = jnp.cos(ang)
    sin = jnp.sin(ang)
    cos_t = jnp.tile(jnp.concatenate([cos, cos], axis=-1), (1, RET_HEADS))
    sin_t = jnp.tile(jnp.concatenate([-sin, sin], axis=-1), (1, RET_HEADS))
    return cos_t, sin_t


def _permute_w_in(w_in):
    parts = []
    acc = 0
    for s in IN_SIZES:
        parts.append(w_in[..., acc:acc + s])
        acc += s
    aq, ak, av, iq, ik, iw, gq, gk, gv, ga, gg, rq, rk, rv, rg, gates = parts
    pad = jnp.zeros(w_in.shape[:-1] + (LANES - IDX_DIM - IDX_HEADS - GLA_RANK,), w_in.dtype)
    return jnp.concatenate([gates, aq, ak, av, iq, gv, gg, rv, rg, gq, gk, rq, rk, ik, iw, ga, pad],
                           axis=-1).astype(BF16)


def _dense_tail(x, z, oa, ob, oc, w):
    m = x.shape[0]
    merged = _merge(oa, ob, oc, z, w["w_branch"], min(m, 256), D_MODEL)
    x = _matmul_residual(merged, w["w_out"], x, min(m, 512), D_MODEL, "out_proj")
    act = _norm_swiglu(x, w["norm_ffn"], w["w_ffn_in"], min(m, 1024), 512)
    return _matmul_residual(act, w["w_ffn_out"], x, min(m, 1024), 512, "ffn_out")


def _prompt_layer(x, w, batch, seq, band, cfar, cos, sin):
    tm = 512
    z = _norm_matmul(x, w["norm_mix"], w["w_in"], tm, 1920)
    qn, kf, kb, vf, vb, iqm, ik2 = _prep_a(z, w["a_q_norm"], w["a_k_norm"], tm)
    oa = _a_prompt(qn, iqm, z, kb, vb, ik2, band * LOG2_E, cfar * LOG2_E, batch, seq, 256)
    z3 = z.reshape(batch, seq, IN_PADDED)
    zero_state = jnp.zeros((batch, GLA_HEADS, GLA_DV, GLA_DK), F32)
    ob, s_gla = _gla(z3, w["gla_wa"], w["gla_ba"], w["gla_norm"], zero_state, 256, CHUNK, 256, 256)
    oc, s_ret = _ret(z3, cos, sin, w["ret_norm"], zero_state, 256, 256, 256)
    x = _dense_tail(x, z, oa, ob.reshape(-1, GLA_WIDTH), oc.reshape(-1, RET_WIDTH), w)
    kidx = z3[:, :, COL_MISC:COL_MISC + IDX_DIM]
    new = (kf.reshape(batch, seq, A_HEADS, A_HEAD_DIM), vf.reshape(batch, seq, A_HEADS, A_HEAD_DIM),
           kidx, s_gla.transpose(0, 1, 3, 2), s_ret.transpose(0, 1, 3, 2))
    return x, new


def _sample_layer(x, w, layer, batch, n_tok, page_table, cache_k, cache_v, cache_kidx,
                  s_gla0, s_ret0, bias_s, expand, cos, sin):
    rows = SAMPLE_ROWS
    hq = A_HEADS * rows
    tm = batch * rows
    z = _norm_matmul(x, w["norm_mix"], w["w_in"], tm, 1920)
    qn, kf, kb, vf, vb, _, _ = _prep_a(z, w["a_q_norm"], w["a_k_norm"], tm)
    z3 = z.reshape(batch, rows, IN_PADDED)
    iq = z3[:, :, COL_AQ + 3 * A_WIDTH:COL_AQ + 4 * A_WIDTH].reshape(batch, rows, IDX_HEADS, IDX_DIM)
    iq_hq = iq.transpose(0, 2, 1, 3).reshape(batch, IDX_HEADS * rows, IDX_DIM).astype(BF16)
    iw = z3[:, :, COL_MISC + MISC_IW:COL_MISC + MISC_IW + IDX_HEADS] * ((IDX_HEADS * IDX_DIM) ** -0.5)
    w_hq = iw.transpose(0, 2, 1).reshape(batch, IDX_HEADS * rows, 1)
    keys, thr = _a_sample_scores(page_table, iq_hq, w_hq, z3, cache_kidx, layer, n_tok, 32)
    qall = qn.reshape(batch, rows, A_HEADS, A_HEAD_DIM).transpose(0, 2, 1, 3).reshape(batch, hq, A_HEAD_DIM)
    oa = _a_sample_attn(page_table, qall, keys, thr, kb.reshape(batch, hq, A_HEAD_DIM),
                        vb.reshape(batch, hq, A_HEAD_DIM), *bias_s, expand, cache_k, cache_v, layer, 8)
    oa = oa.reshape(batch, A_HEADS, rows, A_HEAD_DIM).transpose(0, 2, 1, 3).reshape(tm, A_WIDTH)
    ob, s_gla = _gla(z3, w["gla_wa"], w["gla_ba"], w["gla_norm"], s_gla0.transpose(0, 1, 3, 2),
                     rows, LANES, LANES, n_tok)
    oc, s_ret = _ret(z3, cos, sin, w["ret_norm"], s_ret0.transpose(0, 1, 3, 2), rows, LANES, n_tok)
    x = _dense_tail(x, z, oa, ob.reshape(tm, GLA_WIDTH), oc.reshape(tm, RET_WIDTH), w)
    kidx = z3[:, :n_tok, COL_MISC:COL_MISC + IDX_DIM]
    new = (kf.reshape(batch, rows, A_HEADS, A_HEAD_DIM)[:, :n_tok],
           vf.reshape(batch, rows, A_HEADS, A_HEAD_DIM)[:, :n_tok],
           kidx, s_gla.transpose(0, 1, 3, 2), s_ret.transpose(0, 1, 3, 2))
    return x, new


def _sample_bias(rel_table):
    rows = SAMPLE_ROWS
    hq = A_HEADS * rows
    band = _bias_band(rel_table, rows, PAGE_SIZE)
    cfar_rows = jnp.repeat(rel_table[REL_BUCKETS - 1].astype(F32), rows)[:, None]
    band_last = jnp.repeat(band[:, :, :PAGE_SIZE].reshape(hq, PAGE_SIZE), A_HEADS, axis=1)
    band_new = jnp.repeat(band[:, :, PAGE_SIZE:PAGE_SIZE + rows].reshape(hq, rows), A_HEADS, axis=1)
    cols = PAGE_SIZE * A_HEADS
    other_head = (jnp.arange(hq)[:, None] // rows) != (jnp.arange(cols)[None, :] % A_HEADS)
    head_mask = jnp.where(other_head, NEG_BIG, 0.0).astype(F32)
    return cfar_rows + head_mask, band_last + head_mask, band_new + head_mask[:, :hq]


def kernel(x_prompt, x_sample, cache_k, cache_v, cache_kidx, state_gla, state_ret, page_table, rel_table, w_in, a_q_norm, a_k_norm, gla_wa, gla_ba, gla_norm, ret_norm, w_branch, w_out, norm_mix, norm_ffn, w_ffn_in, w_ffn_out):
    depth = w_in.shape[0]
    bp, tp, d = x_prompt.shape
    bs, ts, _ = x_sample.shape
    past = page_table.shape[1] * PAGE_SIZE
    tq = 256
    row = lambda a: a.reshape(1, -1).astype(F32)

    cfar = rel_table[REL_BUCKETS - 1].astype(F32)
    band_p = _bias_band(rel_table, tq, tq)
    bias_s = _sample_bias(rel_table)
    expand = jnp.asarray(np.kron(np.eye(PAGE_SIZE), np.ones((1, A_HEADS))), dtype=BF16)
    cos_p, sin_p = _rope_tables(jnp.arange(tp))
    cos_s, sin_s = _rope_tables(past + jnp.arange(SAMPLE_ROWS))

    xp = x_prompt.reshape(bp * tp, d)
    xs = jnp.pad(x_sample, ((0, 0), (0, SAMPLE_ROWS - ts), (0, 0))).reshape(bs * SAMPLE_ROWS, d)
    rows_p, rows_s = [], []
    for l in range(depth):
        w = dict(w_in=_permute_w_in(w_in[l]), a_q_norm=row(a_q_norm[l]), a_k_norm=row(a_k_norm[l]),
                 gla_wa=gla_wa[l].astype(BF16), gla_ba=row(gla_ba[l]), gla_norm=row(gla_norm[l]),
                 ret_norm=row(ret_norm[l]), w_branch=w_branch[l].astype(BF16), w_out=w_out[l].astype(BF16),
                 norm_mix=row(norm_mix[l]), norm_ffn=row(norm_ffn[l]), w_ffn_in=w_ffn_in[l].astype(BF16),
                 w_ffn_out=w_ffn_out[l].astype(BF16))
        xp, new_p = _prompt_layer(xp, w, bp, tp, band_p, cfar, cos_p, sin_p)
        xs, new_s = _sample_layer(xs, w, l, bs, ts, page_table, cache_k, cache_v, cache_kidx,
                                  state_gla[l], state_ret[l], bias_s, expand, cos_s, sin_s)
        rows_p.append(new_p)
        rows_s.append(new_s)
    outs_p = [jnp.stack(r) for r in zip(*rows_p)]
    outs_s = [jnp.stack(r) for r in zip(*rows_s)]
    y_p = xp.reshape(bp, tp, d)
    y_s = xs.reshape(bs, SAMPLE_ROWS, d)[:, :ts]
    return (y_p, y_s, *outs_p, *outs_s)
```

```python
import functools
import math

import numpy as np
import jax
import jax.numpy as jnp
from jax import lax
from jax.experimental import pallas as pl
from jax.experimental.pallas import tpu as pltpu

D_MODEL = 2048
PAGE_SIZE = 128
A_HEADS = 8
A_HEAD_DIM = 128
A_WIDTH = A_HEADS * A_HEAD_DIM
IDX_HEADS = 16
IDX_DIM = 64
TOPK_MAX = 256
REL_BUCKETS = 32
REL_MAX_DIST = 128
GLA_HEADS = 4
GLA_DK = 64
GLA_DV = 128
GLA_WIDTH = GLA_HEADS * GLA_DV
GLA_RANK = 16
GLA_TAU = 16.0
RET_HEADS = 4
RET_DK = 64
RET_DV = 128
RET_WIDTH = RET_HEADS * RET_DV
ROPE_BASE = 10000.0
CHUNK = 64
MIX_WIDTH = A_WIDTH + GLA_WIDTH + RET_WIDTH
D_FF = -(-8 * D_MODEL // (3 * 256)) * 256
EPS = 1e-6
IN_SIZES = (A_WIDTH, A_WIDTH, A_WIDTH, IDX_HEADS * IDX_DIM, IDX_DIM, IDX_HEADS,
            GLA_HEADS * GLA_DK, GLA_HEADS * GLA_DK, GLA_WIDTH, GLA_RANK, GLA_WIDTH,
            RET_HEADS * RET_DK, RET_HEADS * RET_DK, RET_WIDTH, RET_WIDTH, 3 * D_MODEL)

COL_GATES = 0
COL_AQ = 3 * D_MODEL
COL_GV = COL_AQ + 4 * A_WIDTH
COL_GQ = COL_GV + 4 * GLA_WIDTH
COL_MISC = COL_GQ + 4 * 256
IN_PADDED = COL_MISC + 128
MISC_IW = IDX_DIM
MISC_GA = IDX_DIM + IDX_HEADS

LANES = 128
SAMPLE_ROWS = 16
VMEM_LIMIT = 56 * 1024 * 1024
INT_MIN = -2147483648
NEG_BIG = -1e30
LOG2_E = math.log2(math.e)
ROW_CHUNK = 128

BF16 = jnp.bfloat16
F32 = jnp.float32
NT_DIMS = (((1,), (1,)), ((), ()))
TN_DIMS = (((0,), (0,)), ((), ()))


def _params(n_axes):
    return pltpu.CompilerParams(dimension_semantics=("arbitrary",) * n_axes,
                                vmem_limit_bytes=VMEM_LIMIT)


def _resident(constant_index):
    return pl.Buffered(1) if constant_index else None


def _dot(a, b):
    return jnp.dot(a, b, preferred_element_type=F32)


def _dot_nt(a, b):
    return lax.dot_general(a, b, NT_DIMS, preferred_element_type=F32)


def _dot_tn(a, b):
    return lax.dot_general(a, b, TN_DIMS, preferred_element_type=F32)


def _norm_matmul_kernel(x_ref, g_ref, w_ref, o_ref, hb_ref):
    @pl.when(pl.program_id(1) == 0)
    def _():
        x = x_ref[...]
        ms = jnp.mean(x * x, axis=-1, keepdims=True)
        hb_ref[...] = (x * lax.rsqrt(ms + EPS) * g_ref[...]).astype(BF16)

    o_ref[...] = _dot_nt(hb_ref[...], w_ref[...])


def _norm_matmul(x, g, w, tm, tn):
    m, d = x.shape
    n = w.shape[0]
    return pl.pallas_call(
        _norm_matmul_kernel,
        grid=(m // tm, n // tn),
        in_specs=[pl.BlockSpec((tm, d), lambda i, j: (i, 0)),
                  pl.BlockSpec((1, d), lambda i, j: (0, 0)),
                  pl.BlockSpec((tn, d), lambda i, j: (j, 0))],
        out_specs=pl.BlockSpec((tm, tn), lambda i, j: (i, j)),
        out_shape=jax.ShapeDtypeStruct((m, n), F32),
        scratch_shapes=[pltpu.VMEM((tm, d), BF16)],
        compiler_params=_params(2),
        name="in_proj",
    )(x, g, w)


def _norm_swiglu_kernel(x_ref, g_ref, wg_ref, wu_ref, o_ref, hb_ref):
    @pl.when(pl.program_id(1) == 0)
    def _():
        x = x_ref[...]
        ms = jnp.mean(x * x, axis=-1, keepdims=True)
        hb_ref[...] = (x * lax.rsqrt(ms + EPS) * g_ref[...]).astype(BF16)

    h = hb_ref[...]
    gate = _dot(h, wg_ref[...].astype(BF16))
    up = _dot(h, wu_ref[...].astype(BF16))
    o_ref[...] = (gate * jax.nn.sigmoid(gate) * up).astype(BF16)


def _norm_swiglu(x, g, w, tm, tf):
    m, d = x.shape
    f = w.shape[1] // 2
    nf = f // tf
    return pl.pallas_call(
        _norm_swiglu_kernel,
        grid=(m // tm, nf),
        in_specs=[pl.BlockSpec((tm, d), lambda i, j: (i, 0)),
                  pl.BlockSpec((1, d), lambda i, j: (0, 0)),
                  pl.BlockSpec((d, tf), lambda i, j: (0, j)),
                  pl.BlockSpec((d, tf), lambda i, j: (0, j + nf))],
        out_specs=pl.BlockSpec((tm, tf), lambda i, j: (i, j)),
        out_shape=jax.ShapeDtypeStruct((m, f), BF16),
        scratch_shapes=[pltpu.VMEM((tm, d), BF16)],
        compiler_params=_params(2),
        name="swiglu",
    )(x, g, w, w)


def _matmul_residual_kernel(a_ref, w_ref, r_ref, o_ref):
    o_ref[...] = r_ref[...] + _dot(a_ref[...], w_ref[...].astype(BF16))


def _matmul_residual(a, w, r, tm, tn, name):
    m, k = a.shape
    n = w.shape[1]
    return pl.pallas_call(
        _matmul_residual_kernel,
        grid=(m // tm, n // tn),
        in_specs=[pl.BlockSpec((tm, k), lambda i, j: (i, 0)),
                  pl.BlockSpec((k, tn), lambda i, j: (0, j), pipeline_mode=_resident(n == tn)),
                  pl.BlockSpec((tm, tn), lambda i, j: (i, j))],
        out_specs=pl.BlockSpec((tm, tn), lambda i, j: (i, j)),
        out_shape=jax.ShapeDtypeStruct((m, n), F32),
        compiler_params=_params(2),
        name=name,
    )(a, w, r)


def _merge_kernel(oa_ref, ob_ref, oc_ref, ga_ref, gb_ref, gc_ref, wa_ref, wb_ref, wc_ref, o_ref):
    ya = _dot(oa_ref[...], wa_ref[...].astype(BF16))
    yb = _dot(ob_ref[...], wb_ref[...].astype(BF16))
    yc = _dot(oc_ref[...], wc_ref[...].astype(BF16))
    merged = (jax.nn.sigmoid(ga_ref[...]) * ya + jax.nn.sigmoid(gb_ref[...]) * yb
              + jax.nn.sigmoid(gc_ref[...]) * yc)
    o_ref[...] = merged.astype(BF16)


def _merge(oa, ob, oc, z, w_branch, tm, tn):
    m = oa.shape[0]
    n = D_MODEL
    nb = n // tn
    a_blocks = A_WIDTH // GLA_WIDTH
    return pl.pallas_call(
        _merge_kernel,
        grid=(m // tm, nb),
        in_specs=[pl.BlockSpec((tm, A_WIDTH), lambda i, j: (i, 0)),
                  pl.BlockSpec((tm, GLA_WIDTH), lambda i, j: (i, 0)),
                  pl.BlockSpec((tm, RET_WIDTH), lambda i, j: (i, 0)),
                  pl.BlockSpec((tm, tn), lambda i, j: (i, j)),
                  pl.BlockSpec((tm, tn), lambda i, j: (i, j + nb)),
                  pl.BlockSpec((tm, tn), lambda i, j: (i, j + 2 * nb)),
                  pl.BlockSpec((A_WIDTH, tn), lambda i, j: (0, j), pipeline_mode=_resident(n == tn)),
                  pl.BlockSpec((GLA_WIDTH, tn), lambda i, j: (a_blocks, j), pipeline_mode=_resident(n == tn)),
                  pl.BlockSpec((RET_WIDTH, tn), lambda i, j: (a_blocks + 1, j),
                               pipeline_mode=_resident(n == tn))],
        out_specs=pl.BlockSpec((tm, tn), lambda i, j: (i, j)),
        out_shape=jax.ShapeDtypeStruct((m, n), BF16),
        compiler_params=_params(2),
        name="merge",
    )(oa, ob, oc, z, z, z, w_branch, w_branch, w_branch)


def _head_rmsnorm(x, g):
    outs = []
    for h in range(A_HEADS):
        xh = x[:, h * A_HEAD_DIM:(h + 1) * A_HEAD_DIM]
        ms = jnp.mean(xh * xh, axis=-1, keepdims=True)
        outs.append(xh * lax.rsqrt(ms + EPS) * g)
    return jnp.concatenate(outs, axis=-1)


def _prep_a_kernel(aq_ref, ak_ref, av_ref, iq_ref, misc_ref, gq_ref, gk_ref,
                   qn_ref, kf_ref, kb_ref, vf_ref, vb_ref, iqm_ref, ik2_ref):
    qn_ref[...] = _head_rmsnorm(aq_ref[...], gq_ref[...]).astype(BF16)
    kn = _head_rmsnorm(ak_ref[...], gk_ref[...])
    kf_ref[...] = kn
    kb_ref[...] = kn.astype(BF16)
    v = av_ref[...]
    vf_ref[...] = v
    vb_ref[...] = v.astype(BF16)
    iq = iq_ref[...].astype(BF16)
    lane = lax.broadcasted_iota(jnp.int32, (iq.shape[0], LANES), 1)
    zero = jnp.zeros((iq.shape[0], LANES), BF16)
    for h in range(IDX_HEADS):
        pair = iq[:, (h // 2) * LANES:(h // 2 + 1) * LANES]
        keep = (lane < IDX_DIM) if h % 2 == 0 else (lane >= IDX_DIM)
        iqm_ref[:, h * LANES:(h + 1) * LANES] = jnp.where(keep, pair, zero)
    ik = misc_ref[...][:, :IDX_DIM].astype(BF16)
    ik2_ref[...] = jnp.concatenate([ik, ik], axis=-1)


def _prep_a(z, gq, gk, tm):
    m = z.shape[0]
    blk = COL_AQ // A_WIDTH
    wide = lambda c: pl.BlockSpec((tm, A_WIDTH), lambda i: (i, c))
    row = pl.BlockSpec((tm, A_WIDTH), lambda i: (i, 0))
    return pl.pallas_call(
        _prep_a_kernel,
        grid=(m // tm,),
        in_specs=[wide(blk), wide(blk + 1), wide(blk + 2), wide(blk + 3),
                  pl.BlockSpec((tm, LANES), lambda i: (i, COL_MISC // LANES)),
                  pl.BlockSpec((1, A_HEAD_DIM), lambda i: (0, 0)),
                  pl.BlockSpec((1, A_HEAD_DIM), lambda i: (0, 0))],
        out_specs=[row, row, row, row, row,
                   pl.BlockSpec((tm, IDX_HEADS * LANES), lambda i: (i, 0)),
                   pl.BlockSpec((tm, LANES), lambda i: (i, 0))],
        out_shape=[jax.ShapeDtypeStruct((m, A_WIDTH), BF16),
                   jax.ShapeDtypeStruct((m, A_WIDTH), F32),
                   jax.ShapeDtypeStruct((m, A_WIDTH), BF16),
                   jax.ShapeDtypeStruct((m, A_WIDTH), F32),
                   jax.ShapeDtypeStruct((m, A_WIDTH), BF16),
                   jax.ShapeDtypeStruct((m, IDX_HEADS * LANES), BF16),
                   jax.ShapeDtypeStruct((m, LANES), BF16)],
        compiler_params=_params(1),
        name="prep_a",
    )(z, z, z, z, z, gq, gk)


def _sortable_key(score):
    bits = pltpu.bitcast(score, jnp.int32)
    return bits ^ ((bits >> 31) & jnp.int32(0x7FFFFFFF))


def _topk_threshold(count_ge, rows, topk):
    def body(it, ans):
        cand = ans + lax.shift_left(jnp.int32(1), jnp.int32(31) - it)
        return jnp.where(count_ge(cand) >= float(topk), cand, ans)

    ans = lax.fori_loop(0, 32, body, jnp.full((rows, 1), INT_MIN, jnp.int32))
    return jnp.maximum(ans, jnp.int32(INT_MIN + 1))


def _a_prompt_kernel(cfar_ref, qn_ref, iqm_ref, mq_ref, kb_ref, vb_ref, ik2_ref, band_ref,
                     o_ref, s_scr, w_scr, acc_scr, m_scr, l_scr, thr_scr, *, topk):
    i = pl.program_id(1)
    tq = qn_ref.shape[1]
    kt = tq
    scale = A_HEAD_DIM ** -0.5
    iw_scale = (IDX_HEADS * IDX_DIM) ** -0.5

    iw = mq_ref[0][:, MISC_IW:MISC_IW + IDX_HEADS] * iw_scale
    for h in range(IDX_HEADS):
        w_scr[h] = jnp.broadcast_to(iw[:, h:h + 1], (tq, LANES))

    def score_tile(j, diagonal):
        k0 = pl.multiple_of(j * kt, kt)
        ik = ik2_ref[0, pl.ds(k0, kt), :]
        acc = jnp.zeros((tq, kt), F32)
        for h in range(IDX_HEADS):
            x = _dot_nt(iqm_ref[0, :, h * LANES:(h + 1) * LANES], ik)
            acc = acc + jnp.maximum(x, 0.0) * jnp.tile(w_scr[h], (1, kt // LANES))
        key = _sortable_key(acc)
        if diagonal:
            r = lax.broadcasted_iota(jnp.int32, (tq, kt), 0)
            c = lax.broadcasted_iota(jnp.int32, (tq, kt), 1)
            key = jnp.where(c <= r, key, jnp.int32(INT_MIN))
        s_scr[:, pl.ds(k0, kt)] = key

    def score_body(j, carry):
        score_tile(j, False)
        return carry

    lax.fori_loop(0, i, score_body, 0)
    score_tile(i, True)

    for rc in range(tq // ROW_CHUNK):
        rows = slice(rc * ROW_CHUNK, (rc + 1) * ROW_CHUNK)

        def count_ge(t, rows=rows):
            tb = jnp.broadcast_to(t, (ROW_CHUNK, LANES))

            def add_tile(j, cnt):
                k0 = pl.multiple_of(j * kt, kt)
                keys = s_scr[rows, pl.ds(k0, kt)]
                for c in range(kt // LANES):
                    cnt = cnt + jnp.where(keys[:, c * LANES:(c + 1) * LANES] >= tb,
                                          jnp.int32(1), jnp.int32(0))
                return cnt

            def add_pair(jj, cnt):
                return add_tile(2 * jj + 1, add_tile(2 * jj, cnt))

            n_pairs = (i + 1) // 2
            cnt = lax.fori_loop(0, n_pairs, add_pair, jnp.zeros((ROW_CHUNK, LANES), jnp.int32))
            cnt = lax.fori_loop(2 * n_pairs, i + 1, add_tile, cnt)
            return jnp.sum(cnt.astype(F32), axis=-1, keepdims=True)

        thr_scr[rows] = jnp.broadcast_to(_topk_threshold(count_ge, ROW_CHUNK, topk), (ROW_CHUNK, LANES))

    m_scr[...] = jnp.full(m_scr.shape, NEG_BIG, F32)
    l_scr[...] = jnp.zeros(l_scr.shape, F32)
    acc_scr[...] = jnp.zeros(acc_scr.shape, F32)
    c_qk = scale * LOG2_E

    def qk(k0, h, rows):
        hs = slice(h * A_HEAD_DIM, (h + 1) * A_HEAD_DIM)
        return _dot_nt(qn_ref[0, rows, hs], kb_ref[0, pl.ds(k0, kt), hs])

    def mask_of(k0, rows):
        return pltpu.bitcast(s_scr[rows, pl.ds(k0, kt)], F32)

    def halves_max(x):
        out = x[:, :LANES]
        for c in range(1, kt // LANES):
            out = jnp.maximum(out, x[:, c * LANES:(c + 1) * LANES])
        return out

    def halves_sum(x):
        out = x[:, :LANES]
        for c in range(1, kt // LANES):
            out = out + x[:, c * LANES:(c + 1) * LANES]
        return out

    def max_tile(j, band_off):
        k0 = pl.multiple_of(j * kt, kt)
        thr_t = jnp.tile(thr_scr[...], (1, kt // LANES))
        mask = jnp.where(s_scr[:, pl.ds(k0, kt)] >= thr_t, 0.0, NEG_BIG)
        s_scr[:, pl.ds(k0, kt)] = pltpu.bitcast(mask, jnp.int32)
        for h in range(A_HEADS):
            for rc in range(tq // ROW_CHUNK):
                rows = slice(rc * ROW_CHUNK, (rc + 1) * ROW_CHUNK)
                if band_off is None:
                    top = halves_max(qk(k0, h, rows) + mask_of(k0, rows)) * c_qk + cfar_ref[h]
                else:
                    top = halves_max(qk(k0, h, rows) * c_qk + band_ref[h, rows, band_off:band_off + kt]
                                     + mask_of(k0, rows))
                m_scr[h, rows] = jnp.maximum(m_scr[h, rows], top)

    def sum_tile(j, band_off):
        k0 = pl.multiple_of(j * kt, kt)
        for h in range(A_HEADS):
            hs = slice(h * A_HEAD_DIM, (h + 1) * A_HEAD_DIM)
            for rc in range(tq // ROW_CHUNK):
                rows = slice(rc * ROW_CHUNK, (rc + 1) * ROW_CHUNK)
                if band_off is None:
                    shift = jnp.tile(cfar_ref[h] - m_scr[h, rows], (1, kt // LANES))
                else:
                    shift = (band_ref[h, rows, band_off:band_off + kt]
                             - jnp.tile(m_scr[h, rows], (1, kt // LANES)))
                p = jnp.exp2(qk(k0, h, rows) * c_qk + shift + mask_of(k0, rows))
                l_scr[h, rows] = l_scr[h, rows] + halves_sum(p)
                acc_scr[rows, hs] = acc_scr[rows, hs] + _dot(p.astype(BF16), vb_ref[0, pl.ds(k0, kt), hs])

    def over_tiles(tile_fn):
        def far_body(j, carry):
            tile_fn(j, None)
            return carry

        lax.fori_loop(0, jnp.maximum(i - 1, 0), far_body, 0)

        @pl.when(i >= 1)
        def _():
            tile_fn(i - 1, 0)

        tile_fn(i, kt)

    over_tiles(max_tile)
    for h in range(A_HEADS):
        m_scr[h] = jnp.broadcast_to(jnp.max(m_scr[h], axis=-1, keepdims=True), (tq, LANES))
    over_tiles(sum_tile)
    for h in range(A_HEADS):
        hs = slice(h * A_HEAD_DIM, (h + 1) * A_HEAD_DIM)
        o_ref[0, :, hs] = (acc_scr[:, hs] / jnp.sum(l_scr[h], axis=-1, keepdims=True)).astype(BF16)


def _rel_bucket(dist):
    n = jnp.maximum(dist, 0)
    max_exact = REL_BUCKETS // 2
    nf = jnp.maximum(n, 1).astype(F32)
    large = max_exact + (jnp.log(nf / max_exact) / math.log(REL_MAX_DIST / max_exact)
                         * (REL_BUCKETS - max_exact)).astype(jnp.int32)
    large = jnp.minimum(large, REL_BUCKETS - 1)
    return jnp.where(n < max_exact, n, large)


def _bias_band(rel_table, rows, kt):
    d = jnp.arange(rows)[:, None] + kt - jnp.arange(2 * kt)[None, :]
    hit = _rel_bucket(d)[None, :, :, None] == jnp.arange(REL_BUCKETS)
    return jnp.sum(jnp.where(hit, rel_table.T.astype(F32)[:, None, None, :], 0.0), axis=-1)


def _a_prompt(qn, iqm, z, kb, vb, ik2, band, cfar, batch, seq, tq):
    topk = min(TOPK_MAX, seq // 4)
    r3 = lambda a: a.reshape(batch, seq, a.shape[-1])
    qblk = lambda w: pl.BlockSpec((1, tq, w), lambda b, i: (b, i, 0))
    full = lambda w: pl.BlockSpec((1, seq, w), lambda b, i: (b, 0, 0), pipeline_mode=pl.Buffered(1))
    out = pl.pallas_call(
        functools.partial(_a_prompt_kernel, topk=topk),
        grid=(batch, seq // tq),
        in_specs=[pl.BlockSpec(memory_space=pltpu.SMEM),
                  qblk(A_WIDTH), qblk(IDX_HEADS * LANES),
                  pl.BlockSpec((1, tq, LANES), lambda b, i: (b, i, COL_MISC // LANES)),
                  full(A_WIDTH), full(A_WIDTH), full(LANES),
                  pl.BlockSpec((A_HEADS, tq, 2 * tq), lambda b, i: (0, 0, 0),
                               pipeline_mode=pl.Buffered(1))],
        out_specs=qblk(A_WIDTH),
        scratch_shapes=[pltpu.VMEM((tq, seq), jnp.int32),
                        pltpu.VMEM((IDX_HEADS, tq, LANES), F32),
                        pltpu.VMEM((tq, A_WIDTH), F32),
                        pltpu.VMEM((A_HEADS, tq, LANES), F32),
                        pltpu.VMEM((A_HEADS, tq, LANES), F32),
                        pltpu.VMEM((tq, LANES), jnp.int32)],
        out_shape=jax.ShapeDtypeStruct((batch, seq, A_WIDTH), BF16),
        compiler_params=_params(2),
        name="a_prompt",
    )(cfar, r3(qn), r3(iqm), r3(z), r3(kb), r3(vb), r3(ik2), band)
    return out.reshape(batch * seq, A_WIDTH)


def _a_sample_score_kernel(pt_ref, iq_ref, w_ref, mnew_ref, *rest, pages_per_step, n_pages, topk):
    page_refs = rest[:pages_per_step]
    keys_ref, thr_ref = rest[pages_per_step:]
    g = pl.program_id(1)
    rows = SAMPLE_ROWS
    iq = iq_ref[0]
    w = w_ref[0]

    def head_sum(x):
        return jnp.sum((jnp.maximum(x, 0.0) * w).reshape(IDX_HEADS, rows, PAGE_SIZE), axis=0)

    for p in range(pages_per_step):
        off = pl.multiple_of((g * pages_per_step + p) * PAGE_SIZE, PAGE_SIZE)
        keys_ref[0, :, pl.ds(off, PAGE_SIZE)] = _sortable_key(
            head_sum(_dot(iq, page_refs[p][...].astype(BF16))))

    @pl.when(g == pl.num_programs(1) - 1)
    def _():
        past = n_pages * PAGE_SIZE
        ik_new = mnew_ref[0][:, :IDX_DIM].astype(BF16)
        ik_new = jnp.concatenate([ik_new, jnp.zeros((PAGE_SIZE - rows, IDX_DIM), BF16)], axis=0)
        r = lax.broadcasted_iota(jnp.int32, (rows, PAGE_SIZE), 0)
        c = lax.broadcasted_iota(jnp.int32, (rows, PAGE_SIZE), 1)
        keys_ref[0, :, past:past + PAGE_SIZE] = jnp.where(
            c <= r, _sortable_key(head_sum(_dot_nt(iq, ik_new))), jnp.int32(INT_MIN))

        def count_ge(t):
            def body(j, cnt):
                off = pl.multiple_of(j * PAGE_SIZE, PAGE_SIZE)
                return cnt + jnp.where(keys_ref[0, :, pl.ds(off, PAGE_SIZE)] >= t,
                                       jnp.int32(1), jnp.int32(0))
            cnt = lax.fori_loop(0, n_pages + 1, body, jnp.zeros((rows, PAGE_SIZE), jnp.int32))
            return jnp.sum(cnt.astype(F32), axis=-1, keepdims=True)

        thr = _topk_threshold(count_ge, rows, topk)
        thr_ref[0] = jnp.broadcast_to(thr, (rows, LANES))


def _a_sample_scores(page_table, iq_hq, w_hq, z3, cache_kidx, layer, n_tok, pages_per_step):
    batch, n_pages = page_table.shape
    rows = SAMPLE_ROWS
    nk = (n_pages + 1) * PAGE_SIZE
    topk = min(TOPK_MAX, (n_pages * PAGE_SIZE + n_tok) // 4)

    def page_spec(p):
        return pl.BlockSpec((None, None, IDX_DIM, PAGE_SIZE),
                            lambda b, g, pt: (layer, pt[b, g * pages_per_step + p], 0, 0))

    return pl.pallas_call(
        functools.partial(_a_sample_score_kernel, pages_per_step=pages_per_step, n_pages=n_pages,
                          topk=topk),
        grid_spec=pltpu.PrefetchScalarGridSpec(
            num_scalar_prefetch=1,
            grid=(batch, n_pages // pages_per_step),
            in_specs=[pl.BlockSpec((1, IDX_HEADS * rows, IDX_DIM), lambda b, g, pt: (b, 0, 0)),
                      pl.BlockSpec((1, IDX_HEADS * rows, 1), lambda b, g, pt: (b, 0, 0)),
                      pl.BlockSpec((1, rows, LANES), lambda b, g, pt: (b, 0, COL_MISC // LANES))]
                     + [page_spec(p) for p in range(pages_per_step)],
            out_specs=[pl.BlockSpec((1, rows, nk), lambda b, g, pt: (b, 0, 0)),
                       pl.BlockSpec((1, rows, LANES), lambda b, g, pt: (b, 0, 0))]),
        out_shape=[jax.ShapeDtypeStruct((batch, rows, nk), jnp.int32),
                   jax.ShapeDtypeStruct((batch, rows, LANES), jnp.int32)],
        compiler_params=_params(2),
        name="a_sample_scores",
    )(page_table, iq_hq, w_hq, z3, *([cache_kidx.transpose(0, 1, 3, 2)] * pages_per_step))


def _a_sample_attn_kernel(pt_ref, qall_ref, keys_ref, thr_ref, knew_ref, vnew_ref, bias_far_ref, bias_last_ref,
                          bias_new_ref, expand_ref, *rest, pages_per_step, n_pages):
    k_refs = rest[:pages_per_step]
    v_refs = rest[pages_per_step:2 * pages_per_step]
    o_ref, acc_scr, m_scr, l_scr, lg_scr = rest[2 * pages_per_step:]
    g = pl.program_id(1)
    last_step = pl.num_programs(1) - 1
    rows = SAMPLE_ROWS
    hq = A_HEADS * rows
    cols = PAGE_SIZE * A_HEADS
    scale = A_HEAD_DIM ** -0.5
    thr = thr_ref[0][:, :1]
    qall = qall_ref[0]

    @pl.when(g == 0)
    def _():
        m_scr[...] = jnp.full(m_scr.shape, NEG_BIG, F32)
        l_scr[...] = jnp.zeros(l_scr.shape, F32)
        acc_scr[...] = jnp.zeros(acc_scr.shape, F32)

    def masked_logits(key_tile, expand, kf, bias):
        hit = jnp.where(key_tile >= thr, 1.0, 0.0).astype(BF16)
        drop = (_dot(hit, expand) - 1.0) * (-NEG_BIG)
        return _dot_nt(qall, kf) * scale + bias + jnp.concatenate([drop] * A_HEADS, axis=0)

    def lane_fold(x, op):
        out = x[:, :LANES]
        for c in range(1, x.shape[1] // LANES):
            out = op(out, x[:, c * LANES:(c + 1) * LANES])
        return out

    def update(n_tiles, width, v_of):
        top = lane_fold(lg_scr[:, :n_tiles * width], jnp.maximum)
        m_old = m_scr[...]
        m_new = jnp.maximum(m_old, jnp.max(top, axis=-1, keepdims=True))
        alpha = jnp.exp(m_old - m_new)
        part = jnp.zeros((hq, LANES), F32)
        acc = alpha * acc_scr[...]
        for t in range(n_tiles):
            p = jnp.exp(lg_scr[:, t * width:(t + 1) * width] - m_new)
            part = part + lane_fold(p, jnp.add)
            acc = acc + _dot(p.astype(BF16), v_of(t))
        l_scr[...] = alpha * l_scr[...] + jnp.sum(part, axis=-1, keepdims=True)
        acc_scr[...] = acc
        m_scr[...] = m_new

    for p in range(pages_per_step):
        page = g * pages_per_step + p
        off = pl.multiple_of(page * PAGE_SIZE, PAGE_SIZE)
        bias = bias_far_ref[...]
        if p == pages_per_step - 1:
            bias = jnp.where(g == last_step, bias_last_ref[...], bias)
        lg_scr[:, p * cols:(p + 1) * cols] = masked_logits(
            keys_ref[0, :, pl.ds(off, PAGE_SIZE)], expand_ref[...],
            k_refs[p][...].reshape(cols, A_HEAD_DIM).astype(BF16), bias)
    update(pages_per_step, cols, lambda t: v_refs[t][...].reshape(cols, A_HEAD_DIM).astype(BF16))

    @pl.when(g == last_step)
    def _():
        past = n_pages * PAGE_SIZE
        lg_scr[:, :hq] = masked_logits(keys_ref[0, :, past:past + PAGE_SIZE], expand_ref[:, :hq],
                                       knew_ref[0], bias_new_ref[...])
        update(1, hq, lambda t: vnew_ref[0])
        o_ref[0] = (acc_scr[...] / l_scr[...]).astype(BF16)


def _a_sample_attn(page_table, qall, keys, thr, knew, vnew, bias_far, bias_last, bias_new, expand,
                   cache_k, cache_v, layer, pages_per_step):
    batch, n_pages = page_table.shape
    rows = SAMPLE_ROWS
    hq = A_HEADS * rows
    nk = keys.shape[-1]

    def page_spec(p):
        return pl.BlockSpec((None, None, PAGE_SIZE, A_HEADS, A_HEAD_DIM),
                            lambda b, g, pt: (layer, pt[b, g * pages_per_step + p], 0, 0, 0))

    per_b = lambda r, w: pl.BlockSpec((1, r, w), lambda b, g, pt: (b, 0, 0))
    const = lambda r, w: pl.BlockSpec((r, w), lambda b, g, pt: (0, 0))
    return pl.pallas_call(
        functools.partial(_a_sample_attn_kernel, pages_per_step=pages_per_step, n_pages=n_pages),
        grid_spec=pltpu.PrefetchScalarGridSpec(
            num_scalar_prefetch=1,
            grid=(batch, n_pages // pages_per_step),
            in_specs=[per_b(hq, A_HEAD_DIM), per_b(rows, nk), per_b(rows, LANES),
                      per_b(hq, A_HEAD_DIM), per_b(hq, A_HEAD_DIM),
                      const(hq, PAGE_SIZE * A_HEADS), const(hq, PAGE_SIZE * A_HEADS), const(hq, hq),
                      const(PAGE_SIZE, PAGE_SIZE * A_HEADS)]
                     + [page_spec(p) for p in range(pages_per_step)] * 2,
            out_specs=per_b(hq, A_HEAD_DIM),
            scratch_shapes=[pltpu.VMEM((hq, A_HEAD_DIM), F32),
                            pltpu.VMEM((hq, 1), F32),
                            pltpu.VMEM((hq, 1), F32),
                            pltpu.VMEM((hq, pages_per_step * PAGE_SIZE * A_HEADS), F32)]),
        out_shape=jax.ShapeDtypeStruct((batch, hq, A_HEAD_DIM), BF16),
        compiler_params=_params(2),
        name="a_sample_attn",
    )(page_table, qall, keys, thr, knew, vnew, bias_far, bias_last, bias_new, expand,
      *([cache_k] * pages_per_step), *([cache_v] * pages_per_step))


def _pad_rows(x, rows):
    if x.shape[0] == rows:
        return x
    return jnp.concatenate([x, jnp.zeros((rows - x.shape[0], x.shape[1]), x.dtype)], axis=0)


def _split3(x):
    hi = x.astype(BF16)
    r1 = x - hi.astype(F32)
    mid = r1.astype(BF16)
    lo = (r1 - mid.astype(F32)).astype(BF16)
    return hi, mid, lo


def _gla_kernel(gq_ref, gk_ref, gv_ref, gg_ref, misc_ref, wa_ref, ba_ref, gn_ref, s0_ref,
                o_ref, sfin_ref, st_scr, oacc_scr, *, chunk, rows, n_valid):
    tt = gq_ref.shape[1]

    @pl.when(pl.program_id(1) == 0)
    def _():
        st_scr[...] = s0_ref[0]

    ga = _pad_rows(misc_ref[0][:, MISC_GA:MISC_GA + GLA_RANK], rows)
    x = _dot(ga.astype(BF16), wa_ref[...]) + ba_ref[...]
    la = (jnp.minimum(x, 0.0) - jnp.log1p(jnp.exp(-jnp.abs(x)))) * (1.0 / GLA_TAU)
    r = lax.broadcasted_iota(jnp.int32, (rows, rows), 0)
    c = lax.broadcasted_iota(jnp.int32, (rows, rows), 1)
    if n_valid < rows:
        rr = lax.broadcasted_iota(jnp.int32, la.shape, 0)
        la = jnp.where(rr < n_valid, la, 0.0)
    tri = jnp.where((c <= r) & (r // chunk == c // chunk), 1.0, 0.0).astype(BF16)
    hi, mid, lo = _split3(la)
    b = _dot(tri, hi) + _dot(tri, mid) + _dot(tri, lo)

    gq = _pad_rows(gq_ref[0], rows) * (GLA_DK ** -0.5)
    gk = _pad_rows(gk_ref[0], rows)
    if n_valid < rows:
        gk = jnp.where(rr < n_valid, gk, 0.0)
    gv = _pad_rows(gv_ref[0], rows)
    causal = (lax.broadcasted_iota(jnp.int32, (chunk, chunk), 1)
              <= lax.broadcasted_iota(jnp.int32, (chunk, chunk), 0))

    for ci in range(rows // chunk):
        rs = slice(ci * chunk, (ci + 1) * chunk)
        for h in range(GLA_HEADS):
            ks = slice(h * GLA_DK, (h + 1) * GLA_DK)
            vs = slice(h * GLA_DV, (h + 1) * GLA_DV)
            bc = b[rs, ks]
            bend = bc[chunk - 1:chunk, :]
            bmid = bc[chunk // 2 - 1:chunk // 2, :]
            qt = (gq[rs, ks] * jnp.exp(bc)).astype(BF16)
            qm = (gq[rs, ks] * jnp.exp(bc - bmid)).astype(BF16)
            km = (gk[rs, ks] * jnp.exp(bmid - bc)).astype(BF16)
            kd = (gk[rs, ks] * jnp.exp(bend - bc)).astype(BF16)
            vh = gv[rs, vs].astype(BF16)
            st = st_scr[h]
            sc = jnp.where(causal, _dot_nt(qm, km), 0.0)
            oacc_scr[rs, vs] = _dot(sc.astype(BF16), vh) + _dot_nt(qt, st.astype(BF16))
            st_scr[h] = st * jnp.exp(bend) + _dot_tn(vh, kd)

    gg = gg_ref[0]
    for h in range(GLA_HEADS):
        vs = slice(h * GLA_DV, (h + 1) * GLA_DV)
        oh = oacc_scr[0:tt, vs]
        ms = jnp.mean(oh * oh, axis=-1, keepdims=True)
        g = gg[:, vs]
        o_ref[0, :, vs] = (oh * lax.rsqrt(ms + EPS) * gn_ref[...] * (g * jax.nn.sigmoid(g))).astype(BF16)

    @pl.when(pl.program_id(1) == pl.num_programs(1) - 1)
    def _():
        sfin_ref[0] = st_scr[...]


def _gla(z3, wa, ba, gn, s0t, tt, chunk, rows, n_valid):
    batch, seq, _ = z3.shape
    q_blk = COL_GQ // 256
    v_blk = COL_GV // GLA_WIDTH
    tok = lambda w, cblk: pl.BlockSpec((1, tt, w), lambda b, t: (b, t, cblk))
    const = lambda shape: pl.BlockSpec(shape, lambda b, t: (0,) * len(shape))
    st_spec = pl.BlockSpec((1, GLA_HEADS, GLA_DV, GLA_DK), lambda b, t: (b, 0, 0, 0))
    return pl.pallas_call(
        functools.partial(_gla_kernel, chunk=chunk, rows=rows, n_valid=n_valid),
        grid=(batch, seq // tt),
        in_specs=[tok(256, q_blk), tok(256, q_blk + 1), tok(GLA_WIDTH, v_blk), tok(GLA_WIDTH, v_blk + 1),
                  tok(LANES, COL_MISC // LANES),
                  const((GLA_RANK, GLA_HEADS * GLA_DK)), const((1, GLA_HEADS * GLA_DK)),
                  const((1, GLA_DV)), st_spec],
        out_specs=[pl.BlockSpec((1, tt, GLA_WIDTH), lambda b, t: (b, t, 0)), st_spec],
        out_shape=[jax.ShapeDtypeStruct((batch, seq, GLA_WIDTH), BF16),
                   jax.ShapeDtypeStruct((batch, GLA_HEADS, GLA_DV, GLA_DK), F32)],
        scratch_shapes=[pltpu.VMEM((GLA_HEADS, GLA_DV, GLA_DK), F32),
                        pltpu.VMEM((rows, GLA_WIDTH), F32)],
        compiler_params=_params(2),
        name="gla",
    )(z3, z3, z3, z3, z3, wa, ba, gn, s0t)


def _ret_kernel(rq_ref, rk_ref, rv_ref, rg_ref, cos_ref, sin_ref, gn_ref, s0_ref,
                o_ref, sfin_ref, st_scr, oacc_scr, *, rows, n_valid, log_gamma):
    tt = rq_ref.shape[1]
    width = RET_HEADS * RET_DK

    @pl.when(pl.program_id(1) == 0)
    def _():
        st_scr[...] = s0_ref[0]

    lane = lax.broadcasted_iota(jnp.int32, (rows, width), 1)
    first_half = (lane % RET_DK) < (RET_DK // 2)
    cos = _pad_rows(cos_ref[...], rows)
    sin = _pad_rows(sin_ref[...], rows)

    def rope(x):
        rot = jnp.where(first_half, pltpu.roll(x, width - RET_DK // 2, axis=1),
                        pltpu.roll(x, RET_DK // 2, axis=1))
        return x * cos + rot * sin

    q = rope(_pad_rows(rq_ref[0], rows))
    k = rope(_pad_rows(rk_ref[0], rows)) * (RET_DK ** -0.5)
    v = _pad_rows(rv_ref[0], rows)
    rr = lax.broadcasted_iota(jnp.int32, (rows, RET_DK), 0)
    if n_valid < rows:
        k = jnp.where(lax.broadcasted_iota(jnp.int32, k.shape, 0) < n_valid, k, 0.0)
    t_i = lax.broadcasted_iota(jnp.int32, (rows, rows), 0)
    s_i = lax.broadcasted_iota(jnp.int32, (rows, rows), 1)
    dist = (t_i - s_i).astype(F32)
    pos1 = (rr + 1).astype(F32)
    rem = (n_valid - 1 - rr).astype(F32)

    for h in range(RET_HEADS):
        ks = slice(h * RET_DK, (h + 1) * RET_DK)
        vs = slice(h * RET_DV, (h + 1) * RET_DV)
        lg = log_gamma[h]
        decay = jnp.where(t_i >= s_i, jnp.exp(dist * lg), 0.0)
        qh = q[:, ks]
        kh = k[:, ks]
        vh = v[:, vs].astype(BF16)
        st = st_scr[h]
        sc = _dot_nt(qh.astype(BF16), kh.astype(BF16)) * decay
        q_in = (qh * jnp.exp(pos1 * lg)).astype(BF16)
        oacc_scr[:, vs] = _dot(sc.astype(BF16), vh) + _dot_nt(q_in, st.astype(BF16))
        kd = (kh * jnp.exp(rem * lg)).astype(BF16)
        st_scr[h] = st * math.exp(n_valid * lg) + _dot_tn(vh, kd)

    rg = rg_ref[0]
    for h in range(RET_HEADS):
        vs = slice(h * RET_DV, (h + 1) * RET_DV)
        oh = oacc_scr[0:tt, vs]
        oc = oh - jnp.mean(oh, axis=-1, keepdims=True)
        var = jnp.mean(oc * oc, axis=-1, keepdims=True)
        g = rg[:, vs]
        o_ref[0, :, vs] = (oc * lax.rsqrt(var + EPS) * gn_ref[...] * (g * jax.nn.sigmoid(g))).astype(BF16)

    @pl.when(pl.program_id(1) == pl.num_programs(1) - 1)
    def _():
        sfin_ref[0] = st_scr[...]


def _ret(z3, cos, sin, gn, s0t, tt, rows, n_valid):
    batch, seq, _ = z3.shape
    q_blk = COL_GQ // 256 + 2
    v_blk = COL_GV // RET_WIDTH + 2
    width = RET_HEADS * RET_DK
    log_gamma = tuple(float(np.log1p(-np.exp2(np.float32(-5.0 - h)), dtype=np.float32))
                      for h in range(RET_HEADS))
    tok = lambda w, cblk: pl.BlockSpec((1, tt, w), lambda b, t: (b, t, cblk))
    st_spec = pl.BlockSpec((1, RET_HEADS, RET_DV, RET_DK), lambda b, t: (b, 0, 0, 0))
    return pl.pallas_call(
        functools.partial(_ret_kernel, rows=rows, n_valid=n_valid, log_gamma=log_gamma),
        grid=(batch, seq // tt),
        in_specs=[tok(256, q_blk), tok(256, q_blk + 1), tok(RET_WIDTH, v_blk), tok(RET_WIDTH, v_blk + 1),
                  pl.BlockSpec((tt, width), lambda b, t: (t, 0)),
                  pl.BlockSpec((tt, width), lambda b, t: (t, 0)),
                  pl.BlockSpec((1, RET_DV), lambda b, t: (0, 0)), st_spec],
        out_specs=[pl.BlockSpec((1, tt, RET_WIDTH), lambda b, t: (b, t, 0)), st_spec],
        out_shape=[jax.ShapeDtypeStruct((batch, seq, RET_WIDTH), BF16),
                   jax.ShapeDtypeStruct((batch, RET_HEADS, RET_DV, RET_DK), F32)],
        scratch_shapes=[pltpu.VMEM((RET_HEADS, RET_DV, RET_DK), F32),
                        pltpu.VMEM((rows, RET_WIDTH), F32)],
        compiler_params=_params(2),
        name="ret",
    )(z3, z3, z3, z3, cos, sin, gn, s0t)


def _rope_tables(pos):
    half = RET_DK // 2
    freqs = ROPE_BASE ** (-jnp.arange(half, dtype=F32) / half)
    ang = pos.astype(F32)[:, None] * freqs[None, :]
    cos = jnp.cos(ang)
    sin = jnp.sin(ang)
    cos_t = jnp.tile(jnp.concatenate([cos, cos], axis=-1), (1, RET_HEADS))
    sin_t = jnp.tile(jnp.concatenate([-sin, sin], axis=-1), (1, RET_HEADS))
    return cos_t, sin_t


def _permuted_segments():
    names = ("aq", "ak", "av", "iq", "ik", "iw", "gq", "gk", "gv", "ga", "gg", "rq", "rk", "rv", "rg", "gates")
    src, acc = {}, 0
    for name, size in zip(names, IN_SIZES):
        src[name] = (acc, size)
        acc += size
    order = ("gates", "aq", "ak", "av", "iq", "gv", "gg", "rv", "rg", "gq", "gk", "rq", "rk", "ik", "iw", "ga")
    out, dst = [], 0
    for name in order:
        out.append((src[name][0], dst, src[name][1]))
        dst += src[name][1]
    return out, dst


def _permute_w_in_kernel(w_ref, o_ref):
    segments, used = _permuted_segments()
    for src, dst, size in segments:
        o_ref[dst:dst + size, :] = w_ref[src:src + size, :].astype(BF16)
    o_ref[used:, :] = jnp.zeros((IN_PADDED - used, o_ref.shape[1]), BF16)


def _permute_w_in(w_in_t):
    n, d = w_in_t.shape
    tc = LANES
    return pl.pallas_call(
        _permute_w_in_kernel,
        grid=(d // tc,),
        in_specs=[pl.BlockSpec((n, tc), lambda j: (0, j))],
        out_specs=pl.BlockSpec((IN_PADDED, tc), lambda j: (0, j)),
        out_shape=jax.ShapeDtypeStruct((IN_PADDED, d), BF16),
        compiler_params=_params(1),
        name="permute_w_in",
    )(w_in_t)


def _dense_tail(x, z, oa, ob, oc, w):
    m = x.shape[0]
    merged = _merge(oa, ob, oc, z, w["w_branch"], min(m, 256), D_MODEL)
    x = _matmul_residual(merged, w["w_out"], x, min(m, 512), D_MODEL, "out_proj")
    act = _norm_swiglu(x, w["norm_ffn"], w["w_ffn_in"], min(m, 1024), 512)
    return _matmul_residual(act, w["w_ffn_out"], x, min(m, 1024), 256, "ffn_out")


def _prompt_layer(x, w, batch, seq, band, cfar, cos, sin):
    tm = 512
    z = _norm_matmul(x, w["norm_mix"], w["w_in"], tm, 1920)
    qn, kf, kb, vf, vb, iqm, ik2 = _prep_a(z, w["a_q_norm"], w["a_k_norm"], tm)
    oa = _a_prompt(qn, iqm, z, kb, vb, ik2, band * LOG2_E, cfar * LOG2_E, batch, seq, 256)
    z3 = z.reshape(batch, seq, IN_PADDED)
    zero_state = jnp.zeros((batch, GLA_HEADS, GLA_DV, GLA_DK), F32)
    ob, s_gla = _gla(z3, w["gla_wa"], w["gla_ba"], w["gla_norm"], zero_state, 256, CHUNK, 256, 256)
    oc, s_ret = _ret(z3, cos, sin, w["ret_norm"], zero_state, 256, 256, 256)
    x = _dense_tail(x, z, oa, ob.reshape(-1, GLA_WIDTH), oc.reshape(-1, RET_WIDTH), w)
    kidx = z3[:, :, COL_MISC:COL_MISC + IDX_DIM]
    new = (kf.reshape(batch, seq, A_HEADS, A_HEAD_DIM), vf.reshape(batch, seq, A_HEADS, A_HEAD_DIM),
           kidx, s_gla.transpose(0, 1, 3, 2), s_ret.transpose(0, 1, 3, 2))
    return x, new


def _sample_layer(x, w, layer, batch, n_tok, page_table, cache_k, cache_v, cache_kidx,
                  s_gla0, s_ret0, bias_s, expand, cos, sin):
    rows = SAMPLE_ROWS
    hq = A_HEADS * rows
    tm = batch * rows
    z = _norm_matmul(x, w["norm_mix"], w["w_in"], tm, 1920)
    qn, kf, kb, vf, vb, _, _ = _prep_a(z, w["a_q_norm"], w["a_k_norm"], tm)
    z3 = z.reshape(batch, rows, IN_PADDED)
    iq = z3[:, :, COL_AQ + 3 * A_WIDTH:COL_AQ + 4 * A_WIDTH].reshape(batch, rows, IDX_HEADS, IDX_DIM)
    iq_hq = iq.transpose(0, 2, 1, 3).reshape(batch, IDX_HEADS * rows, IDX_DIM).astype(BF16)
    iw = z3[:, :, COL_MISC + MISC_IW:COL_MISC + MISC_IW + IDX_HEADS] * ((IDX_HEADS * IDX_DIM) ** -0.5)
    w_hq = iw.transpose(0, 2, 1).reshape(batch, IDX_HEADS * rows, 1)
    keys, thr = _a_sample_scores(page_table, iq_hq, w_hq, z3, cache_kidx, layer, n_tok, 32)
    qall = qn.reshape(batch, rows, A_HEADS, A_HEAD_DIM).transpose(0, 2, 1, 3).reshape(batch, hq, A_HEAD_DIM)
    oa = _a_sample_attn(page_table, qall, keys, thr, kb.reshape(batch, hq, A_HEAD_DIM),
                        vb.reshape(batch, hq, A_HEAD_DIM), *bias_s, expand, cache_k, cache_v, layer, 8)
    oa = oa.reshape(batch, A_HEADS, rows, A_HEAD_DIM).transpose(0, 2, 1, 3).reshape(tm, A_WIDTH)
    ob, s_gla = _gla(z3, w["gla_wa"], w["gla_ba"], w["gla_norm"], s_gla0.transpose(0, 1, 3, 2),
                     rows, LANES, LANES, n_tok)
    oc, s_ret = _ret(z3, cos, sin, w["ret_norm"], s_ret0.transpose(0, 1, 3, 2), rows, LANES, n_tok)
    x = _dense_tail(x, z, oa, ob.reshape(tm, GLA_WIDTH), oc.reshape(tm, RET_WIDTH), w)
    kidx = z3[:, :n_tok, COL_MISC:COL_MISC + IDX_DIM]
    new = (kf.reshape(batch, rows, A_HEADS, A_HEAD_DIM)[:, :n_tok],
           vf.reshape(batch, rows, A_HEADS, A_HEAD_DIM)[:, :n_tok],
           kidx, s_gla.transpose(0, 1, 3, 2), s_ret.transpose(0, 1, 3, 2))
    return x, new


def _sample_bias(rel_table):
    rows = SAMPLE_ROWS
    hq = A_HEADS * rows
    band = _bias_band(rel_table, rows, PAGE_SIZE)
    cfar_rows = jnp.repeat(rel_table[REL_BUCKETS - 1].astype(F32), rows)[:, None]
    band_last = jnp.repeat(band[:, :, :PAGE_SIZE].reshape(hq, PAGE_SIZE), A_HEADS, axis=1)
    band_new = jnp.repeat(band[:, :, PAGE_SIZE:PAGE_SIZE + rows].reshape(hq, rows), A_HEADS, axis=1)
    cols = PAGE_SIZE * A_HEADS
    other_head = (jnp.arange(hq)[:, None] // rows) != (jnp.arange(cols)[None, :] % A_HEADS)
    head_mask = jnp.where(other_head, NEG_BIG, 0.0).astype(F32)
    return cfar_rows + head_mask, band_last + head_mask, band_new + head_mask[:, :hq]


def kernel(x_prompt, x_sample, cache_k, cache_v, cache_kidx, state_gla, state_ret, page_table, rel_table, w_in, a_q_norm, a_k_norm, gla_wa, gla_ba, gla_norm, ret_norm, w_branch, w_out, norm_mix, norm_ffn, w_ffn_in, w_ffn_out):
    depth = w_in.shape[0]
    bp, tp, d = x_prompt.shape
    bs, ts, _ = x_sample.shape
    past = page_table.shape[1] * PAGE_SIZE
    tq = 256
    row = lambda a: a.reshape(1, -1).astype(F32)

    cfar = rel_table[REL_BUCKETS - 1].astype(F32)
    band_p = _bias_band(rel_table, tq, tq)
    bias_s = _sample_bias(rel_table)
    expand = jnp.asarray(np.kron(np.eye(PAGE_SIZE), np.ones((1, A_HEADS))), dtype=BF16)
    cos_p, sin_p = _rope_tables(jnp.arange(tp))
    cos_s, sin_s = _rope_tables(past + jnp.arange(SAMPLE_ROWS))

    xp = x_prompt.reshape(bp * tp, d)
    xs = jnp.pad(x_sample, ((0, 0), (0, SAMPLE_ROWS - ts), (0, 0))).reshape(bs * SAMPLE_ROWS, d)
    rows_p, rows_s = [], []
    for l in range(depth):
        w = dict(w_in=_permute_w_in(w_in[l].T), a_q_norm=row(a_q_norm[l]), a_k_norm=row(a_k_norm[l]),
                 gla_wa=gla_wa[l].astype(BF16), gla_ba=row(gla_ba[l]), gla_norm=row(gla_norm[l]),
                 ret_norm=row(ret_norm[l]), w_branch=w_branch[l], w_out=w_out[l],
                 norm_mix=row(norm_mix[l]), norm_ffn=row(norm_ffn[l]), w_ffn_in=w_ffn_in[l],
                 w_ffn_out=w_ffn_out[l])
        xp, new_p = _prompt_layer(xp, w, bp, tp, band_p, cfar, cos_p, sin_p)
        xs, new_s = _sample_layer(xs, w, l, bs, ts, page_table, cache_k, cache_v, cache_kidx,
                                  state_gla[l], state_ret[l], bias_s, expand, cos_s, sin_s)
        rows_p.append(new_p)
        rows_s.append(new_s)
    outs_p = [jnp.stack(r) for r in zip(*rows_p)]
    outs_s = [jnp.stack(r) for r in zip(*rows_s)]
    y_p = xp.reshape(bp, tp, d)
    y_s = xs.reshape(bs, SAMPLE_ROWS, d)[:, :ts]
    return (y_p, y_s, *outs_p, *outs_s)
```

```python
import functools
import math

import numpy as np
import jax
import jax.numpy as jnp
from jax import lax
from jax.experimental import pallas as pl
from jax.experimental.pallas import tpu as pltpu

D_MODEL = 2048
PAGE_SIZE = 128
A_HEADS = 8
A_HEAD_DIM = 128
A_WIDTH = A_HEADS * A_HEAD_DIM
IDX_HEADS = 16
IDX_DIM = 64
TOPK_MAX = 256
REL_BUCKETS = 32
REL_MAX_DIST = 128
GLA_HEADS = 4
GLA_DK = 64
GLA_DV = 128
GLA_WIDTH = GLA_HEADS * GLA_DV
GLA_RANK = 16
GLA_TAU = 16.0
RET_HEADS = 4
RET_DK = 64
RET_DV = 128
RET_WIDTH = RET_HEADS * RET_DV
ROPE_BASE = 10000.0
CHUNK = 64
MIX_WIDTH = A_WIDTH + GLA_WIDTH + RET_WIDTH
D_FF = -(-8 * D_MODEL // (3 * 256)) * 256
EPS = 1e-6
IN_SIZES = (A_WIDTH, A_WIDTH, A_WIDTH, IDX_HEADS * IDX_DIM, IDX_DIM, IDX_HEADS,
            GLA_HEADS * GLA_DK, GLA_HEADS * GLA_DK, GLA_WIDTH, GLA_RANK, GLA_WIDTH,
            RET_HEADS * RET_DK, RET_HEADS * RET_DK, RET_WIDTH, RET_WIDTH, 3 * D_MODEL)

COL_GATES = 0
COL_AQ = 3 * D_MODEL
COL_GV = COL_AQ + 4 * A_WIDTH
COL_GQ = COL_GV + 4 * GLA_WIDTH
COL_MISC = COL_GQ + 4 * 256
IN_PADDED = COL_MISC + 128
MISC_IW = IDX_DIM
MISC_GA = IDX_DIM + IDX_HEADS

LANES = 128
SAMPLE_ROWS = 16
VMEM_LIMIT = 56 * 1024 * 1024
INT_MIN = -2147483648
NEG_BIG = -1e30
LOG2_E = math.log2(math.e)
ROW_CHUNK = 128

BF16 = jnp.bfloat16
F32 = jnp.float32
NT_DIMS = (((1,), (1,)), ((), ()))
TN_DIMS = (((0,), (0,)), ((), ()))


def _params(n_axes):
    return pltpu.CompilerParams(dimension_semantics=("arbitrary",) * n_axes,
                                vmem_limit_bytes=VMEM_LIMIT)


def _resident(constant_index):
    return pl.Buffered(1) if constant_index else None


def _dot(a, b):
    return jnp.dot(a, b, preferred_element_type=F32)


def _dot_nt(a, b):
    return lax.dot_general(a, b, NT_DIMS, preferred_element_type=F32)


def _dot_tn(a, b):
    return lax.dot_general(a, b, TN_DIMS, preferred_element_type=F32)


def _norm_matmul_kernel(x_ref, g_ref, w_ref, o_ref, hb_ref):
    @pl.when(pl.program_id(1) == 0)
    def _():
        x = x_ref[...]
        ms = jnp.mean(x * x, axis=-1, keepdims=True)
        hb_ref[...] = (x * lax.rsqrt(ms + EPS) * g_ref[...]).astype(BF16)

    o_ref[...] = _dot_nt(hb_ref[...], w_ref[...])


def _norm_matmul(x, g, w, tm, tn):
    m, d = x.shape
    n = w.shape[0]
    return pl.pallas_call(
        _norm_matmul_kernel,
        grid=(m // tm, n // tn),
        in_specs=[pl.BlockSpec((tm, d), lambda i, j: (i, 0)),
                  pl.BlockSpec((1, d), lambda i, j: (0, 0)),
                  pl.BlockSpec((tn, d), lambda i, j: (j, 0))],
        out_specs=pl.BlockSpec((tm, tn), lambda i, j: (i, j)),
        out_shape=jax.ShapeDtypeStruct((m, n), F32),
        scratch_shapes=[pltpu.VMEM((tm, d), BF16)],
        compiler_params=_params(2),
        name="in_proj",
    )(x, g, w)


def _norm_swiglu_kernel(x_ref, g_ref, wg_ref, wu_ref, o_ref, hb_ref):
    @pl.when(pl.program_id(1) == 0)
    def _():
        x = x_ref[...]
        ms = jnp.mean(x * x, axis=-1, keepdims=True)
        hb_ref[...] = (x * lax.rsqrt(ms + EPS) * g_ref[...]).astype(BF16)

    h = hb_ref[...]
    gate = _dot(h, wg_ref[...].astype(BF16))
    up = _dot(h, wu_ref[...].astype(BF16))
    o_ref[...] = (gate * jax.nn.sigmoid(gate) * up).astype(BF16)


def _norm_swiglu(x, g, w, layer, tm, tf):
    m, d = x.shape
    f = w.shape[2] // 2
    nf = f // tf
    return pl.pallas_call(
        _norm_swiglu_kernel,
        grid=(m // tm, nf),
        in_specs=[pl.BlockSpec((tm, d), lambda i, j: (i, 0)),
                  pl.BlockSpec((1, d), lambda i, j: (0, 0)),
                  pl.BlockSpec((None, d, tf), lambda i, j: (layer, 0, j)),
                  pl.BlockSpec((None, d, tf), lambda i, j: (layer, 0, j + nf))],
        out_specs=pl.BlockSpec((tm, tf), lambda i, j: (i, j)),
        out_shape=jax.ShapeDtypeStruct((m, f), BF16),
        scratch_shapes=[pltpu.VMEM((tm, d), BF16)],
        compiler_params=_params(2),
        name="swiglu",
    )(x, g, w, w)


def _matmul_residual_kernel(a_ref, w_ref, r_ref, o_ref, wb_ref):
    @pl.when(pl.program_id(1) == 0)
    def _():
        wb_ref[...] = w_ref[...].astype(BF16)

    o_ref[...] = r_ref[...] + _dot(a_ref[...], wb_ref[...])


def _matmul_residual(a, w, layer, r, tm, tn, name):
    m, k = a.shape
    n = w.shape[2]
    return pl.pallas_call(
        _matmul_residual_kernel,
        grid=(n // tn, m // tm),
        in_specs=[pl.BlockSpec((tm, k), lambda j, i: (i, 0)),
                  pl.BlockSpec((None, k, tn), lambda j, i: (layer, 0, j), pipeline_mode=pl.Buffered(1)),
                  pl.BlockSpec((tm, tn), lambda j, i: (i, j))],
        out_specs=pl.BlockSpec((tm, tn), lambda j, i: (i, j)),
        out_shape=jax.ShapeDtypeStruct((m, n), F32),
        scratch_shapes=[pltpu.VMEM((k, tn), BF16)],
        compiler_params=_params(2),
        name=name,
    )(a, w, r)


def _merge_kernel(oa_ref, ob_ref, oc_ref, ga_ref, gb_ref, gc_ref, wa_ref, wb_ref, wc_ref, o_ref):
    ya = _dot(oa_ref[...], wa_ref[...].astype(BF16))
    yb = _dot(ob_ref[...], wb_ref[...].astype(BF16))
    yc = _dot(oc_ref[...], wc_ref[...].astype(BF16))
    merged = (jax.nn.sigmoid(ga_ref[...]) * ya + jax.nn.sigmoid(gb_ref[...]) * yb
              + jax.nn.sigmoid(gc_ref[...]) * yc)
    o_ref[...] = merged.astype(BF16)


def _merge(oa, ob, oc, z, w_branch, layer, tm, tn):
    m = oa.shape[0]
    n = D_MODEL
    nb = n // tn
    a_blocks = A_WIDTH // GLA_WIDTH
    return pl.pallas_call(
        _merge_kernel,
        grid=(m // tm, nb),
        in_specs=[pl.BlockSpec((tm, A_WIDTH), lambda i, j: (i, 0)),
                  pl.BlockSpec((tm, GLA_WIDTH), lambda i, j: (i, 0)),
                  pl.BlockSpec((tm, RET_WIDTH), lambda i, j: (i, 0)),
                  pl.BlockSpec((tm, tn), lambda i, j: (i, j)),
                  pl.BlockSpec((tm, tn), lambda i, j: (i, j + nb)),
                  pl.BlockSpec((tm, tn), lambda i, j: (i, j + 2 * nb)),
                  pl.BlockSpec((None, A_WIDTH, tn), lambda i, j: (layer, 0, j),
                               pipeline_mode=_resident(n == tn)),
                  pl.BlockSpec((None, GLA_WIDTH, tn), lambda i, j: (layer, a_blocks, j),
                               pipeline_mode=_resident(n == tn)),
                  pl.BlockSpec((None, RET_WIDTH, tn), lambda i, j: (layer, a_blocks + 1, j),
                               pipeline_mode=_resident(n == tn))],
        out_specs=pl.BlockSpec((tm, tn), lambda i, j: (i, j)),
        out_shape=jax.ShapeDtypeStruct((m, n), BF16),
        compiler_params=_params(2),
        name="merge",
    )(oa, ob, oc, z, z, z, w_branch, w_branch, w_branch)


def _head_rmsnorm(x, g):
    outs = []
    for h in range(A_HEADS):
        xh = x[:, h * A_HEAD_DIM:(h + 1) * A_HEAD_DIM]
        ms = jnp.mean(xh * xh, axis=-1, keepdims=True)
        outs.append(xh * lax.rsqrt(ms + EPS) * g)
    return jnp.concatenate(outs, axis=-1)


def _prep_a_kernel(aq_ref, ak_ref, av_ref, iq_ref, misc_ref, gq_ref, gk_ref,
                   qn_ref, kf_ref, kb_ref, vf_ref, vb_ref, iqm_ref, ik2_ref):
    qn_ref[...] = _head_rmsnorm(aq_ref[...], gq_ref[...]).astype(BF16)
    kn = _head_rmsnorm(ak_ref[...], gk_ref[...])
    kf_ref[...] = kn
    kb_ref[...] = kn.astype(BF16)
    v = av_ref[...]
    vf_ref[...] = v
    vb_ref[...] = v.astype(BF16)
    iq = iq_ref[...].astype(BF16)
    lane = lax.broadcasted_iota(jnp.int32, (iq.shape[0], LANES), 1)
    zero = jnp.zeros((iq.shape[0], LANES), BF16)
    for h in range(IDX_HEADS):
        pair = iq[:, (h // 2) * LANES:(h // 2 + 1) * LANES]
        keep = (lane < IDX_DIM) if h % 2 == 0 else (lane >= IDX_DIM)
        iqm_ref[:, h * LANES:(h + 1) * LANES] = jnp.where(keep, pair, zero)
    ik = misc_ref[...][:, :IDX_DIM].astype(BF16)
    ik2_ref[...] = jnp.concatenate([ik, ik], axis=-1)


def _prep_a(z, gq, gk, tm):
    m = z.shape[0]
    blk = COL_AQ // A_WIDTH
    wide = lambda c: pl.BlockSpec((tm, A_WIDTH), lambda i: (i, c))
    row = pl.BlockSpec((tm, A_WIDTH), lambda i: (i, 0))
    return pl.pallas_call(
        _prep_a_kernel,
        grid=(m // tm,),
        in_specs=[wide(blk), wide(blk + 1), wide(blk + 2), wide(blk + 3),
                  pl.BlockSpec((tm, LANES), lambda i: (i, COL_MISC // LANES)),
                  pl.BlockSpec((1, A_HEAD_DIM), lambda i: (0, 0)),
                  pl.BlockSpec((1, A_HEAD_DIM), lambda i: (0, 0))],
        out_specs=[row, row, row, row, row,
                   pl.BlockSpec((tm, IDX_HEADS * LANES), lambda i: (i, 0)),
                   pl.BlockSpec((tm, LANES), lambda i: (i, 0))],
        out_shape=[jax.ShapeDtypeStruct((m, A_WIDTH), BF16),
                   jax.ShapeDtypeStruct((m, A_WIDTH), F32),
                   jax.ShapeDtypeStruct((m, A_WIDTH), BF16),
                   jax.ShapeDtypeStruct((m, A_WIDTH), F32),
                   jax.ShapeDtypeStruct((m, A_WIDTH), BF16),
                   jax.ShapeDtypeStruct((m, IDX_HEADS * LANES), BF16),
                   jax.ShapeDtypeStruct((m, LANES), BF16)],
        compiler_params=_params(1),
        name="prep_a",
    )(z, z, z, z, z, gq, gk)


def _sortable_key(score):
    bits = pltpu.bitcast(score, jnp.int32)
    return bits ^ ((bits >> 31) & jnp.int32(0x7FFFFFFF))


def _topk_threshold(count_ge, rows, topk):
    def body(it, ans):
        cand = ans + lax.shift_left(jnp.int32(1), jnp.int32(31) - it)
        return jnp.where(count_ge(cand) >= float(topk), cand, ans)

    ans = lax.fori_loop(0, 32, body, jnp.full((rows, 1), INT_MIN, jnp.int32))
    return jnp.maximum(ans, jnp.int32(INT_MIN + 1))


def _a_prompt_kernel(cfar_ref, qn_ref, iqm_ref, mq_ref, kb_ref, vb_ref, ik2_ref, band_ref,
                     o_ref, s_scr, w_scr, acc_scr, m_scr, l_scr, thr_scr, *, topk):
    i = pl.program_id(1)
    tq = qn_ref.shape[1]
    kt = tq
    scale = A_HEAD_DIM ** -0.5
    iw_scale = (IDX_HEADS * IDX_DIM) ** -0.5

    iw = mq_ref[0][:, MISC_IW:MISC_IW + IDX_HEADS] * iw_scale
    for h in range(IDX_HEADS):
        w_scr[h] = jnp.broadcast_to(iw[:, h:h + 1], (tq, LANES))

    def score_tile(j, diagonal):
        k0 = pl.multiple_of(j * kt, kt)
        ik = ik2_ref[0, pl.ds(k0, kt), :]
        acc = jnp.zeros((tq, kt), F32)
        for h in range(IDX_HEADS):
            x = _dot_nt(iqm_ref[0, :, h * LANES:(h + 1) * LANES], ik)
            acc = acc + jnp.maximum(x, 0.0) * jnp.tile(w_scr[h], (1, kt // LANES))
        key = _sortable_key(acc)
        if diagonal:
            r = lax.broadcasted_iota(jnp.int32, (tq, kt), 0)
            c = lax.broadcasted_iota(jnp.int32, (tq, kt), 1)
            key = jnp.where(c <= r, key, jnp.int32(INT_MIN))
        s_scr[:, pl.ds(k0, kt)] = key

    def score_body(j, carry):
        score_tile(j, False)
        return carry

    lax.fori_loop(0, i, score_body, 0)
    score_tile(i, True)

    for rc in range(tq // ROW_CHUNK):
        rows = slice(rc * ROW_CHUNK, (rc + 1) * ROW_CHUNK)

        def count_ge(t, rows=rows):
            tb = jnp.broadcast_to(t, (ROW_CHUNK, LANES))

            def add_tile(j, cnt):
                k0 = pl.multiple_of(j * kt, kt)
                keys = s_scr[rows, pl.ds(k0, kt)]
                for c in range(kt // LANES):
                    cnt = cnt + jnp.where(keys[:, c * LANES:(c + 1) * LANES] >= tb,
                                          jnp.int32(1), jnp.int32(0))
                return cnt

            def add_pair(jj, cnt):
                return add_tile(2 * jj + 1, add_tile(2 * jj, cnt))

            n_pairs = (i + 1) // 2
            cnt = lax.fori_loop(0, n_pairs, add_pair, jnp.zeros((ROW_CHUNK, LANES), jnp.int32))
            cnt = lax.fori_loop(2 * n_pairs, i + 1, add_tile, cnt)
            return jnp.sum(cnt.astype(F32), axis=-1, keepdims=True)

        thr_scr[rows] = jnp.broadcast_to(_topk_threshold(count_ge, ROW_CHUNK, topk), (ROW_CHUNK, LANES))

    m_scr[...] = jnp.full(m_scr.shape, NEG_BIG, F32)
    l_scr[...] = jnp.zeros(l_scr.shape, F32)
    acc_scr[...] = jnp.zeros(acc_scr.shape, F32)
    c_qk = scale * LOG2_E

    def qk(k0, h, rows):
        hs = slice(h * A_HEAD_DIM, (h + 1) * A_HEAD_DIM)
        return _dot_nt(qn_ref[0, rows, hs], kb_ref[0, pl.ds(k0, kt), hs])

    def mask_of(k0, rows):
        return pltpu.bitcast(s_scr[rows, pl.ds(k0, kt)], F32)

    def halves_max(x):
        out = x[:, :LANES]
        for c in range(1, kt // LANES):
            out = jnp.maximum(out, x[:, c * LANES:(c + 1) * LANES])
        return out

    def halves_sum(x):
        out = x[:, :LANES]
        for c in range(1, kt // LANES):
            out = out + x[:, c * LANES:(c + 1) * LANES]
        return out

    def max_tile(j, band_off):
        k0 = pl.multiple_of(j * kt, kt)
        thr_t = jnp.tile(thr_scr[...], (1, kt // LANES))
        mask = jnp.where(s_scr[:, pl.ds(k0, kt)] >= thr_t, 0.0, NEG_BIG)
        s_scr[:, pl.ds(k0, kt)] = pltpu.bitcast(mask, jnp.int32)
        for h in range(A_HEADS):
            for rc in range(tq // ROW_CHUNK):
                rows = slice(rc * ROW_CHUNK, (rc + 1) * ROW_CHUNK)
                if band_off is None:
                    top = halves_max(qk(k0, h, rows) + mask_of(k0, rows)) * c_qk + cfar_ref[h]
                else:
                    top = halves_max(qk(k0, h, rows) * c_qk + band_ref[h, rows, band_off:band_off + kt]
                                     + mask_of(k0, rows))
                m_scr[h, rows] = jnp.maximum(m_scr[h, rows], top)

    def sum_tile(j, band_off):
        k0 = pl.multiple_of(j * kt, kt)
        for h in range(A_HEADS):
            hs = slice(h * A_HEAD_DIM, (h + 1) * A_HEAD_DIM)
            for rc in range(tq // ROW_CHUNK):
                rows = slice(rc * ROW_CHUNK, (rc + 1) * ROW_CHUNK)
                if band_off is None:
                    shift = jnp.tile(cfar_ref[h] - m_scr[h, rows], (1, kt // LANES))
                else:
                    shift = (band_ref[h, rows, band_off:band_off + kt]
                             - jnp.tile(m_scr[h, rows], (1, kt // LANES)))
                p = jnp.exp2(qk(k0, h, rows) * c_qk + shift + mask_of(k0, rows))
                l_scr[h, rows] = l_scr[h, rows] + halves_sum(p)
                acc_scr[rows, hs] = acc_scr[rows, hs] + _dot(p.astype(BF16), vb_ref[0, pl.ds(k0, kt), hs])

    def over_tiles(tile_fn):
        def far_body(j, carry):
            tile_fn(j, None)
            return carry

        lax.fori_loop(0, jnp.maximum(i - 1, 0), far_body, 0)

        @pl.when(i >= 1)
        def _():
            tile_fn(i - 1, 0)

        tile_fn(i, kt)

    over_tiles(max_tile)
    for h in range(A_HEADS):
        m_scr[h] = jnp.broadcast_to(jnp.max(m_scr[h], axis=-1, keepdims=True), (tq, LANES))
    over_tiles(sum_tile)
    for h in range(A_HEADS):
        hs = slice(h * A_HEAD_DIM, (h + 1) * A_HEAD_DIM)
        o_ref[0, :, hs] = (acc_scr[:, hs] / jnp.sum(l_scr[h], axis=-1, keepdims=True)).astype(BF16)


def _rel_bucket(dist):
    n = jnp.maximum(dist, 0)
    max_exact = REL_BUCKETS // 2
    nf = jnp.maximum(n, 1).astype(F32)
    large = max_exact + (jnp.log(nf / max_exact) / math.log(REL_MAX_DIST / max_exact)
                         * (REL_BUCKETS - max_exact)).astype(jnp.int32)
    large = jnp.minimum(large, REL_BUCKETS - 1)
    return jnp.where(n < max_exact, n, large)


def _bias_band(rel_table, rows, kt):
    d = jnp.arange(rows)[:, None] + kt - jnp.arange(2 * kt)[None, :]
    hit = _rel_bucket(d)[None, :, :, None] == jnp.arange(REL_BUCKETS)
    return jnp.sum(jnp.where(hit, rel_table.T.astype(F32)[:, None, None, :], 0.0), axis=-1)


def _a_prompt(qn, iqm, z, kb, vb, ik2, band, cfar, batch, seq, tq):
    topk = min(TOPK_MAX, seq // 4)
    r3 = lambda a: a.reshape(batch, seq, a.shape[-1])
    qblk = lambda w: pl.BlockSpec((1, tq, w), lambda b, i: (b, i, 0))
    full = lambda w: pl.BlockSpec((1, seq, w), lambda b, i: (b, 0, 0), pipeline_mode=pl.Buffered(1))
    out = pl.pallas_call(
        functools.partial(_a_prompt_kernel, topk=topk),
        grid=(batch, seq // tq),
        in_specs=[pl.BlockSpec(memory_space=pltpu.SMEM),
                  qblk(A_WIDTH), qblk(IDX_HEADS * LANES),
                  pl.BlockSpec((1, tq, LANES), lambda b, i: (b, i, COL_MISC // LANES)),
                  full(A_WIDTH), full(A_WIDTH), full(LANES),
                  pl.BlockSpec((A_HEADS, tq, 2 * tq), lambda b, i: (0, 0, 0),
                               pipeline_mode=pl.Buffered(1))],
        out_specs=qblk(A_WIDTH),
        scratch_shapes=[pltpu.VMEM((tq, seq), jnp.int32),
                        pltpu.VMEM((IDX_HEADS, tq, LANES), F32),
                        pltpu.VMEM((tq, A_WIDTH), F32),
                        pltpu.VMEM((A_HEADS, tq, LANES), F32),
                        pltpu.VMEM((A_HEADS, tq, LANES), F32),
                        pltpu.VMEM((tq, LANES), jnp.int32)],
        out_shape=jax.ShapeDtypeStruct((batch, seq, A_WIDTH), BF16),
        compiler_params=_params(2),
        name="a_prompt",
    )(cfar, r3(qn), r3(iqm), r3(z), r3(kb), r3(vb), r3(ik2), band)
    return out.reshape(batch * seq, A_WIDTH)


def _a_sample_score_kernel(pt_ref, iq_ref, w_ref, mnew_ref, *rest, pages_per_step, n_pages, topk):
    page_refs = rest[:pages_per_step]
    keys_ref, thr_ref = rest[pages_per_step:]
    g = pl.program_id(1)
    rows = SAMPLE_ROWS
    iq = iq_ref[0]
    w = w_ref[0]

    def head_sum(x):
        return jnp.sum((jnp.maximum(x, 0.0) * w).reshape(IDX_HEADS, rows, PAGE_SIZE), axis=0)

    for p in range(pages_per_step):
        off = pl.multiple_of((g * pages_per_step + p) * PAGE_SIZE, PAGE_SIZE)
        keys_ref[0, :, pl.ds(off, PAGE_SIZE)] = _sortable_key(
            head_sum(_dot(iq, page_refs[p][...].astype(BF16))))

    @pl.when(g == pl.num_programs(1) - 1)
    def _():
        past = n_pages * PAGE_SIZE
        ik_new = mnew_ref[0][:, :IDX_DIM].astype(BF16)
        ik_new = jnp.concatenate([ik_new, jnp.zeros((PAGE_SIZE - rows, IDX_DIM), BF16)], axis=0)
        r = lax.broadcasted_iota(jnp.int32, (rows, PAGE_SIZE), 0)
        c = lax.broadcasted_iota(jnp.int32, (rows, PAGE_SIZE), 1)
        keys_ref[0, :, past:past + PAGE_SIZE] = jnp.where(
            c <= r, _sortable_key(head_sum(_dot_nt(iq, ik_new))), jnp.int32(INT_MIN))

        def count_ge(t):
            def body(j, cnt):
                off = pl.multiple_of(j * PAGE_SIZE, PAGE_SIZE)
                return cnt + jnp.where(keys_ref[0, :, pl.ds(off, PAGE_SIZE)] >= t,
                                       jnp.int32(1), jnp.int32(0))
            cnt = lax.fori_loop(0, n_pages + 1, body, jnp.zeros((rows, PAGE_SIZE), jnp.int32))
            return jnp.sum(cnt.astype(F32), axis=-1, keepdims=True)

        thr = _topk_threshold(count_ge, rows, topk)
        thr_ref[0] = jnp.broadcast_to(thr, (rows, LANES))


def _a_sample_scores(page_table, iq_hq, w_hq, z3, cache_kidx, layer, n_tok, pages_per_step):
    batch, n_pages = page_table.shape
    rows = SAMPLE_ROWS
    nk = (n_pages + 1) * PAGE_SIZE
    topk = min(TOPK_MAX, (n_pages * PAGE_SIZE + n_tok) // 4)

    def page_spec(p):
        return pl.BlockSpec((None, None, IDX_DIM, PAGE_SIZE),
                            lambda b, g, pt: (layer, pt[b, g * pages_per_step + p], 0, 0))

    return pl.pallas_call(
        functools.partial(_a_sample_score_kernel, pages_per_step=pages_per_step, n_pages=n_pages,
                          topk=topk),
        grid_spec=pltpu.PrefetchScalarGridSpec(
            num_scalar_prefetch=1,
            grid=(batch, n_pages // pages_per_step),
            in_specs=[pl.BlockSpec((1, IDX_HEADS * rows, IDX_DIM), lambda b, g, pt: (b, 0, 0)),
                      pl.BlockSpec((1, IDX_HEADS * rows, 1), lambda b, g, pt: (b, 0, 0)),
                      pl.BlockSpec((1, rows, LANES), lambda b, g, pt: (b, 0, COL_MISC // LANES))]
                     + [page_spec(p) for p in range(pages_per_step)],
            out_specs=[pl.BlockSpec((1, rows, nk), lambda b, g, pt: (b, 0, 0)),
                       pl.BlockSpec((1, rows, LANES), lambda b, g, pt: (b, 0, 0))]),
        out_shape=[jax.ShapeDtypeStruct((batch, rows, nk), jnp.int32),
                   jax.ShapeDtypeStruct((batch, rows, LANES), jnp.int32)],
        compiler_params=_params(2),
        name="a_sample_scores",
    )(page_table, iq_hq, w_hq, z3, *([cache_kidx.transpose(0, 1, 3, 2)] * pages_per_step))


def _a_sample_attn_kernel(pt_ref, qall_ref, keys_ref, thr_ref, knew_ref, vnew_ref, bias_far_ref, bias_last_ref,
                          bias_new_ref, expand_ref, *rest, pages_per_step, n_pages):
    k_refs = rest[:pages_per_step]
    v_refs = rest[pages_per_step:2 * pages_per_step]
    o_ref, acc_scr, m_scr, l_scr, lg_scr = rest[2 * pages_per_step:]
    g = pl.program_id(1)
    last_step = pl.num_programs(1) - 1
    rows = SAMPLE_ROWS
    hq = A_HEADS * rows
    cols = PAGE_SIZE * A_HEADS
    scale = A_HEAD_DIM ** -0.5
    thr = thr_ref[0][:, :1]
    qall = qall_ref[0]

    @pl.when(g == 0)
    def _():
        m_scr[...] = jnp.full(m_scr.shape, NEG_BIG, F32)
        l_scr[...] = jnp.zeros(l_scr.shape, F32)
        acc_scr[...] = jnp.zeros(acc_scr.shape, F32)

    def masked_logits(key_tile, expand, kf, bias):
        hit = jnp.where(key_tile >= thr, 1.0, 0.0).astype(BF16)
        drop = (_dot(hit, expand) - 1.0) * (-NEG_BIG)
        return _dot_nt(qall, kf) * scale + bias + jnp.concatenate([drop] * A_HEADS, axis=0)

    def lane_fold(x, op):
        out = x[:, :LANES]
        for c in range(1, x.shape[1] // LANES):
            out = op(out, x[:, c * LANES:(c + 1) * LANES])
        return out

    def update(n_tiles, width, v_of):
        top = lane_fold(lg_scr[:, :n_tiles * width], jnp.maximum)
        m_old = m_scr[...]
        m_new = jnp.maximum(m_old, jnp.max(top, axis=-1, keepdims=True))
        alpha = jnp.exp(m_old - m_new)
        part = jnp.zeros((hq, LANES), F32)
        acc = alpha * acc_scr[...]
        for t in range(n_tiles):
            p = jnp.exp(lg_scr[:, t * width:(t + 1) * width] - m_new)
            part = part + lane_fold(p, jnp.add)
            acc = acc + _dot(p.astype(BF16), v_of(t))
        l_scr[...] = alpha * l_scr[...] + jnp.sum(part, axis=-1, keepdims=True)
        acc_scr[...] = acc
        m_scr[...] = m_new

    for p in range(pages_per_step):
        page = g * pages_per_step + p
        off = pl.multiple_of(page * PAGE_SIZE, PAGE_SIZE)
        bias = bias_far_ref[...]
        if p == pages_per_step - 1:
            bias = jnp.where(g == last_step, bias_last_ref[...], bias)
        lg_scr[:, p * cols:(p + 1) * cols] = masked_logits(
            keys_ref[0, :, pl.ds(off, PAGE_SIZE)], expand_ref[...],
            k_refs[p][...].reshape(cols, A_HEAD_DIM).astype(BF16), bias)
    update(pages_per_step, cols, lambda t: v_refs[t][...].reshape(cols, A_HEAD_DIM).astype(BF16))

    @pl.when(g == last_step)
    def _():
        past = n_pages * PAGE_SIZE
        lg_scr[:, :hq] = masked_logits(keys_ref[0, :, past:past + PAGE_SIZE], expand_ref[:, :hq],
                                       knew_ref[0], bias_new_ref[...])
        update(1, hq, lambda t: vnew_ref[0])
        o_ref[0] = (acc_scr[...] / l_scr[...]).astype(BF16)


def _a_sample_attn(page_table, qall, keys, thr, knew, vnew, bias_far, bias_last, bias_new, expand,
                   cache_k, cache_v, layer, pages_per_step):
    batch, n_pages = page_table.shape
    rows = SAMPLE_ROWS
    hq = A_HEADS * rows
    nk = keys.shape[-1]

    def page_spec(p):
        return pl.BlockSpec((None, None, PAGE_SIZE, A_HEADS, A_HEAD_DIM),
                            lambda b, g, pt: (layer, pt[b, g * pages_per_step + p], 0, 0, 0))

    per_b = lambda r, w: pl.BlockSpec((1, r, w), lambda b, g, pt: (b, 0, 0))
    const = lambda r, w: pl.BlockSpec((r, w), lambda b, g, pt: (0, 0))
    return pl.pallas_call(
        functools.partial(_a_sample_attn_kernel, pages_per_step=pages_per_step, n_pages=n_pages),
        grid_spec=pltpu.PrefetchScalarGridSpec(
            num_scalar_prefetch=1,
            grid=(batch, n_pages // pages_per_step),
            in_specs=[per_b(hq, A_HEAD_DIM), per_b(rows, nk), per_b(rows, LANES),
                      per_b(hq, A_HEAD_DIM), per_b(hq, A_HEAD_DIM),
                      const(hq, PAGE_SIZE * A_HEADS), const(hq, PAGE_SIZE * A_HEADS), const(hq, hq),
                      const(PAGE_SIZE, PAGE_SIZE * A_HEADS)]
                     + [page_spec(p) for p in range(pages_per_step)] * 2,
            out_specs=per_b(hq, A_HEAD_DIM),
            scratch_shapes=[pltpu.VMEM((hq, A_HEAD_DIM), F32),
                            pltpu.VMEM((hq, 1), F32),
                            pltpu.VMEM((hq, 1), F32),
                            pltpu.VMEM((hq, pages_per_step * PAGE_SIZE * A_HEADS), F32)]),
        out_shape=jax.ShapeDtypeStruct((batch, hq, A_HEAD_DIM), BF16),
        compiler_params=_params(2),
        name="a_sample_attn",
    )(page_table, qall, keys, thr, knew, vnew, bias_far, bias_last, bias_new, expand,
      *([cache_k] * pages_per_step), *([cache_v] * pages_per_step))


def _pad_rows(x, rows):
    if x.shape[0] == rows:
        return x
    return jnp.concatenate([x, jnp.zeros((rows - x.shape[0], x.shape[1]), x.dtype)], axis=0)


def _split3(x):
    hi = x.astype(BF16)
    r1 = x - hi.astype(F32)
    mid = r1.astype(BF16)
    lo = (r1 - mid.astype(F32)).astype(BF16)
    return hi, mid, lo


def _gla_kernel(gq_ref, gk_ref, gv_ref, gg_ref, misc_ref, wa_ref, ba_ref, gn_ref, s0_ref,
                o_ref, sfin_ref, st_scr, oacc_scr, *, chunk, rows, n_valid):
    tt = gq_ref.shape[1]

    @pl.when(pl.program_id(1) == 0)
    def _():
        st_scr[...] = s0_ref[0]

    ga = _pad_rows(misc_ref[0][:, MISC_GA:MISC_GA + GLA_RANK], rows)
    x = _dot(ga.astype(BF16), wa_ref[...]) + ba_ref[...]
    la = (jnp.minimum(x, 0.0) - jnp.log1p(jnp.exp(-jnp.abs(x)))) * (1.0 / GLA_TAU)
    r = lax.broadcasted_iota(jnp.int32, (rows, rows), 0)
    c = lax.broadcasted_iota(jnp.int32, (rows, rows), 1)
    if n_valid < rows:
        rr = lax.broadcasted_iota(jnp.int32, la.shape, 0)
        la = jnp.where(rr < n_valid, la, 0.0)
    tri = jnp.where((c <= r) & (r // chunk == c // chunk), 1.0, 0.0).astype(BF16)
    hi, mid, lo = _split3(la)
    b = _dot(tri, hi) + _dot(tri, mid) + _dot(tri, lo)

    gq = _pad_rows(gq_ref[0], rows) * (GLA_DK ** -0.5)
    gk = _pad_rows(gk_ref[0], rows)
    if n_valid < rows:
        gk = jnp.where(rr < n_valid, gk, 0.0)
    gv = _pad_rows(gv_ref[0], rows)
    causal = (lax.broadcasted_iota(jnp.int32, (chunk, chunk), 1)
              <= lax.broadcasted_iota(jnp.int32, (chunk, chunk), 0))

    for ci in range(rows // chunk):
        rs = slice(ci * chunk, (ci + 1) * chunk)
        for h in range(GLA_HEADS):
            ks = slice(h * GLA_DK, (h + 1) * GLA_DK)
            vs = slice(h * GLA_DV, (h + 1) * GLA_DV)
            bc = b[rs, ks]
            bend = bc[chunk - 1:chunk, :]
            bmid = bc[chunk // 2 - 1:chunk // 2, :]
            qt = (gq[rs, ks] * jnp.exp(bc)).astype(BF16)
            qm = (gq[rs, ks] * jnp.exp(bc - bmid)).astype(BF16)
            km = (gk[rs, ks] * jnp.exp(bmid - bc)).astype(BF16)
            kd = (gk[rs, ks] * jnp.exp(bend - bc)).astype(BF16)
            vh = gv[rs, vs].astype(BF16)
            st = st_scr[h]
            sc = jnp.where(causal, _dot_nt(qm, km), 0.0)
            oacc_scr[rs, vs] = _dot(sc.astype(BF16), vh) + _dot_nt(qt, st.astype(BF16))
            st_scr[h] = st * jnp.exp(bend) + _dot_tn(vh, kd)

    gg = gg_ref[0]
    for h in range(GLA_HEADS):
        vs = slice(h * GLA_DV, (h + 1) * GLA_DV)
        oh = oacc_scr[0:tt, vs]
        ms = jnp.mean(oh * oh, axis=-1, keepdims=True)
        g = gg[:, vs]
        o_ref[0, :, vs] = (oh * lax.rsqrt(ms + EPS) * gn_ref[...] * (g * jax.nn.sigmoid(g))).astype(BF16)

    @pl.when(pl.program_id(1) == pl.num_programs(1) - 1)
    def _():
        sfin_ref[0] = st_scr[...]


def _gla(z3, wa, ba, gn, s0t, tt, chunk, rows, n_valid):
    batch, seq, _ = z3.shape
    q_blk = COL_GQ // 256
    v_blk = COL_GV // GLA_WIDTH
    tok = lambda w, cblk: pl.BlockSpec((1, tt, w), lambda b, t: (b, t, cblk))
    const = lambda shape: pl.BlockSpec(shape, lambda b, t: (0,) * len(shape))
    st_spec = pl.BlockSpec((1, GLA_HEADS, GLA_DV, GLA_DK), lambda b, t: (b, 0, 0, 0))
    return pl.pallas_call(
        functools.partial(_gla_kernel, chunk=chunk, rows=rows, n_valid=n_valid),
        grid=(batch, seq // tt),
        in_specs=[tok(256, q_blk), tok(256, q_blk + 1), tok(GLA_WIDTH, v_blk), tok(GLA_WIDTH, v_blk + 1),
                  tok(LANES, COL_MISC // LANES),
                  const((GLA_RANK, GLA_HEADS * GLA_DK)), const((1, GLA_HEADS * GLA_DK)),
                  const((1, GLA_DV)), st_spec],
        out_specs=[pl.BlockSpec((1, tt, GLA_WIDTH), lambda b, t: (b, t, 0)), st_spec],
        out_shape=[jax.ShapeDtypeStruct((batch, seq, GLA_WIDTH), BF16),
                   jax.ShapeDtypeStruct((batch, GLA_HEADS, GLA_DV, GLA_DK), F32)],
        scratch_shapes=[pltpu.VMEM((GLA_HEADS, GLA_DV, GLA_DK), F32),
                        pltpu.VMEM((rows, GLA_WIDTH), F32)],
        compiler_params=_params(2),
        name="gla",
    )(z3, z3, z3, z3, z3, wa, ba, gn, s0t)


def _ret_kernel(rq_ref, rk_ref, rv_ref, rg_ref, cos_ref, sin_ref, gn_ref, s0_ref,
                o_ref, sfin_ref, st_scr, oacc_scr, *, rows, n_valid, log_gamma):
    tt = rq_ref.shape[1]
    width = RET_HEADS * RET_DK

    @pl.when(pl.program_id(1) == 0)
    def _():
        st_scr[...] = s0_ref[0]

    lane = lax.broadcasted_iota(jnp.int32, (rows, width), 1)
    first_half = (lane % RET_DK) < (RET_DK // 2)
    cos = _pad_rows(cos_ref[...], rows)
    sin = _pad_rows(sin_ref[...], rows)

    def rope(x):
        rot = jnp.where(first_half, pltpu.roll(x, width - RET_DK // 2, axis=1),
                        pltpu.roll(x, RET_DK // 2, axis=1))
        return x * cos + rot * sin

    q = rope(_pad_rows(rq_ref[0], rows))
    k = rope(_pad_rows(rk_ref[0], rows)) * (RET_DK ** -0.5)
    v = _pad_rows(rv_ref[0], rows)
    rr = lax.broadcasted_iota(jnp.int32, (rows, RET_DK), 0)
    if n_valid < rows:
        k = jnp.where(lax.broadcasted_iota(jnp.int32, k.shape, 0) < n_valid, k, 0.0)
    t_i = lax.broadcasted_iota(jnp.int32, (rows, rows), 0)
    s_i = lax.broadcasted_iota(jnp.int32, (rows, rows), 1)
    dist = (t_i - s_i).astype(F32)
    pos1 = (rr + 1).astype(F32)
    rem = (n_valid - 1 - rr).astype(F32)

    for h in range(RET_HEADS):
        ks = slice(h * RET_DK, (h + 1) * RET_DK)
        vs = slice(h * RET_DV, (h + 1) * RET_DV)
        lg = log_gamma[h]
        decay = jnp.where(t_i >= s_i, jnp.exp(dist * lg), 0.0)
        qh = q[:, ks]
        kh = k[:, ks]
        vh = v[:, vs].astype(BF16)
        st = st_scr[h]
        sc = _dot_nt(qh.astype(BF16), kh.astype(BF16)) * decay
        q_in = (qh * jnp.exp(pos1 * lg)).astype(BF16)
        oacc_scr[:, vs] = _dot(sc.astype(BF16), vh) + _dot_nt(q_in, st.astype(BF16))
        kd = (kh * jnp.exp(rem * lg)).astype(BF16)
        st_scr[h] = st * math.exp(n_valid * lg) + _dot_tn(vh, kd)

    rg = rg_ref[0]
    for h in range(RET_HEADS):
        vs = slice(h * RET_DV, (h + 1) * RET_DV)
        oh = oacc_scr[0:tt, vs]
        oc = oh - jnp.mean(oh, axis=-1, keepdims=True)
        var = jnp.mean(oc * oc, axis=-1, keepdims=True)
        g = rg[:, vs]
        o_ref[0, :, vs] = (oc * lax.rsqrt(var + EPS) * gn_ref[...] * (g * jax.nn.sigmoid(g))).astype(BF16)

    @pl.when(pl.program_id(1) == pl.num_programs(1) - 1)
    def _():
        sfin_ref[0] = st_scr[...]


def _ret(z3, cos, sin, gn, s0t, tt, rows, n_valid):
    batch, seq, _ = z3.shape
    q_blk = COL_GQ // 256 + 2
    v_blk = COL_GV // RET_WIDTH + 2
    width = RET_HEADS * RET_DK
    log_gamma = tuple(float(np.log1p(-np.exp2(np.float32(-5.0 - h)), dtype=np.float32))
                      for h in range(RET_HEADS))
    tok = lambda w, cblk: pl.BlockSpec((1, tt, w), lambda b, t: (b, t, cblk))
    st_spec = pl.BlockSpec((1, RET_HEADS, RET_DV, RET_DK), lambda b, t: (b, 0, 0, 0))
    return pl.pallas_call(
        functools.partial(_ret_kernel, rows=rows, n_valid=n_valid, log_gamma=log_gamma),
        grid=(batch, seq // tt),
        in_specs=[tok(256, q_blk), tok(256, q_blk + 1), tok(RET_WIDTH, v_blk), tok(RET_WIDTH, v_blk + 1),
                  pl.BlockSpec((tt, width), lambda b, t: (t, 0)),
                  pl.BlockSpec((tt, width), lambda b, t: (t, 0)),
                  pl.BlockSpec((1, RET_DV), lambda b, t: (0, 0)), st_spec],
        out_specs=[pl.BlockSpec((1, tt, RET_WIDTH), lambda b, t: (b, t, 0)), st_spec],
        out_shape=[jax.ShapeDtypeStruct((batch, seq, RET_WIDTH), BF16),
                   jax.ShapeDtypeStruct((batch, RET_HEADS, RET_DV, RET_DK), F32)],
        scratch_shapes=[pltpu.VMEM((RET_HEADS, RET_DV, RET_DK), F32),
                        pltpu.VMEM((rows, RET_WIDTH), F32)],
        compiler_params=_params(2),
        name="ret",
    )(z3, z3, z3, z3, cos, sin, gn, s0t)


def _rope_tables(pos):
    half = RET_DK // 2
    freqs = ROPE_BASE ** (-jnp.arange(half, dtype=F32) / half)
    ang = pos.astype(F32)[:, None] * freqs[None, :]
    cos = jnp.cos(ang)
    sin = jnp.sin(ang)
    cos_t = jnp.tile(jnp.concatenate([cos, cos], axis=-1), (1, RET_HEADS))
    sin_t = jnp.tile(jnp.concatenate([-sin, sin], axis=-1), (1, RET_HEADS))
    return cos_t, sin_t


def _permuted_segments():
    names = ("aq", "ak", "av", "iq", "ik", "iw", "gq", "gk", "gv", "ga", "gg", "rq", "rk", "rv", "rg", "gates")
    src, acc = {}, 0
    for name, size in zip(names, IN_SIZES):
        src[name] = (acc, size)
        acc += size
    order = ("gates", "aq", "ak", "av", "iq", "gv", "gg", "rv", "rg", "gq", "gk", "rq", "rk", "ik", "iw", "ga")
    out, dst = [], 0
    for name in order:
        out.append((src[name][0], dst, src[name][1]))
        dst += src[name][1]
    return out, dst


def _permute_w_in_kernel(w_ref, o_ref):
    segments, used = _permuted_segments()
    for src, dst, size in segments:
        o_ref[dst:dst + size, :] = w_ref[src:src + size, :].astype(BF16)
    o_ref[used:, :] = jnp.zeros((IN_PADDED - used, o_ref.shape[1]), BF16)


def _permute_w_in(w_in_t, layer):
    _, n, d = w_in_t.shape
    tc = LANES
    return pl.pallas_call(
        _permute_w_in_kernel,
        grid=(d // tc,),
        in_specs=[pl.BlockSpec((None, n, tc), lambda j: (layer, 0, j))],
        out_specs=pl.BlockSpec((IN_PADDED, tc), lambda j: (0, j)),
        out_shape=jax.ShapeDtypeStruct((IN_PADDED, d), BF16),
        compiler_params=_params(1),
        name="permute_w_in",
    )(w_in_t)


def _dense_tail(x, z, oa, ob, oc, w):
    m = x.shape[0]
    layer = w["layer"]
    merged = _merge(oa, ob, oc, z, w["w_branch"], layer, min(m, 256), D_MODEL)
    x = _matmul_residual(merged, w["w_out"], layer, x, min(m, 512), D_MODEL, "out_proj")
    act = _norm_swiglu(x, w["norm_ffn"], w["w_ffn_in"], layer, min(m, 1024), 512)
    return _matmul_residual(act, w["w_ffn_out"], layer, x, min(m, 1024), 512, "ffn_out")


def _prompt_layer(x, w, batch, seq, band, cfar, cos, sin):
    tm = 512
    z = _norm_matmul(x, w["norm_mix"], w["w_in"], tm, 1920)
    qn, kf, kb, vf, vb, iqm, ik2 = _prep_a(z, w["a_q_norm"], w["a_k_norm"], tm)
    oa = _a_prompt(qn, iqm, z, kb, vb, ik2, band * LOG2_E, cfar * LOG2_E, batch, seq, 256)
    z3 = z.reshape(batch, seq, IN_PADDED)
    zero_state = jnp.zeros((batch, GLA_HEADS, GLA_DV, GLA_DK), F32)
    ob, s_gla = _gla(z3, w["gla_wa"], w["gla_ba"], w["gla_norm"], zero_state, 256, CHUNK, 256, 256)
    oc, s_ret = _ret(z3, cos, sin, w["ret_norm"], zero_state, 256, 256, 256)
    x = _dense_tail(x, z, oa, ob.reshape(-1, GLA_WIDTH), oc.reshape(-1, RET_WIDTH), w)
    kidx = z3[:, :, COL_MISC:COL_MISC + IDX_DIM]
    new = (kf.reshape(batch, seq, A_HEADS, A_HEAD_DIM), vf.reshape(batch, seq, A_HEADS, A_HEAD_DIM),
           kidx, s_gla.transpose(0, 1, 3, 2), s_ret.transpose(0, 1, 3, 2))
    return x, new


def _sample_layer(x, w, layer, batch, n_tok, page_table, cache_k, cache_v, cache_kidx,
                  s_gla0, s_ret0, bias_s, expand, cos, sin):
    rows = SAMPLE_ROWS
    hq = A_HEADS * rows
    tm = batch * rows
    z = _norm_matmul(x, w["norm_mix"], w["w_in"], tm, 1920)
    qn, kf, kb, vf, vb, _, _ = _prep_a(z, w["a_q_norm"], w["a_k_norm"], tm)
    z3 = z.reshape(batch, rows, IN_PADDED)
    iq = z3[:, :, COL_AQ + 3 * A_WIDTH:COL_AQ + 4 * A_WIDTH].reshape(batch, rows, IDX_HEADS, IDX_DIM)
    iq_hq = iq.transpose(0, 2, 1, 3).reshape(batch, IDX_HEADS * rows, IDX_DIM).astype(BF16)
    iw = z3[:, :, COL_MISC + MISC_IW:COL_MISC + MISC_IW + IDX_HEADS] * ((IDX_HEADS * IDX_DIM) ** -0.5)
    w_hq = iw.transpose(0, 2, 1).reshape(batch, IDX_HEADS * rows, 1)
    keys, thr = _a_sample_scores(page_table, iq_hq, w_hq, z3, cache_kidx, layer, n_tok, 32)
    qall = qn.reshape(batch, rows, A_HEADS, A_HEAD_DIM).transpose(0, 2, 1, 3).reshape(batch, hq, A_HEAD_DIM)
    oa = _a_sample_attn(page_table, qall, keys, thr, kb.reshape(batch, hq, A_HEAD_DIM),
                        vb.reshape(batch, hq, A_HEAD_DIM), *bias_s, expand, cache_k, cache_v, layer, 8)
    oa = oa.reshape(batch, A_HEADS, rows, A_HEAD_DIM).transpose(0, 2, 1, 3).reshape(tm, A_WIDTH)
    ob, s_gla = _gla(z3, w["gla_wa"], w["gla_ba"], w["gla_norm"], s_gla0.transpose(0, 1, 3, 2),
                     rows, LANES, LANES, n_tok)
    oc, s_ret = _ret(z3, cos, sin, w["ret_norm"], s_ret0.transpose(0, 1, 3, 2), rows, LANES, n_tok)
    x = _dense_tail(x, z, oa, ob.reshape(tm, GLA_WIDTH), oc.reshape(tm, RET_WIDTH), w)
    kidx = z3[:, :n_tok, COL_MISC:COL_MISC + IDX_DIM]
    new = (kf.reshape(batch, rows, A_HEADS, A_HEAD_DIM)[:, :n_tok],
           vf.reshape(batch, rows, A_HEADS, A_HEAD_DIM)[:, :n_tok],
           kidx, s_gla.transpose(0, 1, 3, 2), s_ret.transpose(0, 1, 3, 2))
    return x, new


def _sample_bias(rel_table):
    rows = SAMPLE_ROWS
    hq = A_HEADS * rows
    band = _bias_band(rel_table, rows, PAGE_SIZE)
    cfar_rows = jnp.repeat(rel_table[REL_BUCKETS - 1].astype(F32), rows)[:, None]
    band_last = jnp.repeat(band[:, :, :PAGE_SIZE].reshape(hq, PAGE_SIZE), A_HEADS, axis=1)
    band_new = jnp.repeat(band[:, :, PAGE_SIZE:PAGE_SIZE + rows].reshape(hq, rows), A_HEADS, axis=1)
    cols = PAGE_SIZE * A_HEADS
    other_head = (jnp.arange(hq)[:, None] // rows) != (jnp.arange(cols)[None, :] % A_HEADS)
    head_mask = jnp.where(other_head, NEG_BIG, 0.0).astype(F32)
    return cfar_rows + head_mask, band_last + head_mask, band_new + head_mask[:, :hq]


def kernel(x_prompt, x_sample, cache_k, cache_v, cache_kidx, state_gla, state_ret, page_table, rel_table, w_in, a_q_norm, a_k_norm, gla_wa, gla_ba, gla_norm, ret_norm, w_branch, w_out, norm_mix, norm_ffn, w_ffn_in, w_ffn_out):
    depth = w_in.shape[0]
    bp, tp, d = x_prompt.shape
    bs, ts, _ = x_sample.shape
    past = page_table.shape[1] * PAGE_SIZE
    tq = 256
    row = lambda a: a.reshape(1, -1).astype(F32)

    cfar = rel_table[REL_BUCKETS - 1].astype(F32)
    band_p = _bias_band(rel_table, tq, tq)
    bias_s = _sample_bias(rel_table)
    expand = jnp.asarray(np.kron(np.eye(PAGE_SIZE), np.ones((1, A_HEADS))), dtype=BF16)
    cos_p, sin_p = _rope_tables(jnp.arange(tp))
    cos_s, sin_s = _rope_tables(past + jnp.arange(SAMPLE_ROWS))

    w_in_t = jnp.swapaxes(w_in, 1, 2)
    xp = x_prompt.reshape(bp * tp, d)
    xs = jnp.pad(x_sample, ((0, 0), (0, SAMPLE_ROWS - ts), (0, 0))).reshape(bs * SAMPLE_ROWS, d)
    rows_p, rows_s = [], []
    for l in range(depth):
        w = dict(layer=l, w_in=_permute_w_in(w_in_t, l), a_q_norm=row(a_q_norm[l]), a_k_norm=row(a_k_norm[l]),
                 gla_wa=gla_wa[l].astype(BF16), gla_ba=row(gla_ba[l]), gla_norm=row(gla_norm[l]),
                 ret_norm=row(ret_norm[l]), w_branch=w_branch, w_out=w_out,
                 norm_mix=row(norm_mix[l]), norm_ffn=row(norm_ffn[l]), w_ffn_in=w_ffn_in,
                 w_ffn_out=w_ffn_out)
        xp, new_p = _prompt_layer(xp, w, bp, tp, band_p, cfar, cos_p, sin_p)
        xs, new_s = _sample_layer(xs, w, l, bs, ts, page_table, cache_k, cache_v, cache_kidx,
                                  state_gla[l], state_ret[l], bias_s, expand, cos_s, sin_s)
        rows_p.append(new_p)
        rows_s.append(new_s)
    outs_p = [jnp.stack(r) for r in zip(*rows_p)]
    outs_s = [jnp.stack(r) for r in zip(*rows_s)]
    y_p = xp.reshape(bp, tp, d)
    y_s = xs.reshape(bs, SAMPLE_ROWS, d)[:, :ts]
    return (y_p, y_s, *outs_p, *outs_s)
```

```python
import functools
import math

import numpy as np
import jax
import jax.numpy as jnp
from jax import lax
from jax.experimental import pallas as pl
from jax.experimental.pallas import tpu as pltpu

D_MODEL = 2048
PAGE_SIZE = 128
A_HEADS = 8
A_HEAD_DIM = 128
A_WIDTH = A_HEADS * A_HEAD_DIM
IDX_HEADS = 16
IDX_DIM = 64
TOPK_MAX = 256
REL_BUCKETS = 32
REL_MAX_DIST = 128
GLA_HEADS = 4
GLA_DK = 64
GLA_DV = 128
GLA_WIDTH = GLA_HEADS * GLA_DV
GLA_RANK = 16
GLA_TAU = 16.0
RET_HEADS = 4
RET_DK = 64
RET_DV = 128
RET_WIDTH = RET_HEADS * RET_DV
ROPE_BASE = 10000.0
CHUNK = 64
MIX_WIDTH = A_WIDTH + GLA_WIDTH + RET_WIDTH
D_FF = -(-8 * D_MODEL // (3 * 256)) * 256
EPS = 1e-6
IN_SIZES = (A_WIDTH, A_WIDTH, A_WIDTH, IDX_HEADS * IDX_DIM, IDX_DIM, IDX_HEADS,
            GLA_HEADS * GLA_DK, GLA_HEADS * GLA_DK, GLA_WIDTH, GLA_RANK, GLA_WIDTH,
            RET_HEADS * RET_DK, RET_HEADS * RET_DK, RET_WIDTH, RET_WIDTH, 3 * D_MODEL)

COL_GATES = 0
COL_AQ = 3 * D_MODEL
COL_GV = COL_AQ + 4 * A_WIDTH
COL_GQ = COL_GV + 4 * GLA_WIDTH
COL_MISC = COL_GQ + 4 * 256
IN_PADDED = COL_MISC + 128
IN_PROJ_COLS = IN_PADDED // 5
MISC_IW = IDX_DIM
MISC_GA = IDX_DIM + IDX_HEADS

LANES = 128
SUBLANES = 8
SAMPLE_ROWS = 16
VMEM_LIMIT = 56 * 1024 * 1024
INT_MIN = -2147483648
NEG_BIG = -1e30
LOG2_E = math.log2(math.e)
GLA_MAX_FACTORED_EXPONENT = 60.0
ROW_CHUNK = 128
ATTN_ROWS = 256

BF16 = jnp.bfloat16
F32 = jnp.float32
NT_DIMS = (((1,), (1,)), ((), ()))
TN_DIMS = (((0,), (0,)), ((), ()))


def _params(n_axes):
    return pltpu.CompilerParams(dimension_semantics=("arbitrary",) * n_axes,
                                vmem_limit_bytes=VMEM_LIMIT)


def _resident(constant_index):
    return pl.Buffered(1) if constant_index else None


def _dot(a, b):
    return jnp.dot(a, b, preferred_element_type=F32)


def _dot_nt(a, b):
    return lax.dot_general(a, b, NT_DIMS, preferred_element_type=F32)


def _dot_tn(a, b):
    return lax.dot_general(a, b, TN_DIMS, preferred_element_type=F32)


def _norm_matmul_kernel(x_ref, g_ref, w_ref, o_ref, hb_ref):
    @pl.when(pl.program_id(1) == 0)
    def _():
        x = x_ref[...]
        ms = jnp.mean(x * x, axis=-1, keepdims=True)
        hb_ref[...] = (x * lax.rsqrt(ms + EPS) * g_ref[...]).astype(BF16)

    o_ref[...] = _dot_nt(hb_ref[...], w_ref[...])


def _norm_matmul(x, g, w, tm, tn):
    m, d = x.shape
    n = w.shape[0]
    return pl.pallas_call(
        _norm_matmul_kernel,
        grid=(m // tm, n // tn),
        in_specs=[pl.BlockSpec((tm, d), lambda i, j: (i, 0)),
                  pl.BlockSpec((1, d), lambda i, j: (0, 0)),
                  pl.BlockSpec((tn, d), lambda i, j: (j, 0))],
        out_specs=pl.BlockSpec((tm, tn), lambda i, j: (i, j)),
        out_shape=jax.ShapeDtypeStruct((m, n), F32),
        scratch_shapes=[pltpu.VMEM((tm, d), BF16)],
        compiler_params=_params(2),
        name="in_proj",
    )(x, g, w)


def _norm_swiglu_kernel(x_ref, g_ref, wg_ref, wu_ref, o_ref, hb_ref):
    @pl.when(pl.program_id(1) == 0)
    def _():
        x = x_ref[...]
        ms = jnp.mean(x * x, axis=-1, keepdims=True)
        hb_ref[...] = (x * lax.rsqrt(ms + EPS) * g_ref[...]).astype(BF16)

    h = hb_ref[...]
    gate = _dot(h, wg_ref[...].astype(BF16))
    up = _dot(h, wu_ref[...].astype(BF16))
    o_ref[...] = (gate * jax.nn.sigmoid(gate) * up).astype(BF16)


def _norm_swiglu(x, g, w, layer, tm, tf):
    m, d = x.shape
    f = w.shape[2] // 2
    nf = f // tf
    return pl.pallas_call(
        _norm_swiglu_kernel,
        grid=(m // tm, nf),
        in_specs=[pl.BlockSpec((tm, d), lambda i, j: (i, 0)),
                  pl.BlockSpec((1, d), lambda i, j: (0, 0)),
                  pl.BlockSpec((None, d, tf), lambda i, j: (layer, 0, j)),
                  pl.BlockSpec((None, d, tf), lambda i, j: (layer, 0, j + nf))],
        out_specs=pl.BlockSpec((tm, tf), lambda i, j: (i, j)),
        out_shape=jax.ShapeDtypeStruct((m, f), BF16),
        scratch_shapes=[pltpu.VMEM((tm, d), BF16)],
        compiler_params=_params(2),
        name="swiglu",
    )(x, g, w, w)


def _matmul_residual_kernel(a_ref, w_ref, r_ref, o_ref, wb_ref):
    @pl.when(pl.program_id(1) == 0)
    def _():
        wb_ref[...] = w_ref[...].astype(BF16)

    o_ref[...] = r_ref[...] + _dot(a_ref[...], wb_ref[...])


def _matmul_residual(a, w, layer, r, tm, tn, name):
    m, k = a.shape
    n = w.shape[2]
    return pl.pallas_call(
        _matmul_residual_kernel,
        grid=(n // tn, m // tm),
        in_specs=[pl.BlockSpec((tm, k), lambda j, i: (i, 0)),
                  pl.BlockSpec((None, k, tn), lambda j, i: (layer, 0, j), pipeline_mode=pl.Buffered(1)),
                  pl.BlockSpec((tm, tn), lambda j, i: (i, j))],
        out_specs=pl.BlockSpec((tm, tn), lambda j, i: (i, j)),
        out_shape=jax.ShapeDtypeStruct((m, n), F32),
        scratch_shapes=[pltpu.VMEM((k, tn), BF16)],
        compiler_params=_params(2),
        name=name,
    )(a, w, r)


def _merge_kernel(oa_ref, ob_ref, oc_ref, ga_ref, gb_ref, gc_ref, wa_ref, wb_ref, wc_ref, o_ref):
    ya = _dot(oa_ref[...], wa_ref[...].astype(BF16))
    yb = _dot(ob_ref[...], wb_ref[...].astype(BF16))
    yc = _dot(oc_ref[...], wc_ref[...].astype(BF16))
    merged = (jax.nn.sigmoid(ga_ref[...]) * ya + jax.nn.sigmoid(gb_ref[...]) * yb
              + jax.nn.sigmoid(gc_ref[...]) * yc)
    o_ref[...] = merged.astype(BF16)


def _merge(oa, ob, oc, z, w_branch, layer, tm, tn):
    m = oa.shape[0]
    n = D_MODEL
    nb = n // tn
    a_blocks = A_WIDTH // GLA_WIDTH
    return pl.pallas_call(
        _merge_kernel,
        grid=(m // tm, nb),
        in_specs=[pl.BlockSpec((tm, A_WIDTH), lambda i, j: (i, 0)),
                  pl.BlockSpec((tm, GLA_WIDTH), lambda i, j: (i, 0)),
                  pl.BlockSpec((tm, RET_WIDTH), lambda i, j: (i, 0)),
                  pl.BlockSpec((tm, tn), lambda i, j: (i, j)),
                  pl.BlockSpec((tm, tn), lambda i, j: (i, j + nb)),
                  pl.BlockSpec((tm, tn), lambda i, j: (i, j + 2 * nb)),
                  pl.BlockSpec((None, A_WIDTH, tn), lambda i, j: (layer, 0, j),
                               pipeline_mode=_resident(n == tn)),
                  pl.BlockSpec((None, GLA_WIDTH, tn), lambda i, j: (layer, a_blocks, j),
                               pipeline_mode=_resident(n == tn)),
                  pl.BlockSpec((None, RET_WIDTH, tn), lambda i, j: (layer, a_blocks + 1, j),
                               pipeline_mode=_resident(n == tn))],
        out_specs=pl.BlockSpec((tm, tn), lambda i, j: (i, j)),
        out_shape=jax.ShapeDtypeStruct((m, n), BF16),
        compiler_params=_params(2),
        name="merge",
    )(oa, ob, oc, z, z, z, w_branch, w_branch, w_branch)


def _head_rmsnorm(x, g):
    outs = []
    for h in range(A_HEADS):
        xh = x[:, h * A_HEAD_DIM:(h + 1) * A_HEAD_DIM]
        ms = jnp.mean(xh * xh, axis=-1, keepdims=True)
        outs.append(xh * lax.rsqrt(ms + EPS) * g)
    return jnp.concatenate(outs, axis=-1)


def _prep_a_kernel(aq_ref, ak_ref, av_ref, iq_ref, misc_ref, gq_ref, gk_ref,
                   qn_ref, kf_ref, kb_ref, vf_ref, vb_ref, iqm_ref, ik2_ref):
    qn_ref[...] = _head_rmsnorm(aq_ref[...], gq_ref[...]).astype(BF16)
    kn = _head_rmsnorm(ak_ref[...], gk_ref[...])
    kf_ref[...] = kn
    kb_ref[...] = kn.astype(BF16)
    v = av_ref[...]
    vf_ref[...] = v
    vb_ref[...] = v.astype(BF16)
    iq = iq_ref[...].astype(BF16)
    lane = lax.broadcasted_iota(jnp.int32, (iq.shape[0], LANES), 1)
    zero = jnp.zeros((iq.shape[0], LANES), BF16)
    for h in range(IDX_HEADS):
        pair = iq[:, (h // 2) * LANES:(h // 2 + 1) * LANES]
        keep = (lane < IDX_DIM) if h % 2 == 0 else (lane >= IDX_DIM)
        iqm_ref[:, h * LANES:(h + 1) * LANES] = jnp.where(keep, pair, zero)
    ik = misc_ref[...][:, :IDX_DIM].astype(BF16)
    ik2_ref[...] = jnp.concatenate([ik, ik], axis=-1)


def _prep_a(z, gq, gk, tm):
    m = z.shape[0]
    blk = COL_AQ // A_WIDTH
    wide = lambda c: pl.BlockSpec((tm, A_WIDTH), lambda i: (i, c))
    row = pl.BlockSpec((tm, A_WIDTH), lambda i: (i, 0))
    return pl.pallas_call(
        _prep_a_kernel,
        grid=(m // tm,),
        in_specs=[wide(blk), wide(blk + 1), wide(blk + 2), wide(blk + 3),
                  pl.BlockSpec((tm, LANES), lambda i: (i, COL_MISC // LANES)),
                  pl.BlockSpec((1, A_HEAD_DIM), lambda i: (0, 0)),
                  pl.BlockSpec((1, A_HEAD_DIM), lambda i: (0, 0))],
        out_specs=[row, row, row, row, row,
                   pl.BlockSpec((tm, IDX_HEADS * LANES), lambda i: (i, 0)),
                   pl.BlockSpec((tm, LANES), lambda i: (i, 0))],
        out_shape=[jax.ShapeDtypeStruct((m, A_WIDTH), BF16),
                   jax.ShapeDtypeStruct((m, A_WIDTH), F32),
                   jax.ShapeDtypeStruct((m, A_WIDTH), BF16),
                   jax.ShapeDtypeStruct((m, A_WIDTH), F32),
                   jax.ShapeDtypeStruct((m, A_WIDTH), BF16),
                   jax.ShapeDtypeStruct((m, IDX_HEADS * LANES), BF16),
                   jax.ShapeDtypeStruct((m, LANES), BF16)],
        compiler_params=_params(1),
        name="prep_a",
    )(z, z, z, z, z, gq, gk)


def _sortable_key(score):
    bits = pltpu.bitcast(score, jnp.int32)
    return bits ^ ((bits >> 31) & jnp.int32(0x7FFFFFFF))


def _topk_threshold(count_ge, rows, topk):
    def body(it, ans):
        cand = ans + lax.shift_left(jnp.int32(1), jnp.int32(31) - it)
        return jnp.where(count_ge(cand) >= float(topk), cand, ans)

    ans = lax.fori_loop(0, 32, body, jnp.full((rows, 1), INT_MIN, jnp.int32))
    return jnp.maximum(ans, jnp.int32(INT_MIN + 1))


def _a_prompt_kernel(cfar_ref, qn_ref, iqm_ref, mq_ref, kb_ref, vb_ref, ik2_ref, band_ref,
                     o_ref, s_scr, w_scr, acc_scr, m_scr, l_scr, thr_scr, *, topk):
    i = pl.program_id(1)
    tq = qn_ref.shape[1]
    kt = tq
    scale = A_HEAD_DIM ** -0.5
    iw_scale = (IDX_HEADS * IDX_DIM) ** -0.5

    iw = mq_ref[0][:, MISC_IW:MISC_IW + IDX_HEADS] * iw_scale
    for h in range(IDX_HEADS):
        w_scr[h] = jnp.broadcast_to(iw[:, h:h + 1], (tq, LANES))

    def score_tile(j, diagonal):
        k0 = pl.multiple_of(j * kt, kt)
        ik = ik2_ref[0, pl.ds(k0, kt), :]
        acc = jnp.zeros((tq, kt), F32)
        for h in range(IDX_HEADS):
            x = _dot_nt(iqm_ref[0, :, h * LANES:(h + 1) * LANES], ik)
            acc = acc + jnp.maximum(x, 0.0) * jnp.tile(w_scr[h], (1, kt // LANES))
        key = _sortable_key(acc)
        if diagonal:
            r = lax.broadcasted_iota(jnp.int32, (tq, kt), 0)
            c = lax.broadcasted_iota(jnp.int32, (tq, kt), 1)
            key = jnp.where(c <= r, key, jnp.int32(INT_MIN))
        s_scr[:, pl.ds(k0, kt)] = key

    def score_body(j, carry):
        score_tile(j, False)
        return carry

    lax.fori_loop(0, i, score_body, 0)
    score_tile(i, True)

    for rc in range(tq // ROW_CHUNK):
        rows = slice(rc * ROW_CHUNK, (rc + 1) * ROW_CHUNK)

        def count_ge(t, rows=rows):
            tb = jnp.broadcast_to(t, (ROW_CHUNK, LANES))

            def add_tile(j, cnt):
                k0 = pl.multiple_of(j * kt, kt)
                keys = s_scr[rows, pl.ds(k0, kt)]
                for c in range(kt // LANES):
                    cnt = cnt + jnp.where(keys[:, c * LANES:(c + 1) * LANES] >= tb,
                                          jnp.int32(1), jnp.int32(0))
                return cnt

            def add_pair(jj, cnt):
                return add_tile(2 * jj + 1, add_tile(2 * jj, cnt))

            n_pairs = (i + 1) // 2
            cnt = lax.fori_loop(0, n_pairs, add_pair, jnp.zeros((ROW_CHUNK, LANES), jnp.int32))
            cnt = lax.fori_loop(2 * n_pairs, i + 1, add_tile, cnt)
            return jnp.sum(cnt.astype(F32), axis=-1, keepdims=True)

        thr_scr[rows] = jnp.broadcast_to(_topk_threshold(count_ge, ROW_CHUNK, topk), (ROW_CHUNK, LANES))

    m_scr[...] = jnp.full(m_scr.shape, NEG_BIG, F32)
    l_scr[...] = jnp.zeros(l_scr.shape, F32)
    acc_scr[...] = jnp.zeros(acc_scr.shape, F32)
    c_qk = scale * LOG2_E

    def qk(k0, width, h, rows):
        hs = slice(h * A_HEAD_DIM, (h + 1) * A_HEAD_DIM)
        return _dot_nt(qn_ref[0, rows, hs], kb_ref[0, pl.ds(k0, width), hs])

    def mask_of(k0, width, rows):
        return pltpu.bitcast(s_scr[rows, pl.ds(k0, width)], F32)

    def lane_fold(x, op):
        out = x[:, :LANES]
        for c in range(1, x.shape[1] // LANES):
            out = op(out, x[:, c * LANES:(c + 1) * LANES])
        return out

    def max_tile(k0, width, band_off):
        thr_t = jnp.tile(thr_scr[...], (1, width // LANES))
        mask = jnp.where(s_scr[:, pl.ds(k0, width)] >= thr_t, 0.0, NEG_BIG)
        s_scr[:, pl.ds(k0, width)] = pltpu.bitcast(mask, jnp.int32)
        for h in range(A_HEADS):
            for rc in range(tq // ATTN_ROWS):
                rows = slice(rc * ATTN_ROWS, (rc + 1) * ATTN_ROWS)
                if band_off is None:
                    top = lane_fold(qk(k0, width, h, rows) + mask_of(k0, width, rows),
                                    jnp.maximum) * c_qk + cfar_ref[h]
                else:
                    top = lane_fold(qk(k0, width, h, rows) * c_qk
                                    + band_ref[h, rows, band_off:band_off + width]
                                    + mask_of(k0, width, rows), jnp.maximum)
                m_scr[h, rows] = jnp.maximum(m_scr[h, rows], top)

    def sum_tile(k0, width, band_off):
        for h in range(A_HEADS):
            hs = slice(h * A_HEAD_DIM, (h + 1) * A_HEAD_DIM)
            for rc in range(tq // ATTN_ROWS):
                rows = slice(rc * ATTN_ROWS, (rc + 1) * ATTN_ROWS)
                if band_off is None:
                    shift = jnp.tile(cfar_ref[h] - m_scr[h, rows], (1, width // LANES))
                else:
                    shift = (band_ref[h, rows, band_off:band_off + width]
                             - jnp.tile(m_scr[h, rows], (1, width // LANES)))
                p = jnp.exp2(qk(k0, width, h, rows) * c_qk + shift + mask_of(k0, width, rows))
                l_scr[h, rows] = l_scr[h, rows] + lane_fold(p, jnp.add)
                acc_scr[rows, hs] = acc_scr[rows, hs] + _dot(p.astype(BF16),
                                                             vb_ref[0, pl.ds(k0, width), hs])

    def over_tiles(tile_fn):
        def far_body(j, carry):
            tile_fn(pl.multiple_of(j * kt, kt), kt, None)
            return carry

        lax.fori_loop(0, jnp.maximum(i - 1, 0), far_body, 0)

        @pl.when(i >= 1)
        def _():
            tile_fn(pl.multiple_of((i - 1) * kt, kt), kt, 0)

        tile_fn(pl.multiple_of(i * kt, kt), kt, kt)

    over_tiles(max_tile)
    for h in range(A_HEADS):
        m_scr[h] = jnp.broadcast_to(jnp.max(m_scr[h], axis=-1, keepdims=True), (tq, LANES))
    over_tiles(sum_tile)
    for h in range(A_HEADS):
        hs = slice(h * A_HEAD_DIM, (h + 1) * A_HEAD_DIM)
        o_ref[0, :, hs] = (acc_scr[:, hs] / jnp.sum(l_scr[h], axis=-1, keepdims=True)).astype(BF16)


def _rel_bucket(dist):
    n = jnp.maximum(dist, 0)
    max_exact = REL_BUCKETS // 2
    nf = jnp.maximum(n, 1).astype(F32)
    large = max_exact + (jnp.log(nf / max_exact) / math.log(REL_MAX_DIST / max_exact)
                         * (REL_BUCKETS - max_exact)).astype(jnp.int32)
    large = jnp.minimum(large, REL_BUCKETS - 1)
    return jnp.where(n < max_exact, n, large)


def _bias_band(rel_table, rows, kt):
    d = jnp.arange(rows)[:, None] + kt - jnp.arange(2 * kt)[None, :]
    hit = _rel_bucket(d)[None, :, :, None] == jnp.arange(REL_BUCKETS)
    return jnp.sum(jnp.where(hit, rel_table.T.astype(F32)[:, None, None, :], 0.0), axis=-1)


def _a_prompt(qn, iqm, z, kb, vb, ik2, band, cfar, batch, seq, tq):
    topk = min(TOPK_MAX, seq // 4)
    r3 = lambda a: a.reshape(batch, seq, a.shape[-1])
    qblk = lambda w: pl.BlockSpec((1, tq, w), lambda b, i: (b, i, 0))
    full = lambda w: pl.BlockSpec((1, seq, w), lambda b, i: (b, 0, 0), pipeline_mode=pl.Buffered(1))
    out = pl.pallas_call(
        functools.partial(_a_prompt_kernel, topk=topk),
        grid=(batch, seq // tq),
        in_specs=[pl.BlockSpec(memory_space=pltpu.SMEM),
                  qblk(A_WIDTH), qblk(IDX_HEADS * LANES),
                  pl.BlockSpec((1, tq, LANES), lambda b, i: (b, i, COL_MISC // LANES)),
                  full(A_WIDTH), full(A_WIDTH), full(LANES),
                  pl.BlockSpec((A_HEADS, tq, 2 * tq), lambda b, i: (0, 0, 0),
                               pipeline_mode=pl.Buffered(1))],
        out_specs=qblk(A_WIDTH),
        scratch_shapes=[pltpu.VMEM((tq, seq), jnp.int32),
                        pltpu.VMEM((IDX_HEADS, tq, LANES), F32),
                        pltpu.VMEM((tq, A_WIDTH), F32),
                        pltpu.VMEM((A_HEADS, tq, LANES), F32),
                        pltpu.VMEM((A_HEADS, tq, LANES), F32),
                        pltpu.VMEM((tq, LANES), jnp.int32)],
        out_shape=jax.ShapeDtypeStruct((batch, seq, A_WIDTH), BF16),
        compiler_params=_params(2),
        name="a_prompt",
    )(cfar, r3(qn), r3(iqm), r3(z), r3(kb), r3(vb), r3(ik2), band)
    return out.reshape(batch * seq, A_WIDTH)


def _a_sample_score_kernel(pt_ref, iq_ref, w_ref, mnew_ref, *rest, pages_per_step, n_pages, topk):
    page_refs = rest[:pages_per_step]
    keys_ref, thr_ref = rest[pages_per_step:]
    g = pl.program_id(1)
    rows = SAMPLE_ROWS
    iq = iq_ref[0]
    w = w_ref[0]

    def head_sum(x):
        return jnp.sum((jnp.maximum(x, 0.0) * w).reshape(IDX_HEADS, rows, PAGE_SIZE), axis=0)

    for p in range(pages_per_step):
        off = pl.multiple_of((g * pages_per_step + p) * PAGE_SIZE, PAGE_SIZE)
        keys_ref[0, :, pl.ds(off, PAGE_SIZE)] = _sortable_key(
            head_sum(_dot(iq, page_refs[p][...].astype(BF16))))

    @pl.when(g == pl.num_programs(1) - 1)
    def _():
        past = n_pages * PAGE_SIZE
        ik_new = mnew_ref[0][:, :IDX_DIM].astype(BF16)
        ik_new = jnp.concatenate([ik_new, jnp.zeros((PAGE_SIZE - rows, IDX_DIM), BF16)], axis=0)
        r = lax.broadcasted_iota(jnp.int32, (rows, PAGE_SIZE), 0)
        c = lax.broadcasted_iota(jnp.int32, (rows, PAGE_SIZE), 1)
        keys_ref[0, :, past:past + PAGE_SIZE] = jnp.where(
            c <= r, _sortable_key(head_sum(_dot_nt(iq, ik_new))), jnp.int32(INT_MIN))

        def count_ge(t):
            def body(j, cnt):
                off = pl.multiple_of(j * PAGE_SIZE, PAGE_SIZE)
                return cnt + jnp.where(keys_ref[0, :, pl.ds(off, PAGE_SIZE)] >= t,
                                       jnp.int32(1), jnp.int32(0))
            cnt = lax.fori_loop(0, n_pages + 1, body, jnp.zeros((rows, PAGE_SIZE), jnp.int32))
            return jnp.sum(cnt.astype(F32), axis=-1, keepdims=True)

        thr = _topk_threshold(count_ge, rows, topk)
        thr_ref[0] = jnp.broadcast_to(thr, (rows, LANES))


def _a_sample_scores(page_table, iq_hq, w_hq, z3, cache_kidx, layer, n_tok, pages_per_step):
    batch, n_pages = page_table.shape
    rows = SAMPLE_ROWS
    nk = (n_pages + 1) * PAGE_SIZE
    topk = min(TOPK_MAX, (n_pages * PAGE_SIZE + n_tok) // 4)

    def page_spec(p):
        return pl.BlockSpec((None, None, IDX_DIM, PAGE_SIZE),
                            lambda b, g, pt: (layer, pt[b, g * pages_per_step + p], 0, 0))

    return pl.pallas_call(
        functools.partial(_a_sample_score_kernel, pages_per_step=pages_per_step, n_pages=n_pages,
                          topk=topk),
        grid_spec=pltpu.PrefetchScalarGridSpec(
            num_scalar_prefetch=1,
            grid=(batch, n_pages // pages_per_step),
            in_specs=[pl.BlockSpec((1, IDX_HEADS * rows, IDX_DIM), lambda b, g, pt: (b, 0, 0)),
                      pl.BlockSpec((1, IDX_HEADS * rows, 1), lambda b, g, pt: (b, 0, 0)),
                      pl.BlockSpec((1, rows, LANES), lambda b, g, pt: (b, 0, COL_MISC // LANES))]
                     + [page_spec(p) for p in range(pages_per_step)],
            out_specs=[pl.BlockSpec((1, rows, nk), lambda b, g, pt: (b, 0, 0)),
                       pl.BlockSpec((1, rows, LANES), lambda b, g, pt: (b, 0, 0))]),
        out_shape=[jax.ShapeDtypeStruct((batch, rows, nk), jnp.int32),
                   jax.ShapeDtypeStruct((batch, rows, LANES), jnp.int32)],
        compiler_params=_params(2),
        name="a_sample_scores",
    )(page_table, iq_hq, w_hq, z3, *([cache_kidx.transpose(0, 1, 3, 2)] * pages_per_step))


def _a_sample_attn_kernel(pt_ref, qall_ref, keys_ref, thr_ref, knew_ref, vnew_ref, bias_far_ref, bias_last_ref,
                          bias_new_ref, expand_ref, *rest, pages_per_step, n_pages):
    k_refs = rest[:pages_per_step]
    v_refs = rest[pages_per_step:2 * pages_per_step]
    o_ref, acc_scr, m_scr, l_scr, lg_scr = rest[2 * pages_per_step:]
    g = pl.program_id(1)
    last_step = pl.num_programs(1) - 1
    rows = SAMPLE_ROWS
    hq = A_HEADS * rows
    cols = PAGE_SIZE * A_HEADS
    scale = A_HEAD_DIM ** -0.5
    thr = thr_ref[0][:, :1]
    qall = qall_ref[0]

    @pl.when(g == 0)
    def _():
        m_scr[...] = jnp.full(m_scr.shape, NEG_BIG, F32)
        l_scr[...] = jnp.zeros(l_scr.shape, F32)
        acc_scr[...] = jnp.zeros(acc_scr.shape, F32)

    def masked_logits(key_tile, expand, kf, bias):
        hit = jnp.where(key_tile >= thr, 1.0, 0.0).astype(BF16)
        drop = (_dot(hit, expand) - 1.0) * (-NEG_BIG)
        return _dot_nt(qall, kf) * scale + bias + jnp.concatenate([drop] * A_HEADS, axis=0)

    def lane_fold(x, op):
        out = x[:, :LANES]
        for c in range(1, x.shape[1] // LANES):
            out = op(out, x[:, c * LANES:(c + 1) * LANES])
        return out

    def update(n_tiles, width, v_of):
        top = lane_fold(lg_scr[:, :n_tiles * width], jnp.maximum)
        m_old = m_scr[...]
        m_new = jnp.maximum(m_old, jnp.max(top, axis=-1, keepdims=True))
        alpha = jnp.exp(m_old - m_new)
        part = jnp.zeros((hq, LANES), F32)
        acc = alpha * acc_scr[...]
        for t in range(n_tiles):
            p = jnp.exp(lg_scr[:, t * width:(t + 1) * width] - m_new)
            part = part + lane_fold(p, jnp.add)
            acc = acc + _dot(p.astype(BF16), v_of(t))
        l_scr[...] = alpha * l_scr[...] + jnp.sum(part, axis=-1, keepdims=True)
        acc_scr[...] = acc
        m_scr[...] = m_new

    for p in range(pages_per_step):
        page = g * pages_per_step + p
        off = pl.multiple_of(page * PAGE_SIZE, PAGE_SIZE)
        bias = bias_far_ref[...]
        if p == pages_per_step - 1:
            bias = jnp.where(g == last_step, bias_last_ref[...], bias)
        lg_scr[:, p * cols:(p + 1) * cols] = masked_logits(
            keys_ref[0, :, pl.ds(off, PAGE_SIZE)], expand_ref[...],
            k_refs[p][...].reshape(cols, A_HEAD_DIM).astype(BF16), bias)
    update(pages_per_step, cols, lambda t: v_refs[t][...].reshape(cols, A_HEAD_DIM).astype(BF16))

    @pl.when(g == last_step)
    def _():
        past = n_pages * PAGE_SIZE
        lg_scr[:, :hq] = masked_logits(keys_ref[0, :, past:past + PAGE_SIZE], expand_ref[:, :hq],
                                       knew_ref[0], bias_new_ref[...])
        update(1, hq, lambda t: vnew_ref[0])
        o_ref[0] = (acc_scr[...] / l_scr[...]).astype(BF16)


def _a_sample_attn(page_table, qall, keys, thr, knew, vnew, bias_far, bias_last, bias_new, expand,
                   cache_k, cache_v, layer, pages_per_step):
    batch, n_pages = page_table.shape
    rows = SAMPLE_ROWS
    hq = A_HEADS * rows
    nk = keys.shape[-1]

    def page_spec(p):
        return pl.BlockSpec((None, None, PAGE_SIZE, A_HEADS, A_HEAD_DIM),
                            lambda b, g, pt: (layer, pt[b, g * pages_per_step + p], 0, 0, 0))

    per_b = lambda r, w: pl.BlockSpec((1, r, w), lambda b, g, pt: (b, 0, 0))
    const = lambda r, w: pl.BlockSpec((r, w), lambda b, g, pt: (0, 0))
    return pl.pallas_call(
        functools.partial(_a_sample_attn_kernel, pages_per_step=pages_per_step, n_pages=n_pages),
        grid_spec=pltpu.PrefetchScalarGridSpec(
            num_scalar_prefetch=1,
            grid=(batch, n_pages // pages_per_step),
            in_specs=[per_b(hq, A_HEAD_DIM), per_b(rows, nk), per_b(rows, LANES),
                      per_b(hq, A_HEAD_DIM), per_b(hq, A_HEAD_DIM),
                      const(hq, PAGE_SIZE * A_HEADS), const(hq, PAGE_SIZE * A_HEADS), const(hq, hq),
                      const(PAGE_SIZE, PAGE_SIZE * A_HEADS)]
                     + [page_spec(p) for p in range(pages_per_step)] * 2,
            out_specs=per_b(hq, A_HEAD_DIM),
            scratch_shapes=[pltpu.VMEM((hq, A_HEAD_DIM), F32),
                            pltpu.VMEM((hq, 1), F32),
                            pltpu.VMEM((hq, 1), F32),
                            pltpu.VMEM((hq, pages_per_step * PAGE_SIZE * A_HEADS), F32)]),
        out_shape=jax.ShapeDtypeStruct((batch, hq, A_HEAD_DIM), BF16),
        compiler_params=_params(2),
        name="a_sample_attn",
    )(page_table, qall, keys, thr, knew, vnew, bias_far, bias_last, bias_new, expand,
      *([cache_k] * pages_per_step), *([cache_v] * pages_per_step))


def _pad_rows(x, rows):
    if x.shape[0] == rows:
        return x
    return jnp.concatenate([x, jnp.zeros((rows - x.shape[0], x.shape[1]), x.dtype)], axis=0)


def _split3(x):
    hi = x.astype(BF16)
    r1 = x - hi.astype(F32)
    mid = r1.astype(BF16)
    lo = (r1 - mid.astype(F32)).astype(BF16)
    return hi, mid, lo


def _gla_kernel(gq_ref, gk_ref, gv_ref, gg_ref, misc_ref, wa_ref, ba_ref, gn_ref, s0_ref,
                o_ref, sfin_ref, st_scr, oacc_scr, b_scr, q_scr, *, chunk, rows, n_valid):
    tt = gq_ref.shape[1]

    @pl.when(pl.program_id(1) == 0)
    def _():
        st_scr[...] = s0_ref[0]

    ga = _pad_rows(misc_ref[0][:, MISC_GA:MISC_GA + GLA_RANK], rows)
    x = _dot(ga.astype(BF16), wa_ref[...]) + ba_ref[...]
    la = (jnp.minimum(x, 0.0) - jnp.log1p(jnp.exp(-jnp.abs(x)))) * (1.0 / GLA_TAU)
    r = lax.broadcasted_iota(jnp.int32, (rows, rows), 0)
    c = lax.broadcasted_iota(jnp.int32, (rows, rows), 1)
    if n_valid < rows:
        rr = lax.broadcasted_iota(jnp.int32, la.shape, 0)
        la = jnp.where(rr < n_valid, la, 0.0)
    tri = jnp.where((c <= r) & (r // chunk == c // chunk), 1.0, 0.0).astype(BF16)
    hi, mid, lo = _split3(la)
    b = _dot(tri, hi) + _dot(tri, mid) + _dot(tri, lo)

    gq = _pad_rows(gq_ref[0], rows) * (GLA_DK ** -0.5)
    gk = _pad_rows(gk_ref[0], rows)
    if n_valid < rows:
        gk = jnp.where(rr < n_valid, gk, 0.0)
    gv = _pad_rows(gv_ref[0], rows)
    causal = (lax.broadcasted_iota(jnp.int32, (chunk, chunk), 1)
              <= lax.broadcasted_iota(jnp.int32, (chunk, chunk), 0))
    n_chunks = rows // chunk
    mid = chunk // 2 - 1

    def pieces(ci, h):
        return (slice(ci * chunk, (ci + 1) * chunk), slice(h * GLA_DK, (h + 1) * GLA_DK),
                slice(h * GLA_DV, (h + 1) * GLA_DV))

    spread = jnp.float32(0.0)
    for ci in range(n_chunks):
        bcf = b[ci * chunk:(ci + 1) * chunk, :]
        spread = jnp.maximum(spread, jnp.max(jnp.abs(bcf - bcf[mid:mid + 1, :])))
    factored_ok = spread <= GLA_MAX_FACTORED_EXPONENT

    @pl.when(factored_ok)
    def _():
        for ci in range(n_chunks):
            for h in range(GLA_HEADS):
                rs, ks, vs = pieces(ci, h)
                bc = b[rs, ks]
                bmid = bc[mid:mid + 1, :]
                qm = (gq[rs, ks] * jnp.exp(bc - bmid)).astype(BF16)
                km = (gk[rs, ks] * jnp.exp(bmid - bc)).astype(BF16)
                sc = jnp.where(causal, _dot_nt(qm, km), 0.0)
                oacc_scr[rs, vs] = _dot(sc.astype(BF16), gv[rs, vs].astype(BF16))

    @pl.when(jnp.logical_not(factored_ok))
    def _():
        b_scr[...] = b
        q_scr[...] = gq
        s_idx = lax.broadcasted_iota(jnp.int32, (chunk, 1), 0)
        for ci in range(n_chunks):
            for h in range(GLA_HEADS):
                rs, ks, vs = pieces(ci, h)
                bc = b[rs, ks]
                kc = gk[rs, ks]
                vc = gv[rs, vs]

                def row_group(g, carry, ci=ci, ks=ks, vs=vs, bc=bc, kc=kc, vc=vc):
                    base = pl.multiple_of(ci * chunk + g * SUBLANES, SUBLANES)
                    b_rows = b_scr[pl.ds(base, SUBLANES), :][:, ks]
                    q_rows = q_scr[pl.ds(base, SUBLANES), :][:, ks]
                    outs = []
                    for j in range(SUBLANES):
                        decay = jnp.exp(jnp.minimum(b_rows[j:j + 1] - bc, 0.0))
                        wgt = jnp.sum(kc * decay * q_rows[j:j + 1], axis=-1, keepdims=True)
                        wgt = jnp.where(s_idx <= g * SUBLANES + j, wgt, 0.0)
                        outs.append(jnp.sum(wgt * vc, axis=0, keepdims=True))
                    oacc_scr[pl.ds(base, SUBLANES), vs] = jnp.concatenate(outs, axis=0)
                    return carry

                lax.fori_loop(0, chunk // SUBLANES, row_group, 0)

    for ci in range(n_chunks):
        for h in range(GLA_HEADS):
            rs, ks, vs = pieces(ci, h)
            bc = b[rs, ks]
            bend = bc[chunk - 1:chunk, :]
            qt = (gq[rs, ks] * jnp.exp(bc)).astype(BF16)
            kd = (gk[rs, ks] * jnp.exp(bend - bc)).astype(BF16)
            st = st_scr[h]
            oacc_scr[rs, vs] = oacc_scr[rs, vs] + _dot_nt(qt, st.astype(BF16))
            st_scr[h] = st * jnp.exp(bend) + _dot_tn(gv[rs, vs].astype(BF16), kd)

    gg = gg_ref[0]
    for h in range(GLA_HEADS):
        vs = slice(h * GLA_DV, (h + 1) * GLA_DV)
        oh = oacc_scr[0:tt, vs]
        ms = jnp.mean(oh * oh, axis=-1, keepdims=True)
        g = gg[:, vs]
        o_ref[0, :, vs] = (oh * lax.rsqrt(ms + EPS) * gn_ref[...] * (g * jax.nn.sigmoid(g))).astype(BF16)

    @pl.when(pl.program_id(1) == pl.num_programs(1) - 1)
    def _():
        sfin_ref[0] = st_scr[...]


def _gla(z3, wa, ba, gn, s0t, tt, chunk, rows, n_valid):
    batch, seq, _ = z3.shape
    q_blk = COL_GQ // 256
    v_blk = COL_GV // GLA_WIDTH
    tok = lambda w, cblk: pl.BlockSpec((1, tt, w), lambda b, t: (b, t, cblk))
    const = lambda shape: pl.BlockSpec(shape, lambda b, t: (0,) * len(shape))
    st_spec = pl.BlockSpec((1, GLA_HEADS, GLA_DV, GLA_DK), lambda b, t: (b, 0, 0, 0))
    return pl.pallas_call(
        functools.partial(_gla_kernel, chunk=chunk, rows=rows, n_valid=n_valid),
        grid=(batch, seq // tt),
        in_specs=[tok(256, q_blk), tok(256, q_blk + 1), tok(GLA_WIDTH, v_blk), tok(GLA_WIDTH, v_blk + 1),
                  tok(LANES, COL_MISC // LANES),
                  const((GLA_RANK, GLA_HEADS * GLA_DK)), const((1, GLA_HEADS * GLA_DK)),
                  const((1, GLA_DV)), st_spec],
        out_specs=[pl.BlockSpec((1, tt, GLA_WIDTH), lambda b, t: (b, t, 0)), st_spec],
        out_shape=[jax.ShapeDtypeStruct((batch, seq, GLA_WIDTH), BF16),
                   jax.ShapeDtypeStruct((batch, GLA_HEADS, GLA_DV, GLA_DK), F32)],
        scratch_shapes=[pltpu.VMEM((GLA_HEADS, GLA_DV, GLA_DK), F32),
                        pltpu.VMEM((rows, GLA_WIDTH), F32),
                        pltpu.VMEM((rows, GLA_HEADS * GLA_DK), F32),
                        pltpu.VMEM((rows, GLA_HEADS * GLA_DK), F32)],
        compiler_params=_params(2),
        name="gla",
    )(z3, z3, z3, z3, z3, wa, ba, gn, s0t)


def _ret_kernel(rq_ref, rk_ref, rv_ref, rg_ref, cos_ref, sin_ref, gn_ref, s0_ref,
                o_ref, sfin_ref, st_scr, oacc_scr, *, rows, n_valid, log_gamma):
    tt = rq_ref.shape[1]
    width = RET_HEADS * RET_DK

    @pl.when(pl.program_id(1) == 0)
    def _():
        st_scr[...] = s0_ref[0]

    lane = lax.broadcasted_iota(jnp.int32, (rows, width), 1)
    first_half = (lane % RET_DK) < (RET_DK // 2)
    cos = _pad_rows(cos_ref[...], rows)
    sin = _pad_rows(sin_ref[...], rows)

    def rope(x):
        rot = jnp.where(first_half, pltpu.roll(x, width - RET_DK // 2, axis=1),
                        pltpu.roll(x, RET_DK // 2, axis=1))
        return x * cos + rot * sin

    q = rope(_pad_rows(rq_ref[0], rows))
    k = rope(_pad_rows(rk_ref[0], rows)) * (RET_DK ** -0.5)
    v = _pad_rows(rv_ref[0], rows)
    rr = lax.broadcasted_iota(jnp.int32, (rows, RET_DK), 0)
    if n_valid < rows:
        k = jnp.where(lax.broadcasted_iota(jnp.int32, k.shape, 0) < n_valid, k, 0.0)
    t_i = lax.broadcasted_iota(jnp.int32, (rows, rows), 0)
    s_i = lax.broadcasted_iota(jnp.int32, (rows, rows), 1)
    dist = (t_i - s_i).astype(F32)
    pos1 = (rr + 1).astype(F32)
    rem = (n_valid - 1 - rr).astype(F32)

    for h in range(RET_HEADS):
        ks = slice(h * RET_DK, (h + 1) * RET_DK)
        vs = slice(h * RET_DV, (h + 1) * RET_DV)
        lg = log_gamma[h]
        decay = jnp.where(t_i >= s_i, jnp.exp(dist * lg), 0.0)
        qh = q[:, ks]
        kh = k[:, ks]
        vh = v[:, vs].astype(BF16)
        st = st_scr[h]
        sc = _dot_nt(qh.astype(BF16), kh.astype(BF16)) * decay
        q_in = (qh * jnp.exp(pos1 * lg)).astype(BF16)
        oacc_scr[:, vs] = _dot(sc.astype(BF16), vh) + _dot_nt(q_in, st.astype(BF16))
        kd = (kh * jnp.exp(rem * lg)).astype(BF16)
        st_scr[h] = st * math.exp(n_valid * lg) + _dot_tn(vh, kd)

    rg = rg_ref[0]
    for h in range(RET_HEADS):
        vs = slice(h * RET_DV, (h + 1) * RET_DV)
        oh = oacc_scr[0:tt, vs]
        oc = oh - jnp.mean(oh, axis=-1, keepdims=True)
        var = jnp.mean(oc * oc, axis=-1, keepdims=True)
        g = rg[:, vs]
        o_ref[0, :, vs] = (oc * lax.rsqrt(var + EPS) * gn_ref[...] * (g * jax.nn.sigmoid(g))).astype(BF16)

    @pl.when(pl.program_id(1) == pl.num_programs(1) - 1)
    def _():
        sfin_ref[0] = st_scr[...]


def _ret(z3, cos, sin, gn, s0t, tt, rows, n_valid):
    batch, seq, _ = z3.shape
    q_blk = COL_GQ // 256 + 2
    v_blk = COL_GV // RET_WIDTH + 2
    width = RET_HEADS * RET_DK
    log_gamma = tuple(float(np.log1p(-np.exp2(np.float32(-5.0 - h)), dtype=np.float32))
                      for h in range(RET_HEADS))
    tok = lambda w, cblk: pl.BlockSpec((1, tt, w), lambda b, t: (b, t, cblk))
    st_spec = pl.BlockSpec((1, RET_HEADS, RET_DV, RET_DK), lambda b, t: (b, 0, 0, 0))
    return pl.pallas_call(
        functools.partial(_ret_kernel, rows=rows, n_valid=n_valid, log_gamma=log_gamma),
        grid=(batch, seq // tt),
        in_specs=[tok(256, q_blk), tok(256, q_blk + 1), tok(RET_WIDTH, v_blk), tok(RET_WIDTH, v_blk + 1),
                  pl.BlockSpec((tt, width), lambda b, t: (t, 0)),
                  pl.BlockSpec((tt, width), lambda b, t: (t, 0)),
                  pl.BlockSpec((1, RET_DV), lambda b, t: (0, 0)), st_spec],
        out_specs=[pl.BlockSpec((1, tt, RET_WIDTH), lambda b, t: (b, t, 0)), st_spec],
        out_shape=[jax.ShapeDtypeStruct((batch, seq, RET_WIDTH), BF16),
                   jax.ShapeDtypeStruct((batch, RET_HEADS, RET_DV, RET_DK), F32)],
        scratch_shapes=[pltpu.VMEM((RET_HEADS, RET_DV, RET_DK), F32),
                        pltpu.VMEM((rows, RET_WIDTH), F32)],
        compiler_params=_params(2),
        name="ret",
    )(z3, z3, z3, z3, cos, sin, gn, s0t)


def _rope_tables(pos):
    half = RET_DK // 2
    freqs = ROPE_BASE ** (-jnp.arange(half, dtype=F32) / half)
    ang = pos.astype(F32)[:, None] * freqs[None, :]
    cos = jnp.cos(ang)
    sin = jnp.sin(ang)
    cos_t = jnp.tile(jnp.concatenate([cos, cos], axis=-1), (1, RET_HEADS))
    sin_t = jnp.tile(jnp.concatenate([-sin, sin], axis=-1), (1, RET_HEADS))
    return cos_t, sin_t


def _permuted_segments():
    names = ("aq", "ak", "av", "iq", "ik", "iw", "gq", "gk", "gv", "ga", "gg", "rq", "rk", "rv", "rg", "gates")
    src, acc = {}, 0
    for name, size in zip(names, IN_SIZES):
        src[name] = (acc, size)
        acc += size
    order = ("gates", "aq", "ak", "av", "iq", "gv", "gg", "rv", "rg", "gq", "gk", "rq", "rk", "ik", "iw", "ga")
    out, dst = [], 0
    for name in order:
        out.append((src[name][0], dst, src[name][1]))
        dst += src[name][1]
    return out, dst


def _permute_w_in_kernel(w_ref, o_ref):
    segments, used = _permuted_segments()
    for src, dst, size in segments:
        o_ref[dst:dst + size, :] = w_ref[src:src + size, :].astype(BF16)
    o_ref[used:, :] = jnp.zeros((IN_PADDED - used, o_ref.shape[1]), BF16)


def _permute_w_in(w_in_t, layer):
    _, n, d = w_in_t.shape
    tc = LANES
    return pl.pallas_call(
        _permute_w_in_kernel,
        grid=(d // tc,),
        in_specs=[pl.BlockSpec((None, n, tc), lambda j: (layer, 0, j))],
        out_specs=pl.BlockSpec((IN_PADDED, tc), lambda j: (0, j)),
        out_shape=jax.ShapeDtypeStruct((IN_PADDED, d), BF16),
        compiler_params=_params(1),
        name="permute_w_in",
    )(w_in_t)


def _dense_tail(x, z, oa, ob, oc, w):
    m = x.shape[0]
    layer = w["layer"]
    merged = _merge(oa, ob, oc, z, w["w_branch"], layer, min(m, 256), D_MODEL)
    x = _matmul_residual(merged, w["w_out"], layer, x, min(m, 512), D_MODEL, "out_proj")
    act = _norm_swiglu(x, w["norm_ffn"], w["w_ffn_in"], layer, min(m, 1024), 512)
    return _matmul_residual(act, w["w_ffn_out"], layer, x, min(m, 1024), 512, "ffn_out")


def _prompt_layer(x, w, batch, seq, band, cfar, cos, sin):
    tm = 512
    z = _norm_matmul(x, w["norm_mix"], w["w_in"], tm, IN_PROJ_COLS)
    qn, kf, kb, vf, vb, iqm, ik2 = _prep_a(z, w["a_q_norm"], w["a_k_norm"], tm)
    oa = _a_prompt(qn, iqm, z, kb, vb, ik2, band * LOG2_E, cfar * LOG2_E, batch, seq, 256)
    z3 = z.reshape(batch, seq, IN_PADDED)
    zero_state = jnp.zeros((batch, GLA_HEADS, GLA_DV, GLA_DK), F32)
    ob, s_gla = _gla(z3, w["gla_wa"], w["gla_ba"], w["gla_norm"], zero_state, 256, CHUNK, 256, 256)
    oc, s_ret = _ret(z3, cos, sin, w["ret_norm"], zero_state, 256, 256, 256)
    x = _dense_tail(x, z, oa, ob.reshape(-1, GLA_WIDTH), oc.reshape(-1, RET_WIDTH), w)
    kidx = z3[:, :, COL_MISC:COL_MISC + IDX_DIM]
    new = (kf.reshape(batch, seq, A_HEADS, A_HEAD_DIM), vf.reshape(batch, seq, A_HEADS, A_HEAD_DIM),
           kidx, s_gla.transpose(0, 1, 3, 2), s_ret.transpose(0, 1, 3, 2))
    return x, new


def _sample_layer(x, w, layer, batch, n_tok, page_table, cache_k, cache_v, cache_kidx,
                  s_gla0, s_ret0, bias_s, expand, cos, sin):
    rows = SAMPLE_ROWS
    hq = A_HEADS * rows
    tm = batch * rows
    z = _norm_matmul(x, w["norm_mix"], w["w_in"], tm, IN_PROJ_COLS)
    qn, kf, kb, vf, vb, _, _ = _prep_a(z, w["a_q_norm"], w["a_k_norm"], tm)
    z3 = z.reshape(batch, rows, IN_PADDED)
    iq = z3[:, :, COL_AQ + 3 * A_WIDTH:COL_AQ + 4 * A_WIDTH].reshape(batch, rows, IDX_HEADS, IDX_DIM)
    iq_hq = iq.transpose(0, 2, 1, 3).reshape(batch, IDX_HEADS * rows, IDX_DIM).astype(BF16)
    iw = z3[:, :, COL_MISC + MISC_IW:COL_MISC + MISC_IW + IDX_HEADS] * ((IDX_HEADS * IDX_DIM) ** -0.5)
    w_hq = iw.transpose(0, 2, 1).reshape(batch, IDX_HEADS * rows, 1)
    keys, thr = _a_sample_scores(page_table, iq_hq, w_hq, z3, cache_kidx, layer, n_tok, 32)
    qall = qn.reshape(batch, rows, A_HEADS, A_HEAD_DIM).transpose(0, 2, 1, 3).reshape(batch, hq, A_HEAD_DIM)
    oa = _a_sample_attn(page_table, qall, keys, thr, kb.reshape(batch, hq, A_HEAD_DIM),
                        vb.reshape(batch, hq, A_HEAD_DIM), *bias_s, expand, cache_k, cache_v, layer, 8)
    oa = oa.reshape(batch, A_HEADS, rows, A_HEAD_DIM).transpose(0, 2, 1, 3).reshape(tm, A_WIDTH)
    ob, s_gla = _gla(z3, w["gla_wa"], w["gla_ba"], w["gla_norm"], s_gla0.transpose(0, 1, 3, 2),
                     rows, LANES, LANES, n_tok)
    oc, s_ret = _ret(z3, cos, sin, w["ret_norm"], s_ret0.transpose(0, 1, 3, 2), rows, LANES, n_tok)
    x = _dense_tail(x, z, oa, ob.reshape(tm, GLA_WIDTH), oc.reshape(tm, RET_WIDTH), w)
    kidx = z3[:, :n_tok, COL_MISC:COL_MISC + IDX_DIM]
    new = (kf.reshape(batch, rows, A_HEADS, A_HEAD_DIM)[:, :n_tok],
           vf.reshape(batch, rows, A_HEADS, A_HEAD_DIM)[:, :n_tok],
           kidx, s_gla.transpose(0, 1, 3, 2), s_ret.transpose(0, 1, 3, 2))
    return x, new


def _sample_bias(rel_table):
    rows = SAMPLE_ROWS
    hq = A_HEADS * rows
    band = _bias_band(rel_table, rows, PAGE_SIZE)
    cfar_rows = jnp.repeat(rel_table[REL_BUCKETS - 1].astype(F32), rows)[:, None]
    band_last = jnp.repeat(band[:, :, :PAGE_SIZE].reshape(hq, PAGE_SIZE), A_HEADS, axis=1)
    band_new = jnp.repeat(band[:, :, PAGE_SIZE:PAGE_SIZE + rows].reshape(hq, rows), A_HEADS, axis=1)
    cols = PAGE_SIZE * A_HEADS
    other_head = (jnp.arange(hq)[:, None] // rows) != (jnp.arange(cols)[None, :] % A_HEADS)
    head_mask = jnp.where(other_head, NEG_BIG, 0.0).astype(F32)
    return cfar_rows + head_mask, band_last + head_mask, band_new + head_mask[:, :hq]


def kernel(x_prompt, x_sample, cache_k, cache_v, cache_kidx, state_gla, state_ret, page_table, rel_table, w_in, a_q_norm, a_k_norm, gla_wa, gla_ba, gla_norm, ret_norm, w_branch, w_out, norm_mix, norm_ffn, w_ffn_in, w_ffn_out):
    depth = w_in.shape[0]
    bp, tp, d = x_prompt.shape
    bs, ts, _ = x_sample.shape
    past = page_table.shape[1] * PAGE_SIZE
    tq = 256
    row = lambda a: a.reshape(1, -1).astype(F32)

    cfar = rel_table[REL_BUCKETS - 1].astype(F32)
    band_p = _bias_band(rel_table, tq, tq)
    bias_s = _sample_bias(rel_table)
    expand = jnp.asarray(np.kron(np.eye(PAGE_SIZE), np.ones((1, A_HEADS))), dtype=BF16)
    cos_p, sin_p = _rope_tables(jnp.arange(tp))
    cos_s, sin_s = _rope_tables(past + jnp.arange(SAMPLE_ROWS))

    w_in_t = jnp.swapaxes(w_in, 1, 2)
    xp = x_prompt.reshape(bp * tp, d)
    xs = jnp.pad(x_sample, ((0, 0), (0, SAMPLE_ROWS - ts), (0, 0))).reshape(bs * SAMPLE_ROWS, d)
    rows_p, rows_s = [], []
    for l in range(depth):
        w = dict(layer=l, w_in=_permute_w_in(w_in_t, l), a_q_norm=row(a_q_norm[l]), a_k_norm=row(a_k_norm[l]),
                 gla_wa=gla_wa[l].astype(BF16), gla_ba=row(gla_ba[l]), gla_norm=row(gla_norm[l]),
                 ret_norm=row(ret_norm[l]), w_branch=w_branch, w_out=w_out,
                 norm_mix=row(norm_mix[l]), norm_ffn=row(norm_ffn[l]), w_ffn_in=w_ffn_in,
                 w_ffn_out=w_ffn_out)
        xp, new_p = _prompt_layer(xp, w, bp, tp, band_p, cfar, cos_p, sin_p)
        xs, new_s = _sample_layer(xs, w, l, bs, ts, page_table, cache_k, cache_v, cache_kidx,
                                  state_gla[l], state_ret[l], bias_s, expand, cos_s, sin_s)
        rows_p.append(new_p)
        rows_s.append(new_s)
    outs_p = [jnp.stack(r) for r in zip(*rows_p)]
    outs_s = [jnp.stack(r) for r in zip(*rows_s)]
    y_p = xp.reshape(bp, tp, d)
    y_s = xs.reshape(bs, SAMPLE_ROWS, d)[:, :ts]
    return (y_p, y_s, *outs_p, *outs_s)
```

```python
import functools
import math

import numpy as np
import jax
import jax.numpy as jnp
from jax import lax
from jax.experimental import pallas as pl
from jax.experimental.pallas import tpu as pltpu

D_MODEL = 2048
PAGE_SIZE = 128
A_HEADS = 8
A_HEAD_DIM = 128
A_WIDTH = A_HEADS * A_HEAD_DIM
IDX_HEADS = 16
IDX_DIM = 64
TOPK_MAX = 256
REL_BUCKETS = 32
REL_MAX_DIST = 128
GLA_HEADS = 4
GLA_DK = 64
GLA_DV = 128
GLA_WIDTH = GLA_HEADS * GLA_DV
GLA_RANK = 16
GLA_TAU = 16.0
RET_HEADS = 4
RET_DK = 64
RET_DV = 128
RET_WIDTH = RET_HEADS * RET_DV
ROPE_BASE = 10000.0
CHUNK = 64
MIX_WIDTH = A_WIDTH + GLA_WIDTH + RET_WIDTH
D_FF = -(-8 * D_MODEL // (3 * 256)) * 256
EPS = 1e-6
IN_SIZES = (A_WIDTH, A_WIDTH, A_WIDTH, IDX_HEADS * IDX_DIM, IDX_DIM, IDX_HEADS,
            GLA_HEADS * GLA_DK, GLA_HEADS * GLA_DK, GLA_WIDTH, GLA_RANK, GLA_WIDTH,
            RET_HEADS * RET_DK, RET_HEADS * RET_DK, RET_WIDTH, RET_WIDTH, 3 * D_MODEL)

COL_GATES = 0
COL_AQ = 3 * D_MODEL
COL_GV = COL_AQ + 4 * A_WIDTH
COL_GQ = COL_GV + 4 * GLA_WIDTH
COL_MISC = COL_GQ + 4 * 256
IN_PADDED = COL_MISC + 128
IN_PROJ_COLS = IN_PADDED // 5
MISC_IW = IDX_DIM
MISC_GA = IDX_DIM + IDX_HEADS

LANES = 128
SUBLANES = 8
SAMPLE_ROWS = 16
VMEM_LIMIT = 56 * 1024 * 1024
INT_MIN = -2147483648
NEG_BIG = -1e30
LOG2_E = math.log2(math.e)
GLA_MAX_FACTORED_EXPONENT = 60.0
ROW_CHUNK = 128
ATTN_ROWS = 256

BF16 = jnp.bfloat16
F32 = jnp.float32
NT_DIMS = (((1,), (1,)), ((), ()))
TN_DIMS = (((0,), (0,)), ((), ()))


def _params(n_axes):
    return pltpu.CompilerParams(dimension_semantics=("arbitrary",) * n_axes,
                                vmem_limit_bytes=VMEM_LIMIT)


def _resident(constant_index):
    return pl.Buffered(1) if constant_index else None


def _dot(a, b):
    return jnp.dot(a, b, preferred_element_type=F32)


def _dot_nt(a, b):
    return lax.dot_general(a, b, NT_DIMS, preferred_element_type=F32)


def _dot_tn(a, b):
    return lax.dot_general(a, b, TN_DIMS, preferred_element_type=F32)


def _norm_matmul_kernel(x_ref, g_ref, w_ref, o_ref, hb_ref):
    @pl.when(pl.program_id(1) == 0)
    def _():
        x = x_ref[...]
        ms = jnp.mean(x * x, axis=-1, keepdims=True)
        hb_ref[...] = (x * lax.rsqrt(ms + EPS) * g_ref[...]).astype(BF16)

    o_ref[...] = _dot_nt(hb_ref[...], w_ref[...])


def _norm_matmul(x, g, w, tm, tn):
    m, d = x.shape
    n = w.shape[0]
    return pl.pallas_call(
        _norm_matmul_kernel,
        grid=(m // tm, n // tn),
        in_specs=[pl.BlockSpec((tm, d), lambda i, j: (i, 0)),
                  pl.BlockSpec((1, d), lambda i, j: (0, 0)),
                  pl.BlockSpec((tn, d), lambda i, j: (j, 0))],
        out_specs=pl.BlockSpec((tm, tn), lambda i, j: (i, j)),
        out_shape=jax.ShapeDtypeStruct((m, n), F32),
        scratch_shapes=[pltpu.VMEM((tm, d), BF16)],
        compiler_params=_params(2),
        name="in_proj",
    )(x, g, w)


def _norm_swiglu_kernel(x_ref, g_ref, wg_ref, wu_ref, o_ref, hb_ref):
    @pl.when(pl.program_id(1) == 0)
    def _():
        x = x_ref[...]
        ms = jnp.mean(x * x, axis=-1, keepdims=True)
        hb_ref[...] = (x * lax.rsqrt(ms + EPS) * g_ref[...]).astype(BF16)

    h = hb_ref[...]
    gate = _dot(h, wg_ref[...].astype(BF16))
    up = _dot(h, wu_ref[...].astype(BF16))
    o_ref[...] = (gate * jax.nn.sigmoid(gate) * up).astype(BF16)


def _norm_swiglu(x, g, w, layer, tm, tf):
    m, d = x.shape
    f = w.shape[2] // 2
    nf = f // tf
    return pl.pallas_call(
        _norm_swiglu_kernel,
        grid=(m // tm, nf),
        in_specs=[pl.BlockSpec((tm, d), lambda i, j: (i, 0)),
                  pl.BlockSpec((1, d), lambda i, j: (0, 0)),
                  pl.BlockSpec((None, d, tf), lambda i, j: (layer, 0, j)),
                  pl.BlockSpec((None, d, tf), lambda i, j: (layer, 0, j + nf))],
        out_specs=pl.BlockSpec((tm, tf), lambda i, j: (i, j)),
        out_shape=jax.ShapeDtypeStruct((m, f), BF16),
        scratch_shapes=[pltpu.VMEM((tm, d), BF16)],
        compiler_params=_params(2),
        name="swiglu",
    )(x, g, w, w)


def _matmul_residual_kernel(a_ref, w_ref, r_ref, o_ref, wb_ref):
    @pl.when(pl.program_id(1) == 0)
    def _():
        wb_ref[...] = w_ref[...].astype(BF16)

    o_ref[...] = r_ref[...] + _dot(a_ref[...], wb_ref[...])


def _matmul_residual(a, w, layer, r, tm, tn, name):
    m, k = a.shape
    n = w.shape[2]
    return pl.pallas_call(
        _matmul_residual_kernel,
        grid=(n // tn, m // tm),
        in_specs=[pl.BlockSpec((tm, k), lambda j, i: (i, 0)),
                  pl.BlockSpec((None, k, tn), lambda j, i: (layer, 0, j), pipeline_mode=pl.Buffered(1)),
                  pl.BlockSpec((tm, tn), lambda j, i: (i, j))],
        out_specs=pl.BlockSpec((tm, tn), lambda j, i: (i, j)),
        out_shape=jax.ShapeDtypeStruct((m, n), F32),
        scratch_shapes=[pltpu.VMEM((k, tn), BF16)],
        compiler_params=_params(2),
        name=name,
    )(a, w, r)


def _merge_kernel(oa_ref, ob_ref, oc_ref, ga_ref, gb_ref, gc_ref, wa_ref, wb_ref, wc_ref, o_ref):
    ya = _dot(oa_ref[...], wa_ref[...].astype(BF16))
    yb = _dot(ob_ref[...], wb_ref[...].astype(BF16))
    yc = _dot(oc_ref[...], wc_ref[...].astype(BF16))
    merged = (jax.nn.sigmoid(ga_ref[...]) * ya + jax.nn.sigmoid(gb_ref[...]) * yb
              + jax.nn.sigmoid(gc_ref[...]) * yc)
    o_ref[...] = merged.astype(BF16)


def _merge(oa, ob, oc, z, w_branch, layer, tm, tn):
    m = oa.shape[0]
    n = D_MODEL
    nb = n // tn
    a_blocks = A_WIDTH // GLA_WIDTH
    return pl.pallas_call(
        _merge_kernel,
        grid=(m // tm, nb),
        in_specs=[pl.BlockSpec((tm, A_WIDTH), lambda i, j: (i, 0)),
                  pl.BlockSpec((tm, GLA_WIDTH), lambda i, j: (i, 0)),
                  pl.BlockSpec((tm, RET_WIDTH), lambda i, j: (i, 0)),
                  pl.BlockSpec((tm, tn), lambda i, j: (i, j)),
                  pl.BlockSpec((tm, tn), lambda i, j: (i, j + nb)),
                  pl.BlockSpec((tm, tn), lambda i, j: (i, j + 2 * nb)),
                  pl.BlockSpec((None, A_WIDTH, tn), lambda i, j: (layer, 0, j),
                               pipeline_mode=_resident(n == tn)),
                  pl.BlockSpec((None, GLA_WIDTH, tn), lambda i, j: (layer, a_blocks, j),
                               pipeline_mode=_resident(n == tn)),
                  pl.BlockSpec((None, RET_WIDTH, tn), lambda i, j: (layer, a_blocks + 1, j),
                               pipeline_mode=_resident(n == tn))],
        out_specs=pl.BlockSpec((tm, tn), lambda i, j: (i, j)),
        out_shape=jax.ShapeDtypeStruct((m, n), BF16),
        compiler_params=_params(2),
        name="merge",
    )(oa, ob, oc, z, z, z, w_branch, w_branch, w_branch)


def _head_rmsnorm(x, g):
    outs = []
    for h in range(A_HEADS):
        xh = x[:, h * A_HEAD_DIM:(h + 1) * A_HEAD_DIM]
        ms = jnp.mean(xh * xh, axis=-1, keepdims=True)
        outs.append(xh * lax.rsqrt(ms + EPS) * g)
    return jnp.concatenate(outs, axis=-1)


def _prep_a_kernel(aq_ref, ak_ref, av_ref, iq_ref, misc_ref, gq_ref, gk_ref,
                   qn_ref, kf_ref, kb_ref, vf_ref, vb_ref, iqm_ref, ik2_ref):
    qn_ref[...] = _head_rmsnorm(aq_ref[...], gq_ref[...]).astype(BF16)
    kn = _head_rmsnorm(ak_ref[...], gk_ref[...])
    kf_ref[...] = kn
    kb_ref[...] = kn.astype(BF16)
    v = av_ref[...]
    vf_ref[...] = v
    vb_ref[...] = v.astype(BF16)
    iq = iq_ref[...].astype(BF16)
    lane = lax.broadcasted_iota(jnp.int32, (iq.shape[0], LANES), 1)
    zero = jnp.zeros((iq.shape[0], LANES), BF16)
    for h in range(IDX_HEADS):
        pair = iq[:, (h // 2) * LANES:(h // 2 + 1) * LANES]
        keep = (lane < IDX_DIM) if h % 2 == 0 else (lane >= IDX_DIM)
        iqm_ref[:, h * LANES:(h + 1) * LANES] = jnp.where(keep, pair, zero)
    ik = misc_ref[...][:, :IDX_DIM].astype(BF16)
    ik2_ref[...] = jnp.concatenate([ik, ik], axis=-1)


def _prep_a(z, gq, gk, tm):
    m = z.shape[0]
    blk = COL_AQ // A_WIDTH
    wide = lambda c: pl.BlockSpec((tm, A_WIDTH), lambda i: (i, c))
    row = pl.BlockSpec((tm, A_WIDTH), lambda i: (i, 0))
    return pl.pallas_call(
        _prep_a_kernel,
        grid=(m // tm,),
        in_specs=[wide(blk), wide(blk + 1), wide(blk + 2), wide(blk + 3),
                  pl.BlockSpec((tm, LANES), lambda i: (i, COL_MISC // LANES)),
                  pl.BlockSpec((1, A_HEAD_DIM), lambda i: (0, 0)),
                  pl.BlockSpec((1, A_HEAD_DIM), lambda i: (0, 0))],
        out_specs=[row, row, row, row, row,
                   pl.BlockSpec((tm, IDX_HEADS * LANES), lambda i: (i, 0)),
                   pl.BlockSpec((tm, LANES), lambda i: (i, 0))],
        out_shape=[jax.ShapeDtypeStruct((m, A_WIDTH), BF16),
                   jax.ShapeDtypeStruct((m, A_WIDTH), F32),
                   jax.ShapeDtypeStruct((m, A_WIDTH), BF16),
                   jax.ShapeDtypeStruct((m, A_WIDTH), F32),
                   jax.ShapeDtypeStruct((m, A_WIDTH), BF16),
                   jax.ShapeDtypeStruct((m, IDX_HEADS * LANES), BF16),
                   jax.ShapeDtypeStruct((m, LANES), BF16)],
        compiler_params=_params(1),
        name="prep_a",
    )(z, z, z, z, z, gq, gk)


def _sortable_key(score):
    bits = pltpu.bitcast(score, jnp.int32)
    return bits ^ ((bits >> 31) & jnp.int32(0x7FFFFFFF))


def _topk_threshold(count_ge, rows, topk):
    def body(it, ans):
        cand = ans + lax.shift_left(jnp.int32(1), jnp.int32(31) - it)
        return jnp.where(count_ge(cand) >= float(topk), cand, ans)

    ans = lax.fori_loop(0, 32, body, jnp.full((rows, 1), INT_MIN, jnp.int32))
    return jnp.maximum(ans, jnp.int32(INT_MIN + 1))


def _tie_cut(tied_below, need, rows, n_bits):
    def body(it, cut):
        cand = cut | lax.shift_left(jnp.int32(1), jnp.int32(n_bits - 1) - it)
        return jnp.where(tied_below(cand) < need, cand, cut)

    return lax.fori_loop(0, n_bits, body, jnp.zeros((rows, 1), jnp.int32))


def _a_prompt_kernel(cfar_ref, qn_ref, iqm_ref, mq_ref, kb_ref, vb_ref, ik2_ref, band_ref,
                     o_ref, s_scr, w_scr, acc_scr, m_scr, l_scr, thr_scr, cut_scr, *, topk):
    i = pl.program_id(1)
    tq = qn_ref.shape[1]
    kt = tq
    scale = A_HEAD_DIM ** -0.5
    iw_scale = (IDX_HEADS * IDX_DIM) ** -0.5

    iw = mq_ref[0][:, MISC_IW:MISC_IW + IDX_HEADS] * iw_scale
    for h in range(IDX_HEADS):
        w_scr[h] = jnp.broadcast_to(iw[:, h:h + 1], (tq, LANES))

    def score_tile(j, diagonal):
        k0 = pl.multiple_of(j * kt, kt)
        ik = ik2_ref[0, pl.ds(k0, kt), :]
        acc = jnp.zeros((tq, kt), F32)
        for h in range(IDX_HEADS):
            x = _dot_nt(iqm_ref[0, :, h * LANES:(h + 1) * LANES], ik)
            acc = acc + jnp.maximum(x, 0.0) * jnp.tile(w_scr[h], (1, kt // LANES))
        key = _sortable_key(acc)
        if diagonal:
            r = lax.broadcasted_iota(jnp.int32, (tq, kt), 0)
            c = lax.broadcasted_iota(jnp.int32, (tq, kt), 1)
            key = jnp.where(c <= r, key, jnp.int32(INT_MIN))
        s_scr[:, pl.ds(k0, kt)] = key

    def score_body(j, carry):
        score_tile(j, False)
        return carry

    lax.fori_loop(0, i, score_body, 0)
    score_tile(i, True)

    for rc in range(tq // ROW_CHUNK):
        rows = slice(rc * ROW_CHUNK, (rc + 1) * ROW_CHUNK)

        def count_ge(t, rows=rows):
            tb = jnp.broadcast_to(t, (ROW_CHUNK, LANES))

            def add_tile(j, cnt):
                k0 = pl.multiple_of(j * kt, kt)
                keys = s_scr[rows, pl.ds(k0, kt)]
                for c in range(kt // LANES):
                    cnt = cnt + jnp.where(keys[:, c * LANES:(c + 1) * LANES] >= tb,
                                          jnp.int32(1), jnp.int32(0))
                return cnt

            def add_pair(jj, cnt):
                return add_tile(2 * jj + 1, add_tile(2 * jj, cnt))

            n_pairs = (i + 1) // 2
            cnt = lax.fori_loop(0, n_pairs, add_pair, jnp.zeros((ROW_CHUNK, LANES), jnp.int32))
            cnt = lax.fori_loop(2 * n_pairs, i + 1, add_tile, cnt)
            return jnp.sum(cnt.astype(F32), axis=-1, keepdims=True)

        thr = _topk_threshold(count_ge, ROW_CHUNK, topk)
        thr_scr[rows] = jnp.broadcast_to(thr, (ROW_CHUNK, LANES))
        cut_scr[rows] = jnp.full((ROW_CHUNK, LANES), s_scr.shape[1], jnp.int32)
        n_ge = count_ge(thr)

        @pl.when(jnp.max(n_ge) > float(topk))
        def _(rows=rows, thr=thr, n_ge=n_ge, count_ge=count_ge):
            need = float(topk) - count_ge(thr + 1)
            tb = jnp.broadcast_to(thr, (ROW_CHUNK, LANES))
            lane = lax.broadcasted_iota(jnp.int32, (ROW_CHUNK, LANES), 1)

            def tied_below(col_limit):
                cb = jnp.broadcast_to(col_limit, (ROW_CHUNK, LANES))

                def add_tile(j, cnt):
                    k0 = pl.multiple_of(j * kt, kt)
                    keys = s_scr[rows, pl.ds(k0, kt)]
                    for c in range(kt // LANES):
                        below = jnp.where(lane + (k0 + c * LANES) < cb, jnp.int32(1), jnp.int32(0))
                        cnt = cnt + jnp.where(keys[:, c * LANES:(c + 1) * LANES] == tb, below, jnp.int32(0))
                    return cnt

                cnt = lax.fori_loop(0, i + 1, add_tile, jnp.zeros((ROW_CHUNK, LANES), jnp.int32))
                return jnp.sum(cnt.astype(F32), axis=-1, keepdims=True)

            cut = _tie_cut(tied_below, need, ROW_CHUNK, (s_scr.shape[1] - 1).bit_length())
            cut = jnp.where(n_ge > float(topk), cut, s_scr.shape[1])
            cut_scr[rows] = jnp.broadcast_to(cut, (ROW_CHUNK, LANES))

    m_scr[...] = jnp.full(m_scr.shape, NEG_BIG, F32)
    l_scr[...] = jnp.zeros(l_scr.shape, F32)
    acc_scr[...] = jnp.zeros(acc_scr.shape, F32)
    c_qk = scale * LOG2_E

    def qk(k0, width, h, rows):
        hs = slice(h * A_HEAD_DIM, (h + 1) * A_HEAD_DIM)
        return _dot_nt(qn_ref[0, rows, hs], kb_ref[0, pl.ds(k0, width), hs])

    def mask_of(k0, width, rows):
        return pltpu.bitcast(s_scr[rows, pl.ds(k0, width)], F32)

    def lane_fold(x, op):
        out = x[:, :LANES]
        for c in range(1, x.shape[1] // LANES):
            out = op(out, x[:, c * LANES:(c + 1) * LANES])
        return out

    def max_tile(k0, width, band_off):
        thr_t = jnp.tile(thr_scr[...], (1, width // LANES))
        cut_t = jnp.tile(cut_scr[...], (1, width // LANES))
        keys = s_scr[:, pl.ds(k0, width)]
        col = lax.broadcasted_iota(jnp.int32, (tq, width), 1) + k0
        tied_out = jnp.where(keys == thr_t, jnp.where(col > cut_t, NEG_BIG, 0.0), 0.0)
        mask = jnp.where(keys >= thr_t, tied_out, NEG_BIG)
        s_scr[:, pl.ds(k0, width)] = pltpu.bitcast(mask, jnp.int32)
        for h in range(A_HEADS):
            for rc in range(tq // ATTN_ROWS):
                rows = slice(rc * ATTN_ROWS, (rc + 1) * ATTN_ROWS)
                if band_off is None:
                    top = lane_fold(qk(k0, width, h, rows) + mask_of(k0, width, rows),
                                    jnp.maximum) * c_qk + cfar_ref[h]
                else:
                    top = lane_fold(qk(k0, width, h, rows) * c_qk
                                    + band_ref[h, rows, band_off:band_off + width]
                                    + mask_of(k0, width, rows), jnp.maximum)
                m_scr[h, rows] = jnp.maximum(m_scr[h, rows], top)

    def sum_tile(k0, width, band_off):
        for h in range(A_HEADS):
            hs = slice(h * A_HEAD_DIM, (h + 1) * A_HEAD_DIM)
            for rc in range(tq // ATTN_ROWS):
                rows = slice(rc * ATTN_ROWS, (rc + 1) * ATTN_ROWS)
                if band_off is None:
                    shift = jnp.tile(cfar_ref[h] - m_scr[h, rows], (1, width // LANES))
                else:
                    shift = (band_ref[h, rows, band_off:band_off + width]
                             - jnp.tile(m_scr[h, rows], (1, width // LANES)))
                p = jnp.exp2(qk(k0, width, h, rows) * c_qk + shift + mask_of(k0, width, rows))
                l_scr[h, rows] = l_scr[h, rows] + lane_fold(p, jnp.add)
                acc_scr[rows, hs] = acc_scr[rows, hs] + _dot(p.astype(BF16),
                                                             vb_ref[0, pl.ds(k0, width), hs])

    def over_tiles(tile_fn):
        def far_body(j, carry):
            tile_fn(pl.multiple_of(j * kt, kt), kt, None)
            return carry

        lax.fori_loop(0, jnp.maximum(i - 1, 0), far_body, 0)

        @pl.when(i >= 1)
        def _():
            tile_fn(pl.multiple_of((i - 1) * kt, kt), kt, 0)

        tile_fn(pl.multiple_of(i * kt, kt), kt, kt)

    over_tiles(max_tile)
    for h in range(A_HEADS):
        m_scr[h] = jnp.broadcast_to(jnp.max(m_scr[h], axis=-1, keepdims=True), (tq, LANES))
    over_tiles(sum_tile)
    for h in range(A_HEADS):
        hs = slice(h * A_HEAD_DIM, (h + 1) * A_HEAD_DIM)
        o_ref[0, :, hs] = (acc_scr[:, hs] / jnp.sum(l_scr[h], axis=-1, keepdims=True)).astype(BF16)


def _rel_bucket(dist):
    n = jnp.maximum(dist, 0)
    max_exact = REL_BUCKETS // 2
    nf = jnp.maximum(n, 1).astype(F32)
    large = max_exact + (jnp.log(nf / max_exact) / math.log(REL_MAX_DIST / max_exact)
                         * (REL_BUCKETS - max_exact)).astype(jnp.int32)
    large = jnp.minimum(large, REL_BUCKETS - 1)
    return jnp.where(n < max_exact, n, large)


def _bias_band(rel_table, rows, kt):
    d = jnp.arange(rows)[:, None] + kt - jnp.arange(2 * kt)[None, :]
    hit = _rel_bucket(d)[None, :, :, None] == jnp.arange(REL_BUCKETS)
    return jnp.sum(jnp.where(hit, rel_table.T.astype(F32)[:, None, None, :], 0.0), axis=-1)


def _a_prompt(qn, iqm, z, kb, vb, ik2, band, cfar, batch, seq, tq):
    topk = min(TOPK_MAX, seq // 4)
    r3 = lambda a: a.reshape(batch, seq, a.shape[-1])
    qblk = lambda w: pl.BlockSpec((1, tq, w), lambda b, i: (b, i, 0))
    full = lambda w: pl.BlockSpec((1, seq, w), lambda b, i: (b, 0, 0), pipeline_mode=pl.Buffered(1))
    out = pl.pallas_call(
        functools.partial(_a_prompt_kernel, topk=topk),
        grid=(batch, seq // tq),
        in_specs=[pl.BlockSpec(memory_space=pltpu.SMEM),
                  qblk(A_WIDTH), qblk(IDX_HEADS * LANES),
                  pl.BlockSpec((1, tq, LANES), lambda b, i: (b, i, COL_MISC // LANES)),
                  full(A_WIDTH), full(A_WIDTH), full(LANES),
                  pl.BlockSpec((A_HEADS, tq, 2 * tq), lambda b, i: (0, 0, 0),
                               pipeline_mode=pl.Buffered(1))],
        out_specs=qblk(A_WIDTH),
        scratch_shapes=[pltpu.VMEM((tq, seq), jnp.int32),
                        pltpu.VMEM((IDX_HEADS, tq, LANES), F32),
                        pltpu.VMEM((tq, A_WIDTH), F32),
                        pltpu.VMEM((A_HEADS, tq, LANES), F32),
                        pltpu.VMEM((A_HEADS, tq, LANES), F32),
                        pltpu.VMEM((tq, LANES), jnp.int32),
                        pltpu.VMEM((tq, LANES), jnp.int32)],
        out_shape=jax.ShapeDtypeStruct((batch, seq, A_WIDTH), BF16),
        compiler_params=_params(2),
        name="a_prompt",
    )(cfar, r3(qn), r3(iqm), r3(z), r3(kb), r3(vb), r3(ik2), band)
    return out.reshape(batch * seq, A_WIDTH)


def _a_sample_score_kernel(pt_ref, iq_ref, w_ref, mnew_ref, *rest, pages_per_step, n_pages, topk):
    page_refs = rest[:pages_per_step]
    keys_ref, thr_ref, cut_ref = rest[pages_per_step:]
    g = pl.program_id(1)
    rows = SAMPLE_ROWS
    iq = iq_ref[0]
    w = w_ref[0]

    def head_sum(x):
        return jnp.sum((jnp.maximum(x, 0.0) * w).reshape(IDX_HEADS, rows, PAGE_SIZE), axis=0)

    for p in range(pages_per_step):
        off = pl.multiple_of((g * pages_per_step + p) * PAGE_SIZE, PAGE_SIZE)
        keys_ref[0, :, pl.ds(off, PAGE_SIZE)] = _sortable_key(
            head_sum(_dot(iq, page_refs[p][...].astype(BF16))))

    @pl.when(g == pl.num_programs(1) - 1)
    def _():
        past = n_pages * PAGE_SIZE
        ik_new = mnew_ref[0][:, :IDX_DIM].astype(BF16)
        ik_new = jnp.concatenate([ik_new, jnp.zeros((PAGE_SIZE - rows, IDX_DIM), BF16)], axis=0)
        r = lax.broadcasted_iota(jnp.int32, (rows, PAGE_SIZE), 0)
        c = lax.broadcasted_iota(jnp.int32, (rows, PAGE_SIZE), 1)
        keys_ref[0, :, past:past + PAGE_SIZE] = jnp.where(
            c <= r, _sortable_key(head_sum(_dot_nt(iq, ik_new))), jnp.int32(INT_MIN))

        def count_ge(t):
            def body(j, cnt):
                off = pl.multiple_of(j * PAGE_SIZE, PAGE_SIZE)
                return cnt + jnp.where(keys_ref[0, :, pl.ds(off, PAGE_SIZE)] >= t,
                                       jnp.int32(1), jnp.int32(0))
            cnt = lax.fori_loop(0, n_pages + 1, body, jnp.zeros((rows, PAGE_SIZE), jnp.int32))
            return jnp.sum(cnt.astype(F32), axis=-1, keepdims=True)

        thr = _topk_threshold(count_ge, rows, topk)
        thr_ref[0] = jnp.broadcast_to(thr, (rows, LANES))

        n_cols = (n_pages + 1) * PAGE_SIZE
        n_ge = count_ge(thr)
        need = float(topk) - count_ge(thr + 1)
        lane = lax.broadcasted_iota(jnp.int32, (rows, PAGE_SIZE), 1)

        def tied_below(col_limit):
            def body(j, cnt):
                off = pl.multiple_of(j * PAGE_SIZE, PAGE_SIZE)
                below = jnp.where(lane + off < col_limit, jnp.int32(1), jnp.int32(0))
                return cnt + jnp.where(keys_ref[0, :, pl.ds(off, PAGE_SIZE)] == thr, below, jnp.int32(0))
            cnt = lax.fori_loop(0, n_pages + 1, body, jnp.zeros((rows, PAGE_SIZE), jnp.int32))
            return jnp.sum(cnt.astype(F32), axis=-1, keepdims=True)

        cut = _tie_cut(tied_below, need, rows, (n_cols - 1).bit_length())
        cut_ref[0] = jnp.broadcast_to(jnp.where(n_ge > float(topk), cut, n_cols), (rows, LANES))


def _a_sample_scores(page_table, iq_hq, w_hq, z3, cache_kidx, layer, n_tok, pages_per_step):
    batch, n_pages = page_table.shape
    rows = SAMPLE_ROWS
    nk = (n_pages + 1) * PAGE_SIZE
    topk = min(TOPK_MAX, (n_pages * PAGE_SIZE + n_tok) // 4)

    def page_spec(p):
        return pl.BlockSpec((None, None, IDX_DIM, PAGE_SIZE),
                            lambda b, g, pt: (layer, pt[b, g * pages_per_step + p], 0, 0))

    return pl.pallas_call(
        functools.partial(_a_sample_score_kernel, pages_per_step=pages_per_step, n_pages=n_pages,
                          topk=topk),
        grid_spec=pltpu.PrefetchScalarGridSpec(
            num_scalar_prefetch=1,
            grid=(batch, n_pages // pages_per_step),
            in_specs=[pl.BlockSpec((1, IDX_HEADS * rows, IDX_DIM), lambda b, g, pt: (b, 0, 0)),
                      pl.BlockSpec((1, IDX_HEADS * rows, 1), lambda b, g, pt: (b, 0, 0)),
                      pl.BlockSpec((1, rows, LANES), lambda b, g, pt: (b, 0, COL_MISC // LANES))]
                     + [page_spec(p) for p in range(pages_per_step)],
            out_specs=[pl.BlockSpec((1, rows, nk), lambda b, g, pt: (b, 0, 0)),
                       pl.BlockSpec((1, rows, LANES), lambda b, g, pt: (b, 0, 0)),
                       pl.BlockSpec((1, rows, LANES), lambda b, g, pt: (b, 0, 0))]),
        out_shape=[jax.ShapeDtypeStruct((batch, rows, nk), jnp.int32),
                   jax.ShapeDtypeStruct((batch, rows, LANES), jnp.int32),
                   jax.ShapeDtypeStruct((batch, rows, LANES), jnp.int32)],
        compiler_params=_params(2),
        name="a_sample_scores",
    )(page_table, iq_hq, w_hq, z3, *([cache_kidx.transpose(0, 1, 3, 2)] * pages_per_step))


def _a_sample_attn_kernel(pt_ref, qall_ref, keys_ref, thr_ref, cut_ref, knew_ref, vnew_ref, bias_far_ref, bias_last_ref,
                          bias_new_ref, expand_ref, *rest, pages_per_step, n_pages):
    k_refs = rest[:pages_per_step]
    v_refs = rest[pages_per_step:2 * pages_per_step]
    o_ref, acc_scr, m_scr, l_scr, lg_scr = rest[2 * pages_per_step:]
    g = pl.program_id(1)
    last_step = pl.num_programs(1) - 1
    rows = SAMPLE_ROWS
    hq = A_HEADS * rows
    cols = PAGE_SIZE * A_HEADS
    scale = A_HEAD_DIM ** -0.5
    thr = thr_ref[0]
    cut = cut_ref[0]
    lane = lax.broadcasted_iota(jnp.int32, (rows, PAGE_SIZE), 1)
    qall = qall_ref[0]

    @pl.when(g == 0)
    def _():
        m_scr[...] = jnp.full(m_scr.shape, NEG_BIG, F32)
        l_scr[...] = jnp.zeros(l_scr.shape, F32)
        acc_scr[...] = jnp.zeros(acc_scr.shape, F32)

    def masked_logits(key_tile, first_col, expand, kf, bias):
        tied_in = jnp.where(key_tile == thr, jnp.where(lane + first_col > cut, 0.0, 1.0), 1.0)
        hit = jnp.where(key_tile >= thr, tied_in, 0.0).astype(BF16)
        drop = (_dot(hit, expand) - 1.0) * (-NEG_BIG)
        return _dot_nt(qall, kf) * scale + bias + jnp.concatenate([drop] * A_HEADS, axis=0)

    def lane_fold(x, op):
        out = x[:, :LANES]
        for c in range(1, x.shape[1] // LANES):
            out = op(out, x[:, c * LANES:(c + 1) * LANES])
        return out

    def update(n_tiles, width, v_of):
        top = lane_fold(lg_scr[:, :n_tiles * width], jnp.maximum)
        m_old = m_scr[...]
        m_new = jnp.maximum(m_old, jnp.max(top, axis=-1, keepdims=True))
        alpha = jnp.exp(m_old - m_new)
        part = jnp.zeros((hq, LANES), F32)
        acc = alpha * acc_scr[...]
        for t in range(n_tiles):
            p = jnp.exp(lg_scr[:, t * width:(t + 1) * width] - m_new)
            part = part + lane_fold(p, jnp.add)
            acc = acc + _dot(p.astype(BF16), v_of(t))
        l_scr[...] = alpha * l_scr[...] + jnp.sum(part, axis=-1, keepdims=True)
        acc_scr[...] = acc
        m_scr[...] = m_new

    for p in range(pages_per_step):
        page = g * pages_per_step + p
        off = pl.multiple_of(page * PAGE_SIZE, PAGE_SIZE)
        bias = bias_far_ref[...]
        if p == pages_per_step - 1:
            bias = jnp.where(g == last_step, bias_last_ref[...], bias)
        lg_scr[:, p * cols:(p + 1) * cols] = masked_logits(
            keys_ref[0, :, pl.ds(off, PAGE_SIZE)], off, expand_ref[...],
            k_refs[p][...].reshape(cols, A_HEAD_DIM).astype(BF16), bias)
    update(pages_per_step, cols, lambda t: v_refs[t][...].reshape(cols, A_HEAD_DIM).astype(BF16))

    @pl.when(g == last_step)
    def _():
        past = n_pages * PAGE_SIZE
        lg_scr[:, :hq] = masked_logits(keys_ref[0, :, past:past + PAGE_SIZE], past, expand_ref[:, :hq],
                                       knew_ref[0], bias_new_ref[...])
        update(1, hq, lambda t: vnew_ref[0])
        o_ref[0] = (acc_scr[...] / l_scr[...]).astype(BF16)


def _a_sample_attn(page_table, qall, keys, thr, cut, knew, vnew, bias_far, bias_last, bias_new, expand,
                   cache_k, cache_v, layer, pages_per_step):
    batch, n_pages = page_table.shape
    rows = SAMPLE_ROWS
    hq = A_HEADS * rows
    nk = keys.shape[-1]

    def page_spec(p):
        return pl.BlockSpec((None, None, PAGE_SIZE, A_HEADS, A_HEAD_DIM),
                            lambda b, g, pt: (layer, pt[b, g * pages_per_step + p], 0, 0, 0))

    per_b = lambda r, w: pl.BlockSpec((1, r, w), lambda b, g, pt: (b, 0, 0))
    const = lambda r, w: pl.BlockSpec((r, w), lambda b, g, pt: (0, 0))
    return pl.pallas_call(
        functools.partial(_a_sample_attn_kernel, pages_per_step=pages_per_step, n_pages=n_pages),
        grid_spec=pltpu.PrefetchScalarGridSpec(
            num_scalar_prefetch=1,
            grid=(batch, n_pages // pages_per_step),
            in_specs=[per_b(hq, A_HEAD_DIM), per_b(rows, nk), per_b(rows, LANES), per_b(rows, LANES),
                      per_b(hq, A_HEAD_DIM), per_b(hq, A_HEAD_DIM),
                      const(hq, PAGE_SIZE * A_HEADS), const(hq, PAGE_SIZE * A_HEADS), const(hq, hq),
                      const(PAGE_SIZE, PAGE_SIZE * A_HEADS)]
                     + [page_spec(p) for p in range(pages_per_step)] * 2,
            out_specs=per_b(hq, A_HEAD_DIM),
            scratch_shapes=[pltpu.VMEM((hq, A_HEAD_DIM), F32),
                            pltpu.VMEM((hq, 1), F32),
                            pltpu.VMEM((hq, 1), F32),
                            pltpu.VMEM((hq, pages_per_step * PAGE_SIZE * A_HEADS), F32)]),
        out_shape=jax.ShapeDtypeStruct((batch, hq, A_HEAD_DIM), BF16),
        compiler_params=_params(2),
        name="a_sample_attn",
    )(page_table, qall, keys, thr, cut, knew, vnew, bias_far, bias_last, bias_new, expand,
      *([cache_k] * pages_per_step), *([cache_v] * pages_per_step))


def _pad_rows(x, rows):
    if x.shape[0] == rows:
        return x
    return jnp.concatenate([x, jnp.zeros((rows - x.shape[0], x.shape[1]), x.dtype)], axis=0)


def _split3(x):
    hi = x.astype(BF16)
    r1 = x - hi.astype(F32)
    mid = r1.astype(BF16)
    lo = (r1 - mid.astype(F32)).astype(BF16)
    return hi, mid, lo


def _gla_kernel(gq_ref, gk_ref, gv_ref, gg_ref, misc_ref, wa_ref, ba_ref, gn_ref, s0_ref,
                o_ref, sfin_ref, st_scr, oacc_scr, b_scr, q_scr, *, chunk, rows, n_valid):
    tt = gq_ref.shape[1]

    @pl.when(pl.program_id(1) == 0)
    def _():
        st_scr[...] = s0_ref[0]

    ga = _pad_rows(misc_ref[0][:, MISC_GA:MISC_GA + GLA_RANK], rows)
    x = _dot(ga.astype(BF16), wa_ref[...]) + ba_ref[...]
    la = (jnp.minimum(x, 0.0) - jnp.log1p(jnp.exp(-jnp.abs(x)))) * (1.0 / GLA_TAU)
    r = lax.broadcasted_iota(jnp.int32, (rows, rows), 0)
    c = lax.broadcasted_iota(jnp.int32, (rows, rows), 1)
    if n_valid < rows:
        rr = lax.broadcasted_iota(jnp.int32, la.shape, 0)
        la = jnp.where(rr < n_valid, la, 0.0)
    tri = jnp.where((c <= r) & (r // chunk == c // chunk), 1.0, 0.0).astype(BF16)
    hi, mid, lo = _split3(la)
    b = _dot(tri, hi) + _dot(tri, mid) + _dot(tri, lo)

    gq = _pad_rows(gq_ref[0], rows) * (GLA_DK ** -0.5)
    gk = _pad_rows(gk_ref[0], rows)
    if n_valid < rows:
        gk = jnp.where(rr < n_valid, gk, 0.0)
    gv = _pad_rows(gv_ref[0], rows)
    causal = (lax.broadcasted_iota(jnp.int32, (chunk, chunk), 1)
              <= lax.broadcasted_iota(jnp.int32, (chunk, chunk), 0))
    n_chunks = rows // chunk
    mid = chunk // 2 - 1

    def pieces(ci, h):
        return (slice(ci * chunk, (ci + 1) * chunk), slice(h * GLA_DK, (h + 1) * GLA_DK),
                slice(h * GLA_DV, (h + 1) * GLA_DV))

    spread = jnp.zeros((1, b.shape[1]), F32)
    for ci in range(n_chunks):
        r0 = ci * chunk
        spread = jnp.maximum(spread, jnp.maximum(b[r0:r0 + 1, :] - b[r0 + mid:r0 + mid + 1, :],
                                                 b[r0 + mid:r0 + mid + 1, :] - b[r0 + chunk - 1:r0 + chunk, :]))
    factored_ok = jnp.max(spread) <= GLA_MAX_FACTORED_EXPONENT

    @pl.when(factored_ok)
    def _():
        for ci in range(n_chunks):
            for h in range(GLA_HEADS):
                rs, ks, vs = pieces(ci, h)
                bc = b[rs, ks]
                bmid = bc[mid:mid + 1, :]
                qm = (gq[rs, ks] * jnp.exp(bc - bmid)).astype(BF16)
                km = (gk[rs, ks] * jnp.exp(bmid - bc)).astype(BF16)
                sc = jnp.where(causal, _dot_nt(qm, km), 0.0)
                oacc_scr[rs, vs] = _dot(sc.astype(BF16), gv[rs, vs].astype(BF16))

    @pl.when(jnp.logical_not(factored_ok))
    def _():
        b_scr[...] = b
        q_scr[...] = gq
        s_idx = lax.broadcasted_iota(jnp.int32, (chunk, 1), 0)
        for ci in range(n_chunks):
            for h in range(GLA_HEADS):
                rs, ks, vs = pieces(ci, h)
                bc = b[rs, ks]
                kc = gk[rs, ks]
                vc = gv[rs, vs]

                def row_group(g, carry, ci=ci, ks=ks, vs=vs, bc=bc, kc=kc, vc=vc):
                    base = pl.multiple_of(ci * chunk + g * SUBLANES, SUBLANES)
                    b_rows = b_scr[pl.ds(base, SUBLANES), :][:, ks]
                    q_rows = q_scr[pl.ds(base, SUBLANES), :][:, ks]
                    outs = []
                    for j in range(SUBLANES):
                        decay = jnp.exp(jnp.minimum(b_rows[j:j + 1] - bc, 0.0))
                        wgt = jnp.sum(kc * decay * q_rows[j:j + 1], axis=-1, keepdims=True)
                        wgt = jnp.where(s_idx <= g * SUBLANES + j, wgt, 0.0)
                        outs.append(jnp.sum(wgt * vc, axis=0, keepdims=True))
                    oacc_scr[pl.ds(base, SUBLANES), vs] = jnp.concatenate(outs, axis=0)
                    return carry

                lax.fori_loop(0, chunk // SUBLANES, row_group, 0)

    for ci in range(n_chunks):
        for h in range(GLA_HEADS):
            rs, ks, vs = pieces(ci, h)
            bc = b[rs, ks]
            bend = bc[chunk - 1:chunk, :]
            qt = (gq[rs, ks] * jnp.exp(bc)).astype(BF16)
            kd = (gk[rs, ks] * jnp.exp(bend - bc)).astype(BF16)
            st = st_scr[h]
            oacc_scr[rs, vs] = oacc_scr[rs, vs] + _dot_nt(qt, st.astype(BF16))
            st_scr[h] = st * jnp.exp(bend) + _dot_tn(gv[rs, vs].astype(BF16), kd)

    gg = gg_ref[0]
    for h in range(GLA_HEADS):
        vs = slice(h * GLA_DV, (h + 1) * GLA_DV)
        oh = oacc_scr[0:tt, vs]
        ms = jnp.mean(oh * oh, axis=-1, keepdims=True)
        g = gg[:, vs]
        o_ref[0, :, vs] = (oh * lax.rsqrt(ms + EPS) * gn_ref[...] * (g * jax.nn.sigmoid(g))).astype(BF16)

    @pl.when(pl.program_id(1) == pl.num_programs(1) - 1)
    def _():
        sfin_ref[0] = st_scr[...]


def _gla(z3, wa, ba, gn, s0t, tt, chunk, rows, n_valid):
    batch, seq, _ = z3.shape
    q_blk = COL_GQ // 256
    v_blk = COL_GV // GLA_WIDTH
    tok = lambda w, cblk: pl.BlockSpec((1, tt, w), lambda b, t: (b, t, cblk))
    const = lambda shape: pl.BlockSpec(shape, lambda b, t: (0,) * len(shape))
    st_spec = pl.BlockSpec((1, GLA_HEADS, GLA_DV, GLA_DK), lambda b, t: (b, 0, 0, 0))
    return pl.pallas_call(
        functools.partial(_gla_kernel, chunk=chunk, rows=rows, n_valid=n_valid),
        grid=(batch, seq // tt),
        in_specs=[tok(256, q_blk), tok(256, q_blk + 1), tok(GLA_WIDTH, v_blk), tok(GLA_WIDTH, v_blk + 1),
                  tok(LANES, COL_MISC // LANES),
                  const((GLA_RANK, GLA_HEADS * GLA_DK)), const((1, GLA_HEADS * GLA_DK)),
                  const((1, GLA_DV)), st_spec],
        out_specs=[pl.BlockSpec((1, tt, GLA_WIDTH), lambda b, t: (b, t, 0)), st_spec],
        out_shape=[jax.ShapeDtypeStruct((batch, seq, GLA_WIDTH), BF16),
                   jax.ShapeDtypeStruct((batch, GLA_HEADS, GLA_DV, GLA_DK), F32)],
        scratch_shapes=[pltpu.VMEM((GLA_HEADS, GLA_DV, GLA_DK), F32),
                        pltpu.VMEM((rows, GLA_WIDTH), F32),
                        pltpu.VMEM((rows, GLA_HEADS * GLA_DK), F32),
                        pltpu.VMEM((rows, GLA_HEADS * GLA_DK), F32)],
        compiler_params=_params(2),
        name="gla",
    )(z3, z3, z3, z3, z3, wa, ba, gn, s0t)


def _ret_kernel(rq_ref, rk_ref, rv_ref, rg_ref, cos_ref, sin_ref, gn_ref, s0_ref,
                o_ref, sfin_ref, st_scr, oacc_scr, *, rows, n_valid, log_gamma):
    tt = rq_ref.shape[1]
    width = RET_HEADS * RET_DK

    @pl.when(pl.program_id(1) == 0)
    def _():
        st_scr[...] = s0_ref[0]

    lane = lax.broadcasted_iota(jnp.int32, (rows, width), 1)
    first_half = (lane % RET_DK) < (RET_DK // 2)
    cos = _pad_rows(cos_ref[...], rows)
    sin = _pad_rows(sin_ref[...], rows)

    def rope(x):
        rot = jnp.where(first_half, pltpu.roll(x, width - RET_DK // 2, axis=1),
                        pltpu.roll(x, RET_DK // 2, axis=1))
        return x * cos + rot * sin

    q = rope(_pad_rows(rq_ref[0], rows))
    k = rope(_pad_rows(rk_ref[0], rows)) * (RET_DK ** -0.5)
    v = _pad_rows(rv_ref[0], rows)
    rr = lax.broadcasted_iota(jnp.int32, (rows, RET_DK), 0)
    if n_valid < rows:
        k = jnp.where(lax.broadcasted_iota(jnp.int32, k.shape, 0) < n_valid, k, 0.0)
    t_i = lax.broadcasted_iota(jnp.int32, (rows, rows), 0)
    s_i = lax.broadcasted_iota(jnp.int32, (rows, rows), 1)
    dist = (t_i - s_i).astype(F32)
    pos1 = (rr + 1).astype(F32)
    rem = (n_valid - 1 - rr).astype(F32)

    for h in range(RET_HEADS):
        ks = slice(h * RET_DK, (h + 1) * RET_DK)
        vs = slice(h * RET_DV, (h + 1) * RET_DV)
        lg = log_gamma[h]
        decay = jnp.where(t_i >= s_i, jnp.exp(dist * lg), 0.0)
        qh = q[:, ks]
        kh = k[:, ks]
        vh = v[:, vs].astype(BF16)
        st = st_scr[h]
        sc = _dot_nt(qh.astype(BF16), kh.astype(BF16)) * decay
        q_in = (qh * jnp.exp(pos1 * lg)).astype(BF16)
        oacc_scr[:, vs] = _dot(sc.astype(BF16), vh) + _dot_nt(q_in, st.astype(BF16))
        kd = (kh * jnp.exp(rem * lg)).astype(BF16)
        st_scr[h] = st * math.exp(n_valid * lg) + _dot_tn(vh, kd)

    rg = rg_ref[0]
    for h in range(RET_HEADS):
        vs = slice(h * RET_DV, (h + 1) * RET_DV)
        oh = oacc_scr[0:tt, vs]
        oc = oh - jnp.mean(oh, axis=-1, keepdims=True)
        var = jnp.mean(oc * oc, axis=-1, keepdims=True)
        g = rg[:, vs]
        o_ref[0, :, vs] = (oc * lax.rsqrt(var + EPS) * gn_ref[...] * (g * jax.nn.sigmoid(g))).astype(BF16)

    @pl.when(pl.program_id(1) == pl.num_programs(1) - 1)
    def _():
        sfin_ref[0] = st_scr[...]


def _ret(z3, cos, sin, gn, s0t, tt, rows, n_valid):
    batch, seq, _ = z3.shape
    q_blk = COL_GQ // 256 + 2
    v_blk = COL_GV // RET_WIDTH + 2
    width = RET_HEADS * RET_DK
    log_gamma = tuple(float(np.log1p(-np.exp2(np.float32(-5.0 - h)), dtype=np.float32))
                      for h in range(RET_HEADS))
    tok = lambda w, cblk: pl.BlockSpec((1, tt, w), lambda b, t: (b, t, cblk))
    st_spec = pl.BlockSpec((1, RET_HEADS, RET_DV, RET_DK), lambda b, t: (b, 0, 0, 0))
    return pl.pallas_call(
        functools.partial(_ret_kernel, rows=rows, n_valid=n_valid, log_gamma=log_gamma),
        grid=(batch, seq // tt),
        in_specs=[tok(256, q_blk), tok(256, q_blk + 1), tok(RET_WIDTH, v_blk), tok(RET_WIDTH, v_blk + 1),
                  pl.BlockSpec((tt, width), lambda b, t: (t, 0)),
                  pl.BlockSpec((tt, width), lambda b, t: (t, 0)),
                  pl.BlockSpec((1, RET_DV), lambda b, t: (0, 0)), st_spec],
        out_specs=[pl.BlockSpec((1, tt, RET_WIDTH), lambda b, t: (b, t, 0)), st_spec],
        out_shape=[jax.ShapeDtypeStruct((batch, seq, RET_WIDTH), BF16),
                   jax.ShapeDtypeStruct((batch, RET_HEADS, RET_DV, RET_DK), F32)],
        scratch_shapes=[pltpu.VMEM((RET_HEADS, RET_DV, RET_DK), F32),
                        pltpu.VMEM((rows, RET_WIDTH), F32)],
        compiler_params=_params(2),
        name="ret",
    )(z3, z3, z3, z3, cos, sin, gn, s0t)


def _rope_tables(pos):
    half = RET_DK // 2
    freqs = ROPE_BASE ** (-jnp.arange(half, dtype=F32) / half)
    ang = pos.astype(F32)[:, None] * freqs[None, :]
    cos = jnp.cos(ang)
    sin = jnp.sin(ang)
    cos_t = jnp.tile(jnp.concatenate([cos, cos], axis=-1), (1, RET_HEADS))
    sin_t = jnp.tile(jnp.concatenate([-sin, sin], axis=-1), (1, RET_HEADS))
    return cos_t, sin_t


def _permuted_segments():
    names = ("aq", "ak", "av", "iq", "ik", "iw", "gq", "gk", "gv", "ga", "gg", "rq", "rk", "rv", "rg", "gates")
    src, acc = {}, 0
    for name, size in zip(names, IN_SIZES):
        src[name] = (acc, size)
        acc += size
    order = ("gates", "aq", "ak", "av", "iq", "gv", "gg", "rv", "rg", "gq", "gk", "rq", "rk", "ik", "iw", "ga")
    out, dst = [], 0
    for name in order:
        out.append((src[name][0], dst, src[name][1]))
        dst += src[name][1]
    return out, dst


def _permute_w_in_kernel(w_ref, o_ref):
    segments, used = _permuted_segments()
    for src, dst, size in segments:
        o_ref[dst:dst + size, :] = w_ref[src:src + size, :].astype(BF16)
    o_ref[used:, :] = jnp.zeros((IN_PADDED - used, o_ref.shape[1]), BF16)


def _permute_w_in(w_in_t, layer):
    _, n, d = w_in_t.shape
    tc = LANES
    return pl.pallas_call(
        _permute_w_in_kernel,
        grid=(d // tc,),
        in_specs=[pl.BlockSpec((None, n, tc), lambda j: (layer, 0, j))],
        out_specs=pl.BlockSpec((IN_PADDED, tc), lambda j: (0, j)),
        out_shape=jax.ShapeDtypeStruct((IN_PADDED, d), BF16),
        compiler_params=_params(1),
        name="permute_w_in",
    )(w_in_t)


def _dense_tail(x, z, oa, ob, oc, w):
    m = x.shape[0]
    layer = w["layer"]
    merged = _merge(oa, ob, oc, z, w["w_branch"], layer, min(m, 256), D_MODEL)
    x = _matmul_residual(merged, w["w_out"], layer, x, min(m, 512), D_MODEL, "out_proj")
    act = _norm_swiglu(x, w["norm_ffn"], w["w_ffn_in"], layer, min(m, 1024), 512)
    return _matmul_residual(act, w["w_ffn_out"], layer, x, min(m, 1024), 512, "ffn_out")


def _prompt_layer(x, w, batch, seq, band, cfar, cos, sin):
    tm = 512
    z = _norm_matmul(x, w["norm_mix"], w["w_in"], tm, IN_PROJ_COLS)
    qn, kf, kb, vf, vb, iqm, ik2 = _prep_a(z, w["a_q_norm"], w["a_k_norm"], tm)
    oa = _a_prompt(qn, iqm, z, kb, vb, ik2, band * LOG2_E, cfar * LOG2_E, batch, seq, 256)
    z3 = z.reshape(batch, seq, IN_PADDED)
    zero_state = jnp.zeros((batch, GLA_HEADS, GLA_DV, GLA_DK), F32)
    ob, s_gla = _gla(z3, w["gla_wa"], w["gla_ba"], w["gla_norm"], zero_state, 256, CHUNK, 256, 256)
    oc, s_ret = _ret(z3, cos, sin, w["ret_norm"], zero_state, 256, 256, 256)
    x = _dense_tail(x, z, oa, ob.reshape(-1, GLA_WIDTH), oc.reshape(-1, RET_WIDTH), w)
    kidx = z3[:, :, COL_MISC:COL_MISC + IDX_DIM]
    new = (kf.reshape(batch, seq, A_HEADS, A_HEAD_DIM), vf.reshape(batch, seq, A_HEADS, A_HEAD_DIM),
           kidx, s_gla.transpose(0, 1, 3, 2), s_ret.transpose(0, 1, 3, 2))
    return x, new


def _sample_layer(x, w, layer, batch, n_tok, page_table, cache_k, cache_v, cache_kidx,
                  s_gla0, s_ret0, bias_s, expand, cos, sin):
    rows = SAMPLE_ROWS
    hq = A_HEADS * rows
    tm = batch * rows
    z = _norm_matmul(x, w["norm_mix"], w["w_in"], tm, IN_PROJ_COLS)
    qn, kf, kb, vf, vb, _, _ = _prep_a(z, w["a_q_norm"], w["a_k_norm"], tm)
    z3 = z.reshape(batch, rows, IN_PADDED)
    iq = z3[:, :, COL_AQ + 3 * A_WIDTH:COL_AQ + 4 * A_WIDTH].reshape(batch, rows, IDX_HEADS, IDX_DIM)
    iq_hq = iq.transpose(0, 2, 1, 3).reshape(batch, IDX_HEADS * rows, IDX_DIM).astype(BF16)
    iw = z3[:, :, COL_MISC + MISC_IW:COL_MISC + MISC_IW + IDX_HEADS] * ((IDX_HEADS * IDX_DIM) ** -0.5)
    w_hq = iw.transpose(0, 2, 1).reshape(batch, IDX_HEADS * rows, 1)
    keys, thr, cut = _a_sample_scores(page_table, iq_hq, w_hq, z3, cache_kidx, layer, n_tok, 32)
    qall = qn.reshape(batch, rows, A_HEADS, A_HEAD_DIM).transpose(0, 2, 1, 3).reshape(batch, hq, A_HEAD_DIM)
    oa = _a_sample_attn(page_table, qall, keys, thr, cut, kb.reshape(batch, hq, A_HEAD_DIM),
                        vb.reshape(batch, hq, A_HEAD_DIM), *bias_s, expand, cache_k, cache_v, layer, 8)
    oa = oa.reshape(batch, A_HEADS, rows, A_HEAD_DIM).transpose(0, 2, 1, 3).reshape(tm, A_WIDTH)
    ob, s_gla = _gla(z3, w["gla_wa"], w["gla_ba"], w["gla_norm"], s_gla0.transpose(0, 1, 3, 2),
                     rows, LANES, LANES, n_tok)
    oc, s_ret = _ret(z3, cos, sin, w["ret_norm"], s_ret0.transpose(0, 1, 3, 2), rows, LANES, n_tok)
    x = _dense_tail(x, z, oa, ob.reshape(tm, GLA_WIDTH), oc.reshape(tm, RET_WIDTH), w)
    kidx = z3[:, :n_tok, COL_MISC:COL_MISC + IDX_DIM]
    new = (kf.reshape(batch, rows, A_HEADS, A_HEAD_DIM)[:, :n_tok],
           vf.reshape(batch, rows, A_HEADS, A_HEAD_DIM)[:, :n_tok],
           kidx, s_gla.transpose(0, 1, 3, 2), s_ret.transpose(0, 1, 3, 2))
    return x, new


def _sample_bias(rel_table):
    rows = SAMPLE_ROWS
    hq = A_HEADS * rows
    band = _bias_band(rel_table, rows, PAGE_SIZE)
    cfar_rows = jnp.repeat(rel_table[REL_BUCKETS - 1].astype(F32), rows)[:, None]
    band_last = jnp.repeat(band[:, :, :PAGE_SIZE].reshape(hq, PAGE_SIZE), A_HEADS, axis=1)
    band_new = jnp.repeat(band[:, :, PAGE_SIZE:PAGE_SIZE + rows].reshape(hq, rows), A_HEADS, axis=1)
    cols = PAGE_SIZE * A_HEADS
    other_head = (jnp.arange(hq)[:, None] // rows) != (jnp.arange(cols)[None, :] % A_HEADS)
    head_mask = jnp.where(other_head, NEG_BIG, 0.0).astype(F32)
    return cfar_rows + head_mask, band_last + head_mask, band_new + head_mask[:, :hq]


def kernel(x_prompt, x_sample, cache_k, cache_v, cache_kidx, state_gla, state_ret, page_table, rel_table, w_in, a_q_norm, a_k_norm, gla_wa, gla_ba, gla_norm, ret_norm, w_branch, w_out, norm_mix, norm_ffn, w_ffn_in, w_ffn_out):
    depth = w_in.shape[0]
    bp, tp, d = x_prompt.shape
    bs, ts, _ = x_sample.shape
    past = page_table.shape[1] * PAGE_SIZE
    tq = 256
    row = lambda a: a.reshape(1, -1).astype(F32)

    cfar = rel_table[REL_BUCKETS - 1].astype(F32)
    band_p = _bias_band(rel_table, tq, tq)
    bias_s = _sample_bias(rel_table)
    expand = jnp.asarray(np.kron(np.eye(PAGE_SIZE), np.ones((1, A_HEADS))), dtype=BF16)
    cos_p, sin_p = _rope_tables(jnp.arange(tp))
    cos_s, sin_s = _rope_tables(past + jnp.arange(SAMPLE_ROWS))

    w_in_t = jnp.swapaxes(w_in, 1, 2)
    xp = x_prompt.reshape(bp * tp, d)
    xs = jnp.pad(x_sample, ((0, 0), (0, SAMPLE_ROWS - ts), (0, 0))).reshape(bs * SAMPLE_ROWS, d)
    rows_p, rows_s = [], []
    for l in range(depth):
        w = dict(layer=l, w_in=_permute_w_in(w_in_t, l), a_q_norm=row(a_q_norm[l]), a_k_norm=row(a_k_norm[l]),
                 gla_wa=gla_wa[l].astype(BF16), gla_ba=row(gla_ba[l]), gla_norm=row(gla_norm[l]),
                 ret_norm=row(ret_norm[l]), w_branch=w_branch, w_out=w_out,
                 norm_mix=row(norm_mix[l]), norm_ffn=row(norm_ffn[l]), w_ffn_in=w_ffn_in,
                 w_ffn_out=w_ffn_out)
        xp, new_p = _prompt_layer(xp, w, bp, tp, band_p, cfar, cos_p, sin_p)
        xs, new_s = _sample_layer(xs, w, l, bs, ts, page_table, cache_k, cache_v, cache_kidx,
                                  state_gla[l], state_ret[l], bias_s, expand, cos_s, sin_s)
        rows_p.append(new_p)
        rows_s.append(new_s)
    outs_p = [jnp.stack(r) for r in zip(*rows_p)]
    outs_s = [jnp.stack(r) for r in zip(*rows_s)]
    y_p = xp.reshape(bp, tp, d)
    y_s = xs.reshape(bs, SAMPLE_ROWS, d)[:, :ts]
    return (y_p, y_s, *outs_p, *outs_s)
```

```python
import functools
import math

import numpy as np
import jax
import jax.numpy as jnp
from jax import lax
from jax.experimental import pallas as pl
from jax.experimental.pallas import tpu as pltpu

D_MODEL = 2048
PAGE_SIZE = 128
A_HEADS = 8
A_HEAD_DIM = 128
A_WIDTH = A_HEADS * A_HEAD_DIM
IDX_HEADS = 16
IDX_DIM = 64
TOPK_MAX = 256
REL_BUCKETS = 32
REL_MAX_DIST = 128
GLA_HEADS = 4
GLA_DK = 64
GLA_DV = 128
GLA_WIDTH = GLA_HEADS * GLA_DV
GLA_RANK = 16
GLA_TAU = 16.0
RET_HEADS = 4
RET_DK = 64
RET_DV = 128
RET_WIDTH = RET_HEADS * RET_DV
ROPE_BASE = 10000.0
CHUNK = 64
MIX_WIDTH = A_WIDTH + GLA_WIDTH + RET_WIDTH
D_FF = -(-8 * D_MODEL // (3 * 256)) * 256
EPS = 1e-6
IN_SIZES = (A_WIDTH, A_WIDTH, A_WIDTH, IDX_HEADS * IDX_DIM, IDX_DIM, IDX_HEADS,
            GLA_HEADS * GLA_DK, GLA_HEADS * GLA_DK, GLA_WIDTH, GLA_RANK, GLA_WIDTH,
            RET_HEADS * RET_DK, RET_HEADS * RET_DK, RET_WIDTH, RET_WIDTH, 3 * D_MODEL)

COL_GATES = 0
COL_AQ = 3 * D_MODEL
COL_GV = COL_AQ + 4 * A_WIDTH
COL_GQ = COL_GV + 4 * GLA_WIDTH
COL_MISC = COL_GQ + 4 * 256
IN_PADDED = COL_MISC + 128
IN_PROJ_COLS = IN_PADDED // 5
MISC_IW = IDX_DIM
MISC_GA = IDX_DIM + IDX_HEADS

LANES = 128
SUBLANES = 8
SAMPLE_ROWS = 16
VMEM_LIMIT = 56 * 1024 * 1024
INT_MIN = -2147483648
NEG_BIG = -1e30
LOG2_E = math.log2(math.e)
GLA_MAX_FACTORED_EXPONENT = 60.0
ROW_CHUNK = 128
ATTN_ROWS = 256

BF16 = jnp.bfloat16
F32 = jnp.float32
NT_DIMS = (((1,), (1,)), ((), ()))
TN_DIMS = (((0,), (0,)), ((), ()))


def _params(n_axes):
    return pltpu.CompilerParams(dimension_semantics=("arbitrary",) * n_axes,
                                vmem_limit_bytes=VMEM_LIMIT)


def _resident(constant_index):
    return pl.Buffered(1) if constant_index else None


def _dot(a, b):
    return jnp.dot(a, b, preferred_element_type=F32)


def _dot_nt(a, b):
    return lax.dot_general(a, b, NT_DIMS, preferred_element_type=F32)


def _dot_tn(a, b):
    return lax.dot_general(a, b, TN_DIMS, preferred_element_type=F32)


def _norm_matmul_kernel(x_ref, g_ref, w_ref, o_ref, hb_ref):
    @pl.when(pl.program_id(1) == 0)
    def _():
        x = x_ref[...]
        ms = jnp.mean(x * x, axis=-1, keepdims=True)
        hb_ref[...] = (x * lax.rsqrt(ms + EPS) * g_ref[...]).astype(BF16)

    o_ref[...] = _dot_nt(hb_ref[...], w_ref[...])


def _norm_matmul(x, g, w, tm, tn):
    m, d = x.shape
    n = w.shape[0]
    return pl.pallas_call(
        _norm_matmul_kernel,
        grid=(m // tm, n // tn),
        in_specs=[pl.BlockSpec((tm, d), lambda i, j: (i, 0)),
                  pl.BlockSpec((1, d), lambda i, j: (0, 0)),
                  pl.BlockSpec((tn, d), lambda i, j: (j, 0))],
        out_specs=pl.BlockSpec((tm, tn), lambda i, j: (i, j)),
        out_shape=jax.ShapeDtypeStruct((m, n), F32),
        scratch_shapes=[pltpu.VMEM((tm, d), BF16)],
        compiler_params=_params(2),
        name="in_proj",
    )(x, g, w)


def _norm_swiglu_kernel(x_ref, g_ref, wg_ref, wu_ref, o_ref, hb_ref):
    @pl.when(pl.program_id(1) == 0)
    def _():
        x = x_ref[...]
        ms = jnp.mean(x * x, axis=-1, keepdims=True)
        hb_ref[...] = (x * lax.rsqrt(ms + EPS) * g_ref[...]).astype(BF16)

    h = hb_ref[...]
    gate = _dot(h, wg_ref[...].astype(BF16))
    up = _dot(h, wu_ref[...].astype(BF16))
    o_ref[...] = (gate * jax.nn.sigmoid(gate) * up).astype(BF16)


def _norm_swiglu(x, g, w, layer, tm, tf):
    m, d = x.shape
    f = w.shape[2] // 2
    nf = f // tf
    return pl.pallas_call(
        _norm_swiglu_kernel,
        grid=(m // tm, nf),
        in_specs=[pl.BlockSpec((tm, d), lambda i, j: (i, 0)),
                  pl.BlockSpec((1, d), lambda i, j: (0, 0)),
                  pl.BlockSpec((None, d, tf), lambda i, j: (layer, 0, j)),
                  pl.BlockSpec((None, d, tf), lambda i, j: (layer, 0, j + nf))],
        out_specs=pl.BlockSpec((tm, tf), lambda i, j: (i, j)),
        out_shape=jax.ShapeDtypeStruct((m, f), BF16),
        scratch_shapes=[pltpu.VMEM((tm, d), BF16)],
        compiler_params=_params(2),
        name="swiglu",
    )(x, g, w, w)


def _matmul_residual_kernel(a_ref, w_ref, r_ref, o_ref, wb_ref):
    @pl.when(pl.program_id(1) == 0)
    def _():
        wb_ref[...] = w_ref[...].astype(BF16)

    o_ref[...] = r_ref[...] + _dot(a_ref[...], wb_ref[...])


def _matmul_residual(a, w, layer, r, tm, tn, name):
    m, k = a.shape
    n = w.shape[2]
    return pl.pallas_call(
        _matmul_residual_kernel,
        grid=(n // tn, m // tm),
        in_specs=[pl.BlockSpec((tm, k), lambda j, i: (i, 0)),
                  pl.BlockSpec((None, k, tn), lambda j, i: (layer, 0, j), pipeline_mode=pl.Buffered(1)),
                  pl.BlockSpec((tm, tn), lambda j, i: (i, j))],
        out_specs=pl.BlockSpec((tm, tn), lambda j, i: (i, j)),
        out_shape=jax.ShapeDtypeStruct((m, n), F32),
        scratch_shapes=[pltpu.VMEM((k, tn), BF16)],
        compiler_params=_params(2),
        name=name,
    )(a, w, r)


def _merge_kernel(oa_ref, ob_ref, oc_ref, ga_ref, gb_ref, gc_ref, wa_ref, wb_ref, wc_ref, o_ref):
    ya = _dot(oa_ref[...], wa_ref[...].astype(BF16))
    yb = _dot(ob_ref[...], wb_ref[...].astype(BF16))
    yc = _dot(oc_ref[...], wc_ref[...].astype(BF16))
    merged = (jax.nn.sigmoid(ga_ref[...]) * ya + jax.nn.sigmoid(gb_ref[...]) * yb
              + jax.nn.sigmoid(gc_ref[...]) * yc)
    o_ref[...] = merged.astype(BF16)


def _merge(oa, ob, oc, z, w_branch, layer, tm, tn):
    m = oa.shape[0]
    n = D_MODEL
    nb = n // tn
    a_blocks = A_WIDTH // GLA_WIDTH
    return pl.pallas_call(
        _merge_kernel,
        grid=(m // tm, nb),
        in_specs=[pl.BlockSpec((tm, A_WIDTH), lambda i, j: (i, 0)),
                  pl.BlockSpec((tm, GLA_WIDTH), lambda i, j: (i, 0)),
                  pl.BlockSpec((tm, RET_WIDTH), lambda i, j: (i, 0)),
                  pl.BlockSpec((tm, tn), lambda i, j: (i, j)),
                  pl.BlockSpec((tm, tn), lambda i, j: (i, j + nb)),
                  pl.BlockSpec((tm, tn), lambda i, j: (i, j + 2 * nb)),
                  pl.BlockSpec((None, A_WIDTH, tn), lambda i, j: (layer, 0, j),
                               pipeline_mode=_resident(n == tn)),
                  pl.BlockSpec((None, GLA_WIDTH, tn), lambda i, j: (layer, a_blocks, j),
                               pipeline_mode=_resident(n == tn)),
                  pl.BlockSpec((None, RET_WIDTH, tn), lambda i, j: (layer, a_blocks + 1, j),
                               pipeline_mode=_resident(n == tn))],
        out_specs=pl.BlockSpec((tm, tn), lambda i, j: (i, j)),
        out_shape=jax.ShapeDtypeStruct((m, n), BF16),
        compiler_params=_params(2),
        name="merge",
    )(oa, ob, oc, z, z, z, w_branch, w_branch, w_branch)


def _head_rmsnorm(x, g):
    outs = []
    for h in range(A_HEADS):
        xh = x[:, h * A_HEAD_DIM:(h + 1) * A_HEAD_DIM]
        ms = jnp.mean(xh * xh, axis=-1, keepdims=True)
        outs.append(xh * lax.rsqrt(ms + EPS) * g)
    return jnp.concatenate(outs, axis=-1)


def _prep_a_kernel(aq_ref, ak_ref, av_ref, iq_ref, misc_ref, gq_ref, gk_ref,
                   qn_ref, kf_ref, kb_ref, vf_ref, vb_ref, iqm_ref, ik2_ref):
    qn_ref[...] = _head_rmsnorm(aq_ref[...], gq_ref[...]).astype(BF16)
    kn = _head_rmsnorm(ak_ref[...], gk_ref[...])
    kf_ref[...] = kn
    kb_ref[...] = kn.astype(BF16)
    v = av_ref[...]
    vf_ref[...] = v
    vb_ref[...] = v.astype(BF16)
    iq = iq_ref[...].astype(BF16)
    lane = lax.broadcasted_iota(jnp.int32, (iq.shape[0], LANES), 1)
    zero = jnp.zeros((iq.shape[0], LANES), BF16)
    for h in range(IDX_HEADS):
        pair = iq[:, (h // 2) * LANES:(h // 2 + 1) * LANES]
        keep = (lane < IDX_DIM) if h % 2 == 0 else (lane >= IDX_DIM)
        iqm_ref[:, h * LANES:(h + 1) * LANES] = jnp.where(keep, pair, zero)
    ik = misc_ref[...][:, :IDX_DIM].astype(BF16)
    ik2_ref[...] = jnp.concatenate([ik, ik], axis=-1)


def _prep_a(z, gq, gk, tm):
    m = z.shape[0]
    blk = COL_AQ // A_WIDTH
    wide = lambda c: pl.BlockSpec((tm, A_WIDTH), lambda i: (i, c))
    row = pl.BlockSpec((tm, A_WIDTH), lambda i: (i, 0))
    return pl.pallas_call(
        _prep_a_kernel,
        grid=(m // tm,),
        in_specs=[wide(blk), wide(blk + 1), wide(blk + 2), wide(blk + 3),
                  pl.BlockSpec((tm, LANES), lambda i: (i, COL_MISC // LANES)),
                  pl.BlockSpec((1, A_HEAD_DIM), lambda i: (0, 0)),
                  pl.BlockSpec((1, A_HEAD_DIM), lambda i: (0, 0))],
        out_specs=[row, row, row, row, row,
                   pl.BlockSpec((tm, IDX_HEADS * LANES), lambda i: (i, 0)),
                   pl.BlockSpec((tm, LANES), lambda i: (i, 0))],
        out_shape=[jax.ShapeDtypeStruct((m, A_WIDTH), BF16),
                   jax.ShapeDtypeStruct((m, A_WIDTH), F32),
                   jax.ShapeDtypeStruct((m, A_WIDTH), BF16),
                   jax.ShapeDtypeStruct((m, A_WIDTH), F32),
                   jax.ShapeDtypeStruct((m, A_WIDTH), BF16),
                   jax.ShapeDtypeStruct((m, IDX_HEADS * LANES), BF16),
                   jax.ShapeDtypeStruct((m, LANES), BF16)],
        compiler_params=_params(1),
        name="prep_a",
    )(z, z, z, z, z, gq, gk)


def _sortable_key(score):
    bits = pltpu.bitcast(score, jnp.int32)
    return bits ^ ((bits >> 31) & jnp.int32(0x7FFFFFFF))


def _topk_threshold(count_ge, rows, topk):
    def body(it, ans):
        cand = ans + lax.shift_left(jnp.int32(1), jnp.int32(31) - it)
        return jnp.where(count_ge(cand) >= float(topk), cand, ans)

    ans = lax.fori_loop(0, 32, body, jnp.full((rows, 1), INT_MIN, jnp.int32))
    return jnp.maximum(ans, jnp.int32(INT_MIN + 1))


def _tie_cut(tied_below, need, rows, n_bits):
    def body(it, cut):
        cand = cut | lax.shift_left(jnp.int32(1), jnp.int32(n_bits - 1) - it)
        return jnp.where(tied_below(cand) < need, cand, cut)

    return lax.fori_loop(0, n_bits, body, jnp.zeros((rows, 1), jnp.int32))


def _a_prompt_kernel(cfar_ref, qn_ref, iqm_ref, mq_ref, kb_ref, vb_ref, ik2_ref, band_ref,
                     o_ref, s_scr, w_scr, acc_scr, m_scr, l_scr, thr_scr, cut_scr, *, topk):
    i = pl.program_id(1)
    tq = qn_ref.shape[1]
    kt = tq
    scale = A_HEAD_DIM ** -0.5
    iw_scale = (IDX_HEADS * IDX_DIM) ** -0.5

    iw = mq_ref[0][:, MISC_IW:MISC_IW + IDX_HEADS] * iw_scale
    for h in range(IDX_HEADS):
        w_scr[h] = jnp.broadcast_to(iw[:, h:h + 1], (tq, LANES))

    def score_tile(j, diagonal):
        k0 = pl.multiple_of(j * kt, kt)
        ik = ik2_ref[0, pl.ds(k0, kt), :]
        acc = jnp.zeros((tq, kt), F32)
        for h in range(IDX_HEADS):
            x = _dot_nt(iqm_ref[0, :, h * LANES:(h + 1) * LANES], ik)
            acc = acc + jnp.maximum(x, 0.0) * jnp.tile(w_scr[h], (1, kt // LANES))
        key = _sortable_key(acc)
        if diagonal:
            r = lax.broadcasted_iota(jnp.int32, (tq, kt), 0)
            c = lax.broadcasted_iota(jnp.int32, (tq, kt), 1)
            key = jnp.where(c <= r, key, jnp.int32(INT_MIN))
        s_scr[:, pl.ds(k0, kt)] = key

    def score_body(j, carry):
        score_tile(j, False)
        return carry

    lax.fori_loop(0, i, score_body, 0)
    score_tile(i, True)

    for rc in range(tq // ROW_CHUNK):
        rows = slice(rc * ROW_CHUNK, (rc + 1) * ROW_CHUNK)

        def count_ge(t, rows=rows):
            tb = jnp.broadcast_to(t, (ROW_CHUNK, LANES))

            def add_tile(j, cnt):
                k0 = pl.multiple_of(j * kt, kt)
                keys = s_scr[rows, pl.ds(k0, kt)]
                for c in range(kt // LANES):
                    cnt = cnt + jnp.where(keys[:, c * LANES:(c + 1) * LANES] >= tb,
                                          jnp.int32(1), jnp.int32(0))
                return cnt

            def add_pair(jj, cnt):
                return add_tile(2 * jj + 1, add_tile(2 * jj, cnt))

            n_pairs = (i + 1) // 2
            cnt = lax.fori_loop(0, n_pairs, add_pair, jnp.zeros((ROW_CHUNK, LANES), jnp.int32))
            cnt = lax.fori_loop(2 * n_pairs, i + 1, add_tile, cnt)
            return jnp.sum(cnt.astype(F32), axis=-1, keepdims=True)

        thr = _topk_threshold(count_ge, ROW_CHUNK, topk)
        thr_scr[rows] = jnp.broadcast_to(thr, (ROW_CHUNK, LANES))
        cut_scr[rows] = jnp.full((ROW_CHUNK, LANES), s_scr.shape[1], jnp.int32)
        n_ge = count_ge(thr)

        @pl.when(jnp.max(n_ge) > float(topk))
        def _(rows=rows, thr=thr, n_ge=n_ge, count_ge=count_ge):
            need = float(topk) - count_ge(thr + 1)
            tb = jnp.broadcast_to(thr, (ROW_CHUNK, LANES))
            lane = lax.broadcasted_iota(jnp.int32, (ROW_CHUNK, LANES), 1)

            def tied_below(col_limit):
                cb = jnp.broadcast_to(col_limit, (ROW_CHUNK, LANES))

                def add_tile(j, cnt):
                    k0 = pl.multiple_of(j * kt, kt)
                    keys = s_scr[rows, pl.ds(k0, kt)]
                    for c in range(kt // LANES):
                        below = jnp.where(lane + (k0 + c * LANES) < cb, jnp.int32(1), jnp.int32(0))
                        cnt = cnt + jnp.where(keys[:, c * LANES:(c + 1) * LANES] == tb, below, jnp.int32(0))
                    return cnt

                cnt = lax.fori_loop(0, i + 1, add_tile, jnp.zeros((ROW_CHUNK, LANES), jnp.int32))
                return jnp.sum(cnt.astype(F32), axis=-1, keepdims=True)

            cut = _tie_cut(tied_below, need, ROW_CHUNK, (s_scr.shape[1] - 1).bit_length())
            cut = jnp.where(n_ge > float(topk), cut, s_scr.shape[1])
            cut_scr[rows] = jnp.broadcast_to(cut, (ROW_CHUNK, LANES))

    m_scr[...] = jnp.full(m_scr.shape, NEG_BIG, F32)
    l_scr[...] = jnp.zeros(l_scr.shape, F32)
    acc_scr[...] = jnp.zeros(acc_scr.shape, F32)
    c_qk = scale * LOG2_E

    def qk(k0, width, h, rows):
        hs = slice(h * A_HEAD_DIM, (h + 1) * A_HEAD_DIM)
        return _dot_nt(qn_ref[0, rows, hs], kb_ref[0, pl.ds(k0, width), hs])

    def mask_of(k0, width, rows):
        return pltpu.bitcast(s_scr[rows, pl.ds(k0, width)], F32)

    def lane_fold(x, op):
        out = x[:, :LANES]
        for c in range(1, x.shape[1] // LANES):
            out = op(out, x[:, c * LANES:(c + 1) * LANES])
        return out

    def max_tile(k0, width, band_off):
        thr_t = jnp.tile(thr_scr[...], (1, width // LANES))
        cut_t = jnp.tile(cut_scr[...], (1, width // LANES))
        keys = s_scr[:, pl.ds(k0, width)]
        col = lax.broadcasted_iota(jnp.int32, (tq, width), 1) + k0
        tied_out = jnp.where(keys == thr_t, jnp.where(col > cut_t, NEG_BIG, 0.0), 0.0)
        mask = jnp.where(keys >= thr_t, tied_out, NEG_BIG)
        s_scr[:, pl.ds(k0, width)] = pltpu.bitcast(mask, jnp.int32)
        for h in range(A_HEADS):
            for rc in range(tq // ATTN_ROWS):
                rows = slice(rc * ATTN_ROWS, (rc + 1) * ATTN_ROWS)
                if band_off is None:
                    top = lane_fold(qk(k0, width, h, rows) + mask_of(k0, width, rows),
                                    jnp.maximum) * c_qk + cfar_ref[h]
                else:
                    top = lane_fold(qk(k0, width, h, rows) * c_qk
                                    + band_ref[h, rows, band_off:band_off + width]
                                    + mask_of(k0, width, rows), jnp.maximum)
                m_scr[h, rows] = jnp.maximum(m_scr[h, rows], top)

    def sum_tile(k0, width, band_off):
        for h in range(A_HEADS):
            hs = slice(h * A_HEAD_DIM, (h + 1) * A_HEAD_DIM)
            for rc in range(tq // ATTN_ROWS):
                rows = slice(rc * ATTN_ROWS, (rc + 1) * ATTN_ROWS)
                if band_off is None:
                    shift = jnp.tile(cfar_ref[h] - m_scr[h, rows], (1, width // LANES))
                else:
                    shift = (band_ref[h, rows, band_off:band_off + width]
                             - jnp.tile(m_scr[h, rows], (1, width // LANES)))
                p = jnp.exp2(qk(k0, width, h, rows) * c_qk + shift + mask_of(k0, width, rows))
                l_scr[h, rows] = l_scr[h, rows] + lane_fold(p, jnp.add)
                acc_scr[rows, hs] = acc_scr[rows, hs] + _dot(p.astype(BF16),
                                                             vb_ref[0, pl.ds(k0, width), hs])

    def over_tiles(tile_fn):
        def far_body(j, carry):
            tile_fn(pl.multiple_of(j * kt, kt), kt, None)
            return carry

        lax.fori_loop(0, jnp.maximum(i - 1, 0), far_body, 0)

        @pl.when(i >= 1)
        def _():
            tile_fn(pl.multiple_of((i - 1) * kt, kt), kt, 0)

        tile_fn(pl.multiple_of(i * kt, kt), kt, kt)

    over_tiles(max_tile)
    for h in range(A_HEADS):
        m_scr[h] = jnp.broadcast_to(jnp.max(m_scr[h], axis=-1, keepdims=True), (tq, LANES))
    over_tiles(sum_tile)
    for h in range(A_HEADS):
        hs = slice(h * A_HEAD_DIM, (h + 1) * A_HEAD_DIM)
        o_ref[0, :, hs] = (acc_scr[:, hs] / jnp.sum(l_scr[h], axis=-1, keepdims=True)).astype(BF16)


def _rel_bucket(dist):
    n = jnp.maximum(dist, 0)
    max_exact = REL_BUCKETS // 2
    nf = jnp.maximum(n, 1).astype(F32)
    large = max_exact + (jnp.log(nf / max_exact) / math.log(REL_MAX_DIST / max_exact)
                         * (REL_BUCKETS - max_exact)).astype(jnp.int32)
    large = jnp.minimum(large, REL_BUCKETS - 1)
    return jnp.where(n < max_exact, n, large)


def _bias_band(rel_table, rows, kt):
    d = jnp.arange(rows)[:, None] + kt - jnp.arange(2 * kt)[None, :]
    hit = _rel_bucket(d)[None, :, :, None] == jnp.arange(REL_BUCKETS)
    return jnp.sum(jnp.where(hit, rel_table.T.astype(F32)[:, None, None, :], 0.0), axis=-1)


def _a_prompt(qn, iqm, z, kb, vb, ik2, band, cfar, batch, seq, tq):
    topk = min(TOPK_MAX, seq // 4)
    r3 = lambda a: a.reshape(batch, seq, a.shape[-1])
    qblk = lambda w: pl.BlockSpec((1, tq, w), lambda b, i: (b, i, 0))
    full = lambda w: pl.BlockSpec((1, seq, w), lambda b, i: (b, 0, 0), pipeline_mode=pl.Buffered(1))
    out = pl.pallas_call(
        functools.partial(_a_prompt_kernel, topk=topk),
        grid=(batch, seq // tq),
        in_specs=[pl.BlockSpec(memory_space=pltpu.SMEM),
                  qblk(A_WIDTH), qblk(IDX_HEADS * LANES),
                  pl.BlockSpec((1, tq, LANES), lambda b, i: (b, i, COL_MISC // LANES)),
                  full(A_WIDTH), full(A_WIDTH), full(LANES),
                  pl.BlockSpec((A_HEADS, tq, 2 * tq), lambda b, i: (0, 0, 0),
                               pipeline_mode=pl.Buffered(1))],
        out_specs=qblk(A_WIDTH),
        scratch_shapes=[pltpu.VMEM((tq, seq), jnp.int32),
                        pltpu.VMEM((IDX_HEADS, tq, LANES), F32),
                        pltpu.VMEM((tq, A_WIDTH), F32),
                        pltpu.VMEM((A_HEADS, tq, LANES), F32),
                        pltpu.VMEM((A_HEADS, tq, LANES), F32),
                        pltpu.VMEM((tq, LANES), jnp.int32),
                        pltpu.VMEM((tq, LANES), jnp.int32)],
        out_shape=jax.ShapeDtypeStruct((batch, seq, A_WIDTH), BF16),
        compiler_params=_params(2),
        name="a_prompt",
    )(cfar, r3(qn), r3(iqm), r3(z), r3(kb), r3(vb), r3(ik2), band)
    return out.reshape(batch * seq, A_WIDTH)


def _a_sample_score_kernel(pt_ref, iq_ref, w_ref, mnew_ref, *rest, pages_per_step, n_pages, topk, n_tok):
    page_refs = rest[:pages_per_step]
    keys_ref, thr_ref, cut_ref = rest[pages_per_step:]
    g = pl.program_id(1)
    rows = SAMPLE_ROWS
    iq = iq_ref[0]
    w = w_ref[0]

    def head_sum(x):
        return jnp.sum((jnp.maximum(x, 0.0) * w).reshape(IDX_HEADS, rows, PAGE_SIZE), axis=0)

    for p in range(pages_per_step):
        off = pl.multiple_of((g * pages_per_step + p) * PAGE_SIZE, PAGE_SIZE)
        keys_ref[0, :, pl.ds(off, PAGE_SIZE)] = _sortable_key(
            head_sum(_dot(iq, page_refs[p][...].astype(BF16))))

    @pl.when(g == pl.num_programs(1) - 1)
    def _():
        past = n_pages * PAGE_SIZE
        ik_new = mnew_ref[0][:, :IDX_DIM].astype(BF16)
        ik_new = jnp.concatenate([ik_new, jnp.zeros((PAGE_SIZE - rows, IDX_DIM), BF16)], axis=0)
        r = lax.broadcasted_iota(jnp.int32, (rows, PAGE_SIZE), 0)
        c = lax.broadcasted_iota(jnp.int32, (rows, PAGE_SIZE), 1)
        keys_ref[0, :, past:past + PAGE_SIZE] = jnp.where(
            c <= r, _sortable_key(head_sum(_dot_nt(iq, ik_new))), jnp.int32(INT_MIN))

        def count_ge(t):
            def body(j, cnt):
                off = pl.multiple_of(j * PAGE_SIZE, PAGE_SIZE)
                return cnt + jnp.where(keys_ref[0, :, pl.ds(off, PAGE_SIZE)] >= t,
                                       jnp.int32(1), jnp.int32(0))
            cnt = lax.fori_loop(0, n_pages + 1, body, jnp.zeros((rows, PAGE_SIZE), jnp.int32))
            return jnp.sum(cnt.astype(F32), axis=-1, keepdims=True)

        thr = _topk_threshold(count_ge, rows, topk)
        thr_ref[0] = jnp.broadcast_to(thr, (rows, LANES))

        n_cols = (n_pages + 1) * PAGE_SIZE
        cut_ref[0] = jnp.full((rows, LANES), n_cols, jnp.int32)
        real = lax.broadcasted_iota(jnp.int32, (rows, 1), 0) < n_tok
        n_ge = jnp.where(real, count_ge(thr), 0.0)

        @pl.when(jnp.max(n_ge) > float(topk))
        def _():
            need = float(topk) - count_ge(thr + 1)
            lane = lax.broadcasted_iota(jnp.int32, (rows, PAGE_SIZE), 1)

            def tied_below(col_limit):
                def body(j, cnt):
                    off = pl.multiple_of(j * PAGE_SIZE, PAGE_SIZE)
                    below = jnp.where(lane + off < col_limit, jnp.int32(1), jnp.int32(0))
                    return cnt + jnp.where(keys_ref[0, :, pl.ds(off, PAGE_SIZE)] == thr, below, jnp.int32(0))
                cnt = lax.fori_loop(0, n_pages + 1, body, jnp.zeros((rows, PAGE_SIZE), jnp.int32))
                return jnp.sum(cnt.astype(F32), axis=-1, keepdims=True)

            cut = _tie_cut(tied_below, need, rows, (n_cols - 1).bit_length())
            cut_ref[0] = jnp.broadcast_to(jnp.where(n_ge > float(topk), cut, n_cols), (rows, LANES))


def _a_sample_scores(page_table, iq_hq, w_hq, z3, cache_kidx, layer, n_tok, pages_per_step):
    batch, n_pages = page_table.shape
    rows = SAMPLE_ROWS
    nk = (n_pages + 1) * PAGE_SIZE
    topk = min(TOPK_MAX, (n_pages * PAGE_SIZE + n_tok) // 4)

    def page_spec(p):
        return pl.BlockSpec((None, None, IDX_DIM, PAGE_SIZE),
                            lambda b, g, pt: (layer, pt[b, g * pages_per_step + p], 0, 0))

    return pl.pallas_call(
        functools.partial(_a_sample_score_kernel, pages_per_step=pages_per_step, n_pages=n_pages,
                          topk=topk, n_tok=n_tok),
        grid_spec=pltpu.PrefetchScalarGridSpec(
            num_scalar_prefetch=1,
            grid=(batch, n_pages // pages_per_step),
            in_specs=[pl.BlockSpec((1, IDX_HEADS * rows, IDX_DIM), lambda b, g, pt: (b, 0, 0)),
                      pl.BlockSpec((1, IDX_HEADS * rows, 1), lambda b, g, pt: (b, 0, 0)),
                      pl.BlockSpec((1, rows, LANES), lambda b, g, pt: (b, 0, COL_MISC // LANES))]
                     + [page_spec(p) for p in range(pages_per_step)],
            out_specs=[pl.BlockSpec((1, rows, nk), lambda b, g, pt: (b, 0, 0)),
                       pl.BlockSpec((1, rows, LANES), lambda b, g, pt: (b, 0, 0)),
                       pl.BlockSpec((1, rows, LANES), lambda b, g, pt: (b, 0, 0))]),
        out_shape=[jax.ShapeDtypeStruct((batch, rows, nk), jnp.int32),
                   jax.ShapeDtypeStruct((batch, rows, LANES), jnp.int32),
                   jax.ShapeDtypeStruct((batch, rows, LANES), jnp.int32)],
        compiler_params=_params(2),
        name="a_sample_scores",
    )(page_table, iq_hq, w_hq, z3, *([cache_kidx.transpose(0, 1, 3, 2)] * pages_per_step))


def _a_sample_attn_kernel(pt_ref, qall_ref, keys_ref, thr_ref, cut_ref, knew_ref, vnew_ref, bias_far_ref, bias_last_ref,
                          bias_new_ref, expand_ref, *rest, pages_per_step, n_pages):
    k_refs = rest[:pages_per_step]
    v_refs = rest[pages_per_step:2 * pages_per_step]
    o_ref, acc_scr, m_scr, l_scr, lg_scr = rest[2 * pages_per_step:]
    g = pl.program_id(1)
    last_step = pl.num_programs(1) - 1
    rows = SAMPLE_ROWS
    hq = A_HEADS * rows
    cols = PAGE_SIZE * A_HEADS
    scale = A_HEAD_DIM ** -0.5
    thr = thr_ref[0]
    cut = cut_ref[0]
    lane = lax.broadcasted_iota(jnp.int32, (rows, PAGE_SIZE), 1)
    qall = qall_ref[0]

    @pl.when(g == 0)
    def _():
        m_scr[...] = jnp.full(m_scr.shape, NEG_BIG, F32)
        l_scr[...] = jnp.zeros(l_scr.shape, F32)
        acc_scr[...] = jnp.zeros(acc_scr.shape, F32)

    def masked_logits(key_tile, first_col, expand, kf, bias):
        tied_in = jnp.where(key_tile == thr, jnp.where(lane + first_col > cut, 0.0, 1.0), 1.0)
        hit = jnp.where(key_tile >= thr, tied_in, 0.0).astype(BF16)
        drop = (_dot(hit, expand) - 1.0) * (-NEG_BIG)
        return _dot_nt(qall, kf) * scale + bias + jnp.concatenate([drop] * A_HEADS, axis=0)

    def lane_fold(x, op):
        out = x[:, :LANES]
        for c in range(1, x.shape[1] // LANES):
            out = op(out, x[:, c * LANES:(c + 1) * LANES])
        return out

    def update(n_tiles, width, v_of):
        top = lane_fold(lg_scr[:, :n_tiles * width], jnp.maximum)
        m_old = m_scr[...]
        m_new = jnp.maximum(m_old, jnp.max(top, axis=-1, keepdims=True))
        alpha = jnp.exp(m_old - m_new)
        part = jnp.zeros((hq, LANES), F32)
        acc = alpha * acc_scr[...]
        for t in range(n_tiles):
            p = jnp.exp(lg_scr[:, t * width:(t + 1) * width] - m_new)
            part = part + lane_fold(p, jnp.add)
            acc = acc + _dot(p.astype(BF16), v_of(t))
        l_scr[...] = alpha * l_scr[...] + jnp.sum(part, axis=-1, keepdims=True)
        acc_scr[...] = acc
        m_scr[...] = m_new

    for p in range(pages_per_step):
        page = g * pages_per_step + p
        off = pl.multiple_of(page * PAGE_SIZE, PAGE_SIZE)
        bias = bias_far_ref[...]
        if p == pages_per_step - 1:
            bias = jnp.where(g == last_step, bias_last_ref[...], bias)
        lg_scr[:, p * cols:(p + 1) * cols] = masked_logits(
            keys_ref[0, :, pl.ds(off, PAGE_SIZE)], off, expand_ref[...],
            k_refs[p][...].reshape(cols, A_HEAD_DIM).astype(BF16), bias)
    update(pages_per_step, cols, lambda t: v_refs[t][...].reshape(cols, A_HEAD_DIM).astype(BF16))

    @pl.when(g == last_step)
    def _():
        past = n_pages * PAGE_SIZE
        lg_scr[:, :hq] = masked_logits(keys_ref[0, :, past:past + PAGE_SIZE], past, expand_ref[:, :hq],
                                       knew_ref[0], bias_new_ref[...])
        update(1, hq, lambda t: vnew_ref[0])
        o_ref[0] = (acc_scr[...] / l_scr[...]).astype(BF16)


def _a_sample_attn(page_table, qall, keys, thr, cut, knew, vnew, bias_far, bias_last, bias_new, expand,
                   cache_k, cache_v, layer, pages_per_step):
    batch, n_pages = page_table.shape
    rows = SAMPLE_ROWS
    hq = A_HEADS * rows
    nk = keys.shape[-1]

    def page_spec(p):
        return pl.BlockSpec((None, None, PAGE_SIZE, A_HEADS, A_HEAD_DIM),
                            lambda b, g, pt: (layer, pt[b, g * pages_per_step + p], 0, 0, 0))

    per_b = lambda r, w: pl.BlockSpec((1, r, w), lambda b, g, pt: (b, 0, 0))
    const = lambda r, w: pl.BlockSpec((r, w), lambda b, g, pt: (0, 0))
    return pl.pallas_call(
        functools.partial(_a_sample_attn_kernel, pages_per_step=pages_per_step, n_pages=n_pages),
        grid_spec=pltpu.PrefetchScalarGridSpec(
            num_scalar_prefetch=1,
            grid=(batch, n_pages // pages_per_step),
            in_specs=[per_b(hq, A_HEAD_DIM), per_b(rows, nk), per_b(rows, LANES), per_b(rows, LANES),
                      per_b(hq, A_HEAD_DIM), per_b(hq, A_HEAD_DIM),
                      const(hq, PAGE_SIZE * A_HEADS), const(hq, PAGE_SIZE * A_HEADS), const(hq, hq),
                      const(PAGE_SIZE, PAGE_SIZE * A_HEADS)]
                     + [page_spec(p) for p in range(pages_per_step)] * 2,
            out_specs=per_b(hq, A_HEAD_DIM),
            scratch_shapes=[pltpu.VMEM((hq, A_HEAD_DIM), F32),
                            pltpu.VMEM((hq, 1), F32),
                            pltpu.VMEM((hq, 1), F32),
                            pltpu.VMEM((hq, pages_per_step * PAGE_SIZE * A_HEADS), F32)]),
        out_shape=jax.ShapeDtypeStruct((batch, hq, A_HEAD_DIM), BF16),
        compiler_params=_params(2),
        name="a_sample_attn",
    )(page_table, qall, keys, thr, cut, knew, vnew, bias_far, bias_last, bias_new, expand,
      *([cache_k] * pages_per_step), *([cache_v] * pages_per_step))


def _pad_rows(x, rows):
    if x.shape[0] == rows:
        return x
    return jnp.concatenate([x, jnp.zeros((rows - x.shape[0], x.shape[1]), x.dtype)], axis=0)


def _split3(x):
    hi = x.astype(BF16)
    r1 = x - hi.astype(F32)
    mid = r1.astype(BF16)
    lo = (r1 - mid.astype(F32)).astype(BF16)
    return hi, mid, lo


def _gla_kernel(gq_ref, gk_ref, gv_ref, gg_ref, misc_ref, wa_ref, ba_ref, gn_ref, s0_ref,
                o_ref, sfin_ref, st_scr, oacc_scr, b_scr, q_scr, *, chunk, rows, n_valid):
    tt = gq_ref.shape[1]

    @pl.when(pl.program_id(1) == 0)
    def _():
        st_scr[...] = s0_ref[0]

    ga = _pad_rows(misc_ref[0][:, MISC_GA:MISC_GA + GLA_RANK], rows)
    x = _dot(ga.astype(BF16), wa_ref[...]) + ba_ref[...]
    la = (jnp.minimum(x, 0.0) - jnp.log1p(jnp.exp(-jnp.abs(x)))) * (1.0 / GLA_TAU)
    r = lax.broadcasted_iota(jnp.int32, (rows, rows), 0)
    c = lax.broadcasted_iota(jnp.int32, (rows, rows), 1)
    if n_valid < rows:
        rr = lax.broadcasted_iota(jnp.int32, la.shape, 0)
        la = jnp.where(rr < n_valid, la, 0.0)
    tri = jnp.where((c <= r) & (r // chunk == c // chunk), 1.0, 0.0).astype(BF16)
    hi, mid, lo = _split3(la)
    b = _dot(tri, hi) + _dot(tri, mid) + _dot(tri, lo)

    gq = _pad_rows(gq_ref[0], rows) * (GLA_DK ** -0.5)
    gk = _pad_rows(gk_ref[0], rows)
    if n_valid < rows:
        gk = jnp.where(rr < n_valid, gk, 0.0)
    gv = _pad_rows(gv_ref[0], rows)
    causal = (lax.broadcasted_iota(jnp.int32, (chunk, chunk), 1)
              <= lax.broadcasted_iota(jnp.int32, (chunk, chunk), 0))
    n_chunks = rows // chunk
    mid = chunk // 2 - 1

    def pieces(ci, h):
        return (slice(ci * chunk, (ci + 1) * chunk), slice(h * GLA_DK, (h + 1) * GLA_DK),
                slice(h * GLA_DV, (h + 1) * GLA_DV))

    spread = jnp.zeros((1, b.shape[1]), F32)
    for ci in range(n_chunks):
        r0 = ci * chunk
        spread = jnp.maximum(spread, jnp.maximum(b[r0:r0 + 1, :] - b[r0 + mid:r0 + mid + 1, :],
                                                 b[r0 + mid:r0 + mid + 1, :] - b[r0 + chunk - 1:r0 + chunk, :]))
    factored_ok = jnp.max(spread) <= GLA_MAX_FACTORED_EXPONENT

    def finish(ci, h, intra):
        rs, ks, vs = pieces(ci, h)
        bc = b[rs, ks]
        bend = bc[chunk - 1:chunk, :]
        qt = (gq[rs, ks] * jnp.exp(bc)).astype(BF16)
        kd = (gk[rs, ks] * jnp.exp(bend - bc)).astype(BF16)
        st = st_scr[h]
        oacc_scr[rs, vs] = intra + _dot_nt(qt, st.astype(BF16))
        st_scr[h] = st * jnp.exp(bend) + _dot_tn(gv[rs, vs].astype(BF16), kd)

    @pl.when(factored_ok)
    def _():
        for ci in range(n_chunks):
            for h in range(GLA_HEADS):
                rs, ks, vs = pieces(ci, h)
                bc = b[rs, ks]
                bmid = bc[mid:mid + 1, :]
                qm = (gq[rs, ks] * jnp.exp(bc - bmid)).astype(BF16)
                km = (gk[rs, ks] * jnp.exp(bmid - bc)).astype(BF16)
                sc = jnp.where(causal, _dot_nt(qm, km), 0.0)
                finish(ci, h, _dot(sc.astype(BF16), gv[rs, vs].astype(BF16)))

    @pl.when(jnp.logical_not(factored_ok))
    def _():
        b_scr[...] = b
        q_scr[...] = gq
        s_idx = lax.broadcasted_iota(jnp.int32, (chunk, 1), 0)
        for ci in range(n_chunks):
            for h in range(GLA_HEADS):
                rs, ks, vs = pieces(ci, h)
                bc = b[rs, ks]
                kc = gk[rs, ks]
                vc = gv[rs, vs]

                def row_group(g, carry, ci=ci, ks=ks, vs=vs, bc=bc, kc=kc, vc=vc):
                    base = pl.multiple_of(ci * chunk + g * SUBLANES, SUBLANES)
                    b_rows = b_scr[pl.ds(base, SUBLANES), :][:, ks]
                    q_rows = q_scr[pl.ds(base, SUBLANES), :][:, ks]
                    outs = []
                    for j in range(SUBLANES):
                        decay = jnp.exp(jnp.minimum(b_rows[j:j + 1] - bc, 0.0))
                        wgt = jnp.sum(kc * decay * q_rows[j:j + 1], axis=-1, keepdims=True)
                        wgt = jnp.where(s_idx <= g * SUBLANES + j, wgt, 0.0)
                        outs.append(jnp.sum(wgt * vc, axis=0, keepdims=True))
                    oacc_scr[pl.ds(base, SUBLANES), vs] = jnp.concatenate(outs, axis=0)
                    return carry

                lax.fori_loop(0, chunk // SUBLANES, row_group, 0)
                finish(ci, h, oacc_scr[rs, vs])

    gg = gg_ref[0]
    for h in range(GLA_HEADS):
        vs = slice(h * GLA_DV, (h + 1) * GLA_DV)
        oh = oacc_scr[0:tt, vs]
        ms = jnp.mean(oh * oh, axis=-1, keepdims=True)
        g = gg[:, vs]
        o_ref[0, :, vs] = (oh * lax.rsqrt(ms + EPS) * gn_ref[...] * (g * jax.nn.sigmoid(g))).astype(BF16)

    @pl.when(pl.program_id(1) == pl.num_programs(1) - 1)
    def _():
        sfin_ref[0] = st_scr[...]


def _gla(z3, wa, ba, gn, s0t, tt, chunk, rows, n_valid):
    batch, seq, _ = z3.shape
    q_blk = COL_GQ // 256
    v_blk = COL_GV // GLA_WIDTH
    tok = lambda w, cblk: pl.BlockSpec((1, tt, w), lambda b, t: (b, t, cblk))
    const = lambda shape: pl.BlockSpec(shape, lambda b, t: (0,) * len(shape))
    st_spec = pl.BlockSpec((1, GLA_HEADS, GLA_DV, GLA_DK), lambda b, t: (b, 0, 0, 0))
    return pl.pallas_call(
        functools.partial(_gla_kernel, chunk=chunk, rows=rows, n_valid=n_valid),
        grid=(batch, seq // tt),
        in_specs=[tok(256, q_blk), tok(256, q_blk + 1), tok(GLA_WIDTH, v_blk), tok(GLA_WIDTH, v_blk + 1),
                  tok(LANES, COL_MISC // LANES),
                  const((GLA_RANK, GLA_HEADS * GLA_DK)), const((1, GLA_HEADS * GLA_DK)),
                  const((1, GLA_DV)), st_spec],
        out_specs=[pl.BlockSpec((1, tt, GLA_WIDTH), lambda b, t: (b, t, 0)), st_spec],
        out_shape=[jax.ShapeDtypeStruct((batch, seq, GLA_WIDTH), BF16),
                   jax.ShapeDtypeStruct((batch, GLA_HEADS, GLA_DV, GLA_DK), F32)],
        scratch_shapes=[pltpu.VMEM((GLA_HEADS, GLA_DV, GLA_DK), F32),
                        pltpu.VMEM((rows, GLA_WIDTH), F32),
                        pltpu.VMEM((rows, GLA_HEADS * GLA_DK), F32),
                        pltpu.VMEM((rows, GLA_HEADS * GLA_DK), F32)],
        compiler_params=_params(2),
        name="gla",
    )(z3, z3, z3, z3, z3, wa, ba, gn, s0t)


def _ret_kernel(rq_ref, rk_ref, rv_ref, rg_ref, cos_ref, sin_ref, gn_ref, s0_ref,
                o_ref, sfin_ref, st_scr, oacc_scr, *, rows, n_valid, log_gamma):
    tt = rq_ref.shape[1]
    width = RET_HEADS * RET_DK

    @pl.when(pl.program_id(1) == 0)
    def _():
        st_scr[...] = s0_ref[0]

    lane = lax.broadcasted_iota(jnp.int32, (rows, width), 1)
    first_half = (lane % RET_DK) < (RET_DK // 2)
    cos = _pad_rows(cos_ref[...], rows)
    sin = _pad_rows(sin_ref[...], rows)

    def rope(x):
        rot = jnp.where(first_half, pltpu.roll(x, width - RET_DK // 2, axis=1),
                        pltpu.roll(x, RET_DK // 2, axis=1))
        return x * cos + rot * sin

    q = rope(_pad_rows(rq_ref[0], rows))
    k = rope(_pad_rows(rk_ref[0], rows)) * (RET_DK ** -0.5)
    v = _pad_rows(rv_ref[0], rows)
    rr = lax.broadcasted_iota(jnp.int32, (rows, RET_DK), 0)
    if n_valid < rows:
        k = jnp.where(lax.broadcasted_iota(jnp.int32, k.shape, 0) < n_valid, k, 0.0)
    t_i = lax.broadcasted_iota(jnp.int32, (rows, rows), 0)
    s_i = lax.broadcasted_iota(jnp.int32, (rows, rows), 1)
    dist = (t_i - s_i).astype(F32)
    pos1 = (rr + 1).astype(F32)
    rem = (n_valid - 1 - rr).astype(F32)

    for h in range(RET_HEADS):
        ks = slice(h * RET_DK, (h + 1) * RET_DK)
        vs = slice(h * RET_DV, (h + 1) * RET_DV)
        lg = log_gamma[h]
        decay = jnp.where(t_i >= s_i, jnp.exp(dist * lg), 0.0)
        qh = q[:, ks]
        kh = k[:, ks]
        vh = v[:, vs].astype(BF16)
        st = st_scr[h]
        sc = _dot_nt(qh.astype(BF16), kh.astype(BF16)) * decay
        q_in = (qh * jnp.exp(pos1 * lg)).astype(BF16)
        oacc_scr[:, vs] = _dot(sc.astype(BF16), vh) + _dot_nt(q_in, st.astype(BF16))
        kd = (kh * jnp.exp(rem * lg)).astype(BF16)
        st_scr[h] = st * math.exp(n_valid * lg) + _dot_tn(vh, kd)

    rg = rg_ref[0]
    for h in range(RET_HEADS):
        vs = slice(h * RET_DV, (h + 1) * RET_DV)
        oh = oacc_scr[0:tt, vs]
        oc = oh - jnp.mean(oh, axis=-1, keepdims=True)
        var = jnp.mean(oc * oc, axis=-1, keepdims=True)
        g = rg[:, vs]
        o_ref[0, :, vs] = (oc * lax.rsqrt(var + EPS) * gn_ref[...] * (g * jax.nn.sigmoid(g))).astype(BF16)

    @pl.when(pl.program_id(1) == pl.num_programs(1) - 1)
    def _():
        sfin_ref[0] = st_scr[...]


def _ret(z3, cos, sin, gn, s0t, tt, rows, n_valid):
    batch, seq, _ = z3.shape
    q_blk = COL_GQ // 256 + 2
    v_blk = COL_GV // RET_WIDTH + 2
    width = RET_HEADS * RET_DK
    log_gamma = tuple(float(np.log1p(-np.exp2(np.float32(-5.0 - h)), dtype=np.float32))
                      for h in range(RET_HEADS))
    tok = lambda w, cblk: pl.BlockSpec((1, tt, w), lambda b, t: (b, t, cblk))
    st_spec = pl.BlockSpec((1, RET_HEADS, RET_DV, RET_DK), lambda b, t: (b, 0, 0, 0))
    return pl.pallas_call(
        functools.partial(_ret_kernel, rows=rows, n_valid=n_valid, log_gamma=log_gamma),
        grid=(batch, seq // tt),
        in_specs=[tok(256, q_blk), tok(256, q_blk + 1), tok(RET_WIDTH, v_blk), tok(RET_WIDTH, v_blk + 1),
                  pl.BlockSpec((tt, width), lambda b, t: (t, 0)),
                  pl.BlockSpec((tt, width), lambda b, t: (t, 0)),
                  pl.BlockSpec((1, RET_DV), lambda b, t: (0, 0)), st_spec],
        out_specs=[pl.BlockSpec((1, tt, RET_WIDTH), lambda b, t: (b, t, 0)), st_spec],
        out_shape=[jax.ShapeDtypeStruct((batch, seq, RET_WIDTH), BF16),
                   jax.ShapeDtypeStruct((batch, RET_HEADS, RET_DV, RET_DK), F32)],
        scratch_shapes=[pltpu.VMEM((RET_HEADS, RET_DV, RET_DK), F32),
                        pltpu.VMEM((rows, RET_WIDTH), F32)],
        compiler_params=_params(2),
        name="ret",
    )(z3, z3, z3, z3, cos, sin, gn, s0t)


def _rope_tables(pos):
    half = RET_DK // 2
    freqs = ROPE_BASE ** (-jnp.arange(half, dtype=F32) / half)
    ang = pos.astype(F32)[:, None] * freqs[None, :]
    cos = jnp.cos(ang)
    sin = jnp.sin(ang)
    cos_t = jnp.tile(jnp.concatenate([cos, cos], axis=-1), (1, RET_HEADS))
    sin_t = jnp.tile(jnp.concatenate([-sin, sin], axis=-1), (1, RET_HEADS))
    return cos_t, sin_t


def _permuted_segments():
    names = ("aq", "ak", "av", "iq", "ik", "iw", "gq", "gk", "gv", "ga", "gg", "rq", "rk", "rv", "rg", "gates")
    src, acc = {}, 0
    for name, size in zip(names, IN_SIZES):
        src[name] = (acc, size)
        acc += size
    order = ("gates", "aq", "ak", "av", "iq", "gv", "gg", "rv", "rg", "gq", "gk", "rq", "rk", "ik", "iw", "ga")
    out, dst = [], 0
    for name in order:
        out.append((src[name][0], dst, src[name][1]))
        dst += src[name][1]
    return out, dst


def _permute_w_in_kernel(w_ref, o_ref):
    segments, used = _permuted_segments()
    for src, dst, size in segments:
        o_ref[dst:dst + size, :] = w_ref[src:src + size, :].astype(BF16)
    o_ref[used:, :] = jnp.zeros((IN_PADDED - used, o_ref.shape[1]), BF16)


def _permute_w_in(w_in_t, layer):
    _, n, d = w_in_t.shape
    tc = LANES
    return pl.pallas_call(
        _permute_w_in_kernel,
        grid=(d // tc,),
        in_specs=[pl.BlockSpec((None, n, tc), lambda j: (layer, 0, j))],
        out_specs=pl.BlockSpec((IN_PADDED, tc), lambda j: (0, j)),
        out_shape=jax.ShapeDtypeStruct((IN_PADDED, d), BF16),
        compiler_params=_params(1),
        name="permute_w_in",
    )(w_in_t)


def _dense_tail(x, z, oa, ob, oc, w):
    m = x.shape[0]
    layer = w["layer"]
    merged = _merge(oa, ob, oc, z, w["w_branch"], layer, min(m, 256), D_MODEL)
    x = _matmul_residual(merged, w["w_out"], layer, x, min(m, 512), D_MODEL, "out_proj")
    act = _norm_swiglu(x, w["norm_ffn"], w["w_ffn_in"], layer, min(m, 1024), 512)
    return _matmul_residual(act, w["w_ffn_out"], layer, x, min(m, 1024), 512, "ffn_out")


def _prompt_layer(x, w, batch, seq, band, cfar, cos, sin):
    tm = 512
    z = _norm_matmul(x, w["norm_mix"], w["w_in"], tm, IN_PROJ_COLS)
    qn, kf, kb, vf, vb, iqm, ik2 = _prep_a(z, w["a_q_norm"], w["a_k_norm"], tm)
    oa = _a_prompt(qn, iqm, z, kb, vb, ik2, band * LOG2_E, cfar * LOG2_E, batch, seq, 256)
    z3 = z.reshape(batch, seq, IN_PADDED)
    zero_state = jnp.zeros((batch, GLA_HEADS, GLA_DV, GLA_DK), F32)
    ob, s_gla = _gla(z3, w["gla_wa"], w["gla_ba"], w["gla_norm"], zero_state, 256, CHUNK, 256, 256)
    oc, s_ret = _ret(z3, cos, sin, w["ret_norm"], zero_state, 256, 256, 256)
    x = _dense_tail(x, z, oa, ob.reshape(-1, GLA_WIDTH), oc.reshape(-1, RET_WIDTH), w)
    kidx = z3[:, :, COL_MISC:COL_MISC + IDX_DIM]
    new = (kf.reshape(batch, seq, A_HEADS, A_HEAD_DIM), vf.reshape(batch, seq, A_HEADS, A_HEAD_DIM),
           kidx, s_gla.transpose(0, 1, 3, 2), s_ret.transpose(0, 1, 3, 2))
    return x, new


def _sample_layer(x, w, layer, batch, n_tok, page_table, cache_k, cache_v, cache_kidx,
                  s_gla0, s_ret0, bias_s, expand, cos, sin):
    rows = SAMPLE_ROWS
    hq = A_HEADS * rows
    tm = batch * rows
    z = _norm_matmul(x, w["norm_mix"], w["w_in"], tm, IN_PROJ_COLS)
    qn, kf, kb, vf, vb, _, _ = _prep_a(z, w["a_q_norm"], w["a_k_norm"], tm)
    z3 = z.reshape(batch, rows, IN_PADDED)
    iq = z3[:, :, COL_AQ + 3 * A_WIDTH:COL_AQ + 4 * A_WIDTH].reshape(batch, rows, IDX_HEADS, IDX_DIM)
    iq_hq = iq.transpose(0, 2, 1, 3).reshape(batch, IDX_HEADS * rows, IDX_DIM).astype(BF16)
    iw = z3[:, :, COL_MISC + MISC_IW:COL_MISC + MISC_IW + IDX_HEADS] * ((IDX_HEADS * IDX_DIM) ** -0.5)
    w_hq = iw.transpose(0, 2, 1).reshape(batch, IDX_HEADS * rows, 1)
    keys, thr, cut = _a_sample_scores(page_table, iq_hq, w_hq, z3, cache_kidx, layer, n_tok, 32)
    qall = qn.reshape(batch, rows, A_HEADS, A_HEAD_DIM).transpose(0, 2, 1, 3).reshape(batch, hq, A_HEAD_DIM)
    oa = _a_sample_attn(page_table, qall, keys, thr, cut, kb.reshape(batch, hq, A_HEAD_DIM),
                        vb.reshape(batch, hq, A_HEAD_DIM), *bias_s, expand, cache_k, cache_v, layer, 8)
    oa = oa.reshape(batch, A_HEADS, rows, A_HEAD_DIM).transpose(0, 2, 1, 3).reshape(tm, A_WIDTH)
    ob, s_gla = _gla(z3, w["gla_wa"], w["gla_ba"], w["gla_norm"], s_gla0.transpose(0, 1, 3, 2),
                     rows, LANES, LANES, n_tok)
    oc, s_ret = _ret(z3, cos, sin, w["ret_norm"], s_ret0.transpose(0, 1, 3, 2), rows, LANES, n_tok)
    x = _dense_tail(x, z, oa, ob.reshape(tm, GLA_WIDTH), oc.reshape(tm, RET_WIDTH), w)
    kidx = z3[:, :n_tok, COL_MISC:COL_MISC + IDX_DIM]
    new = (kf.reshape(batch, rows, A_HEADS, A_HEAD_DIM)[:, :n_tok],
           vf.reshape(batch, rows, A_HEADS, A_HEAD_DIM)[:, :n_tok],
           kidx, s_gla.transpose(0, 1, 3, 2), s_ret.transpose(0, 1, 3, 2))
    return x, new


def _sample_bias(rel_table):
    rows = SAMPLE_ROWS
    hq = A_HEADS * rows
    band = _bias_band(rel_table, rows, PAGE_SIZE)
    cfar_rows = jnp.repeat(rel_table[REL_BUCKETS - 1].astype(F32), rows)[:, None]
    band_last = jnp.repeat(band[:, :, :PAGE_SIZE].reshape(hq, PAGE_SIZE), A_HEADS, axis=1)
    band_new = jnp.repeat(band[:, :, PAGE_SIZE:PAGE_SIZE + rows].reshape(hq, rows), A_HEADS, axis=1)
    cols = PAGE_SIZE * A_HEADS
    other_head = (jnp.arange(hq)[:, None] // rows) != (jnp.arange(cols)[None, :] % A_HEADS)
    head_mask = jnp.where(other_head, NEG_BIG, 0.0).astype(F32)
    return cfar_rows + head_mask, band_last + head_mask, band_new + head_mask[:, :hq]


def kernel(x_prompt, x_sample, cache_k, cache_v, cache_kidx, state_gla, state_ret, page_table, rel_table, w_in, a_q_norm, a_k_norm, gla_wa, gla_ba, gla_norm, ret_norm, w_branch, w_out, norm_mix, norm_ffn, w_ffn_in, w_ffn_out):
    depth = w_in.shape[0]
    bp, tp, d = x_prompt.shape
    bs, ts, _ = x_sample.shape
    past = page_table.shape[1] * PAGE_SIZE
    tq = 256
    row = lambda a: a.reshape(1, -1).astype(F32)

    cfar = rel_table[REL_BUCKETS - 1].astype(F32)
    band_p = _bias_band(rel_table, tq, tq)
    bias_s = _sample_bias(rel_table)
    expand = jnp.asarray(np.kron(np.eye(PAGE_SIZE), np.ones((1, A_HEADS))), dtype=BF16)
    cos_p, sin_p = _rope_tables(jnp.arange(tp))
    cos_s, sin_s = _rope_tables(past + jnp.arange(SAMPLE_ROWS))

    w_in_t = jnp.swapaxes(w_in, 1, 2)
    xp = x_prompt.reshape(bp * tp, d)
    xs = jnp.pad(x_sample, ((0, 0), (0, SAMPLE_ROWS - ts), (0, 0))).reshape(bs * SAMPLE_ROWS, d)
    rows_p, rows_s = [], []
    for l in range(depth):
        w = dict(layer=l, w_in=_permute_w_in(w_in_t, l), a_q_norm=row(a_q_norm[l]), a_k_norm=row(a_k_norm[l]),
                 gla_wa=gla_wa[l].astype(BF16), gla_ba=row(gla_ba[l]), gla_norm=row(gla_norm[l]),
                 ret_norm=row(ret_norm[l]), w_branch=w_branch, w_out=w_out,
                 norm_mix=row(norm_mix[l]), norm_ffn=row(norm_ffn[l]), w_ffn_in=w_ffn_in,
                 w_ffn_out=w_ffn_out)
        xp, new_p = _prompt_layer(xp, w, bp, tp, band_p, cfar, cos_p, sin_p)
        xs, new_s = _sample_layer(xs, w, l, bs, ts, page_table, cache_k, cache_v, cache_kidx,
                                  state_gla[l], state_ret[l], bias_s, expand, cos_s, sin_s)
        rows_p.append(new_p)
        rows_s.append(new_s)
    outs_p = [jnp.stack(r) for r in zip(*rows_p)]
    outs_s = [jnp.stack(r) for r in zip(*rows_s)]
    y_p = xp.reshape(bp, tp, d)
    y_s = xs.reshape(bs, SAMPLE_ROWS, d)[:, :ts]
    return (y_p, y_s, *outs_p, *outs_s)
```
